```python
import jax, jax.numpy as jnp
from jax import lax
import numpy as np

D_MODEL = 1024
BATCH = 4
SEQ = 4096
DEPTH = 4
DEC_BATCH = 128
DEC_SEQ = 8
PAST_LEN = 2048
PAGE_SIZE = 128

HEAD_DIM = 64
D_MIX = D_MODEL
A_WIDTH = D_MIX // 4
B_WIDTH = D_MIX // 4
C_WIDTH = D_MIX - A_WIDTH - B_WIDTH
A_HEADS = A_WIDTH // HEAD_DIM
B_HEADS = B_WIDTH // HEAD_DIM
C_HEADS = C_WIDTH // HEAD_DIM
C_KV_HEADS = 2
C_GROUP = C_HEADS // C_KV_HEADS
RWKV_LORA_W = 64
RWKV_LORA_A = 64
RWKV_LORA_G = 128
MLSTM_CONV = 4
MLSTM_CHUNK = 64
CMP_BLOCK = 32
CMP_STRIDE = 16
SEL_BLOCK = 64
SEL_TOP = 16
WINDOW = 512
Q_BLK = 128
D_FF = 2752
FFN_CONV = 3
ALPHA = (2 * DEPTH) ** 0.25
BETA = (8 * DEPTH) ** -0.25
FORCE_BONUS = 1e4
NEG_INF = -1e30

A_SPLITS = (A_WIDTH, A_WIDTH, A_WIDTH, RWKV_LORA_W, RWKV_LORA_A, RWKV_LORA_G)
B_SPLITS = (B_WIDTH, B_WIDTH, B_WIDTH, B_HEADS, B_HEADS, B_WIDTH)
C_SPLITS = (C_WIDTH, 6 * C_KV_HEADS * HEAD_DIM, 3 * C_HEADS)
A_PROJ = sum(A_SPLITS)
B_PROJ = sum(B_SPLITS)
C_PROJ = sum(C_SPLITS)
D_IN = A_PROJ + B_PROJ + C_PROJ

kernel_name = "hymba_rwkv7_mlstm_nsa_decoder_step"

F32 = jnp.float32


def _split(x, widths):
    offs = [0]
    for w in widths:
        offs.append(offs[-1] + int(w))
    return [x[..., offs[i]:offs[i + 1]] for i in range(len(widths))]


def _layernorm(x, g, b, eps=1e-5):
    xf = x.astype(F32)
    mu = jnp.mean(xf, -1, keepdims=True)
    var = jnp.mean(jnp.square(xf - mu), -1, keepdims=True)
    return (xf - mu) * lax.rsqrt(var + eps) * g + b


def _headnorm(h, eps):
    mu = jnp.mean(h, -1, keepdims=True)
    var = jnp.mean(jnp.square(h - mu), -1, keepdims=True)
    return (h - mu) * lax.rsqrt(var + eps)


def _causal_dwconv(u, buf, w, b):
    K = w.shape[0]
    T = u.shape[1]
    full = jnp.concatenate([buf.astype(u.dtype), u], axis=1)
    y = b + sum(full[:, j:j + T] * w[j] for j in range(K))
    return y, full[:, T:]


def _masked_softmax(s, mask):
    s = jnp.where(mask, s, NEG_INF)
    return jax.nn.softmax(s, axis=-1) * mask


def _rwkv7(pa, shift_prev, S0, W, l):
    B_, T, _ = pa.shape
    prev = jnp.concatenate([shift_prev[:, None].astype(pa.dtype), pa[:, :-1]], axis=1)
    pm = pa + (prev - pa) * W["rwkv_mu"][l]
    r, k, v, wd, ad, gd = _split(pm, A_SPLITS)
    w = -jax.nn.softplus(-(W["rwkv_w0"][l] + jnp.tanh(wd) @ W["rwkv_w2"][l])) - 0.5
    a = jax.nn.sigmoid(W["rwkv_a0"][l] + ad @ W["rwkv_a2"][l])
    g = jax.nn.sigmoid(gd) @ W["rwkv_g2"][l]
    hd = lambda z: z.reshape(B_, T, A_HEADS, HEAD_DIM).astype(F32)
    kk = hd(k * W["rwkv_k_k"][l])
    kk = kk / jnp.maximum(jnp.sqrt(jnp.sum(kk * kk, -1, keepdims=True)), 1e-12)
    k = k * (1 + (a - 1) * W["rwkv_k_a"][l])
    r_h, k_h, v_h, a_h = hd(r), hd(k), hd(v), hd(a)
    decay = jnp.exp(-jnp.exp(hd(w)))

    def step(S, inp):
        r_t, d_t, k_t, v_t, ka_t, kb_t = inp
        sa = jnp.einsum('bhij,bhj->bhi', S, ka_t)
        S = S * d_t[:, :, None, :] + sa[..., None] * kb_t[:, :, None, :] + v_t[..., None] * k_t[:, :, None, :]
        return S, jnp.einsum('bhij,bhj->bhi', S, r_t)

    xs = tuple(z.swapaxes(0, 1) for z in (r_h, decay, k_h, v_h, -kk, kk * a_h))
    S_T, y = lax.scan(step, S0.astype(F32), xs)
    y = y.swapaxes(0, 1)
    y = _headnorm(y, 64e-5) * W["rwkv_ln_g"][l].reshape(A_HEADS, HEAD_DIM) + W["rwkv_ln_b"][l].reshape(A_HEADS, HEAD_DIM)
    y = y + jnp.sum(r_h * k_h * W["rwkv_r_k"][l], -1, keepdims=True) * v_h
    out = y.reshape(B_, T, A_WIDTH) * g
    return out.astype(pa.dtype), pa[:, -1], S_T


def _mlstm_chunkwise(q, k, v, ig, lf, C0, n0, m0):
    B_, T, H, dh = q.shape
    L = MLSTM_CHUNK if T % MLSTM_CHUNK == 0 else T
    nC = T // L
    chunk = lambda a: a.reshape(B_, nC, L, *a.shape[2:]).swapaxes(0, 1)
    causal = jnp.tril(jnp.ones((L, L), bool))[None, :, :, None]

    def step(carry, inp):
        C, n, m = carry
        qc, kc, vc, ic, fc = inp
        b = jnp.cumsum(fc, axis=1)
        D = jnp.where(causal, b[:, :, None] - b[:, None] + ic[:, None], -jnp.inf)
        inter = b + m[:, None]
        mt = jnp.maximum(inter, D.max(axis=2))
        qk = jnp.einsum('bthd,bshd->btsh', qc, kc) * jnp.exp(D - mt[:, :, None])
        iw = jnp.exp(inter - mt)
        num = jnp.einsum('btsh,bshd->bthd', qk, vc) + iw[..., None] * jnp.einsum('bhij,bthj->bthi', C, qc)
        den = qk.sum(axis=2) + iw * jnp.einsum('bhj,bthj->bth', n, qc)
        h = num / jnp.maximum(jnp.abs(den), jnp.exp(-mt))[..., None]
        bL = b[:, -1]
        gs = bL[:, None] - b + ic
        m_new = jnp.maximum(bL + m, gs.max(axis=1))
        ws = jnp.exp(gs - m_new[:, None])
        wc = jnp.exp(bL + m - m_new)
        C_new = wc[..., None, None] * C + jnp.einsum('bsh,bshi,bshj->bhij', ws, vc, kc)
        n_new = wc[..., None] * n + jnp.einsum('bsh,bshj->bhj', ws, kc)
        return (C_new, n_new, m_new), h

    (C, n, m), h = lax.scan(step, (C0.astype(F32), n0.astype(F32), m0.astype(F32)),
                            tuple(chunk(a) for a in (q, k, v, ig, lf)))
    return h.swapaxes(0, 1).reshape(B_, T, H, dh), C, n, m


def _mlstm(pb, conv_buf, C0, n0, m0, W, l):
    B_, T, _ = pb.shape
    q, k, v, ig, fg, o = _split(pb, B_SPLITS)
    qk, conv_new = _causal_dwconv(jnp.concatenate([q, k], -1), conv_buf, W["mlstm_conv_w"][l], W["mlstm_conv_b"][l])
    q, k = jnp.split(jax.nn.silu(qk), 2, axis=-1)
    hd = lambda z: z.reshape(B_, T, B_HEADS, HEAD_DIM).astype(F32)
    q, k, v = hd(q), hd(k) * HEAD_DIM ** -0.5, hd(v)
    ig = (ig + W["mlstm_i_b"][l]).astype(F32)
    lf = jax.nn.log_sigmoid((fg + W["mlstm_f_b"][l]).astype(F32))
    h, C, n, m = _mlstm_chunkwise(q, k, v, ig, lf, C0, n0, m0)
    h = _headnorm(h, 1e-5) * W["mlstm_norm_g"][l].reshape(B_HEADS, HEAD_DIM)
    out = jax.nn.sigmoid(o.astype(F32)) * h.reshape(B_, T, B_WIDTH)
    return out.astype(pb.dtype), conv_new, C, n, m


def _compress(kv, pool_w, cmp_w):
    B_, Lp = kv.shape[:2]
    r = CMP_BLOCK // CMP_STRIDE
    n_str = Lp // CMP_STRIDE
    n_c = n_str - r + 1
    ch = kv.reshape(B_, n_str, CMP_STRIDE, 2, C_KV_HEADS, HEAD_DIM).astype(F32)
    pw = pool_w.reshape(r, CMP_STRIDE, 2, C_KV_HEADS, HEAD_DIM)
    pooled = sum(jnp.einsum('bnskgd,skgd->bnkgd', ch[:, i:i + n_c], pw[i]) for i in range(r))
    return jnp.einsum('bnkgd,kgde->bnkge', pooled, cmp_w)


def _nsa(pc, past_kv, win_buf, W, l):
    B_, T, _ = pc.shape
    P = past_kv.shape[1]
    Wb = win_buf.shape[1]
    G, R, dh = C_KV_HEADS, C_GROUP, HEAD_DIM
    q, kv, gl = _split(pc, C_SPLITS)
    q = q.reshape(B_, T, G, R, dh)
    kv = kv.reshape(B_, T, 6, G, dh)
    gates = jax.nn.sigmoid((gl + W["nsa_gate_b"][l]).astype(F32)).reshape(B_, T, G, R, 3)
    new_rows = kv[:, :, :4]
    L = P + T
    Lp = -(-L // SEL_BLOCK) * SEL_BLOCK
    full = jnp.concatenate([past_kv.astype(kv.dtype), new_rows], axis=1)
    full = jnp.pad(full, ((0, 0), (0, Lp - L), (0, 0), (0, 0), (0, 0)))
    kc = _compress(full[:, :, 0:2], W["nsa_pool_w"][l], W["nsa_cmp_w"][l])
    n_c = kc.shape[1]
    n_s = Lp // SEL_BLOCK
    n_sel = min(SEL_TOP, n_s)
    sk = full[:, :, 2].reshape(B_, n_s, SEL_BLOCK, G, dh).transpose(0, 3, 1, 2, 4)
    sv = full[:, :, 3].reshape(B_, n_s, SEL_BLOCK, G, dh).transpose(0, 3, 1, 2, 4)
    c_start = jnp.arange(n_c) * CMP_STRIDE
    s_start = jnp.arange(n_s) * SEL_BLOCK
    M = ((c_start[:, None] < s_start[None, :] + SEL_BLOCK) &
         (c_start[:, None] + CMP_BLOCK > s_start[None, :])).astype(F32)
    win_all = jnp.concatenate([win_buf.astype(kv.dtype), kv[:, :, 4:]], axis=1)
    win_state = win_all[:, -min(WINDOW, Wb + T):]
    win_pad = jnp.pad(win_all, ((0, 0), (WINDOW - Wb, 0), (0, 0), (0, 0), (0, 0)))
    gather = jax.vmap(jax.vmap(lambda a, i: a[i]))
    qblk = Q_BLK if T % Q_BLK == 0 else 1
    n_qb = T // qblk
    qs = q.reshape(B_, n_qb, qblk, G, R, dh).swapaxes(0, 1)
    gs = gates.reshape(B_, n_qb, qblk, G, R, 3).swapaxes(0, 1)
    scale = dh ** -0.5

    def one_block(args):
        bi, qb, gb = args
        i_loc = bi * qblk + jnp.arange(qblk)
        t = P + i_loc
        qf = qb.astype(F32) * scale
        s_c = jnp.einsum('btgrd,bngd->btgrn', qf, kc[:, :, 0])
        c_mask = ((c_start + CMP_BLOCK - 1)[None, :] <= t[:, None])[None, :, None, None, :]
        p_c = _masked_softmax(s_c, c_mask)
        o_c = jnp.einsum('btgrn,bngd->btgrd', p_c, kc[:, :, 1])
        imp = jnp.einsum('btgrn,ns->btgs', p_c, M)
        blk = jnp.arange(n_s)
        cur = t // SEL_BLOCK
        valid = blk[None, :] * SEL_BLOCK <= t[:, None]
        forced = (blk[None, :] == 0) | (blk[None, :] == cur[:, None]) | (blk[None, :] == cur[:, None] - 1)
        score = jnp.where(valid[None, :, None, :], imp + jnp.where(forced, FORCE_BONUS, 0.0)[None, :, None, :], NEG_INF)
        _, idx = lax.top_k(score, n_sel)
        idx_g = idx.transpose(0, 2, 1, 3).reshape(B_, G, qblk * n_sel)
        ks = gather(sk, idx_g).reshape(B_, G, qblk, n_sel * SEL_BLOCK, dh).astype(F32)
        vs = gather(sv, idx_g).reshape(B_, G, qblk, n_sel * SEL_BLOCK, dh).astype(F32)
        spos = (idx[..., None] * SEL_BLOCK + jnp.arange(SEL_BLOCK)).reshape(B_, qblk, G, n_sel * SEL_BLOCK)
        s_mask = (spos <= t[None, :, None, None])[:, :, :, None, :]
        p_s = _masked_softmax(jnp.einsum('btgrd,bgtkd->btgrk', qf, ks), s_mask)
        o_s = jnp.einsum('btgrk,bgtkd->btgrd', p_s, vs)
        kw = lax.dynamic_slice_in_dim(win_pad, bi * qblk, WINDOW + qblk, axis=1).astype(F32)
        j = bi * qblk + jnp.arange(WINDOW + qblk)
        rel = j - WINDOW
        w_mask = ((j[None, :] >= WINDOW - Wb) & (rel[None, :] <= i_loc[:, None]) &
                  (rel[None, :] > i_loc[:, None] - WINDOW))[None, :, None, None, :]
        p_w = _masked_softmax(jnp.einsum('btgrd,bjgd->btgrj', qf, kw[:, :, 0]), w_mask)
        o_w = jnp.einsum('btgrj,bjgd->btgrd', p_w, kw[:, :, 1])
        o = gb[..., 0:1] * o_c + gb[..., 1:2] * o_s + gb[..., 2:3] * o_w
        return o.astype(qb.dtype)

    o = lax.map(one_block, (jnp.arange(n_qb), qs, gs))
    o = o.swapaxes(0, 1).reshape(B_, T, C_WIDTH)
    return o, new_rows, win_state


def _ffn(h, buf, W, l):
    ug, uv = jnp.split(h @ W["ffn_up"][l], 2, axis=-1)
    ug, buf_new = _causal_dwconv(ug, buf, W["ffn_conv_w"][l], W["ffn_conv_b"][l])
    return (jax.nn.silu(ug) * uv) @ W["ffn_down"][l], buf_new


def _trunk(x, c, st, nsa_past, W):
    names = ("nsa_kv", "win_kv", "rwkv", "rwkv_shift", "mlstm_C", "mlstm_n", "mlstm_m", "mlstm_conv", "ffn_conv")
    new = {nm: [] for nm in names}
    for l in range(DEPTH):
        mod = jax.nn.silu(c) @ W["ada_w"][l] + W["ada_b"][l]
        sh1, sc1, g1, sh2, sc2, g2 = jnp.split(mod[:, None, :], 6, axis=-1)
        h = x * (1 + sc1) + sh1
        pa, pb, pc = _split(h @ W["w_in"][l], (A_PROJ, B_PROJ, C_PROJ))
        ya, a_shift, a_S = _rwkv7(pa, st["rwkv_shift"][l], st["rwkv"][l], W, l)
        yb, b_conv, b_C, b_n, b_m = _mlstm(pb, st["mlstm_conv"][l], st["mlstm_C"][l], st["mlstm_n"][l], st["mlstm_m"][l], W, l)
        yc, c_rows, c_win = _nsa(pc, nsa_past(l), st["win_kv"][l], W, l)
        y = jnp.concatenate([ya, yb, yc], axis=-1) @ W["w_out"][l]
        x = _layernorm(ALPHA * x + (1 + g1) * y, W["ln_g"][l, 0], W["ln_b"][l, 0])
        h = x * (1 + sc2) + sh2
        y, f_conv = _ffn(h, st["ffn_conv"][l], W, l)
        x = _layernorm(ALPHA * x + (1 + g2) * y, W["ln_g"][l, 1], W["ln_b"][l, 1])
        for nm, val in zip(names, (c_rows, c_win, a_S, a_shift, b_C, b_n, b_m, b_conv, f_conv)):
            new[nm].append(val)
    return x, {nm: jnp.stack(v) for nm, v in new.items()}


def setup_inputs(seed: int = 0) -> dict:
    key = jax.random.key(seed)
    ks = iter(jax.random.split(key, 64))
    nk = lambda: next(ks)
    nrm = lambda shape, scale=1.0: scale * jax.random.normal(nk(), shape, F32)
    G, dh = C_KV_HEADS, HEAD_DIM
    n_pages = PAST_LEN // PAGE_SIZE
    n_pool = (DEC_BATCH * n_pages * 5) // 4
    win_buf = min(WINDOW, PAST_LEN)
    page_table = jax.random.permutation(nk(), n_pool)[: DEC_BATCH * n_pages].reshape(DEC_BATCH, n_pages).astype(jnp.int32)
    return {
        "x_prompt": nrm((BATCH, SEQ, D_MODEL)),
        "x_sample": nrm((DEC_BATCH, DEC_SEQ, D_MODEL)),
        "c_prompt": nrm((BATCH, D_MODEL)),
        "c_sample": nrm((DEC_BATCH, D_MODEL)),
        "cache_nsa_kv": nrm((DEPTH, n_pool, PAGE_SIZE, 4, G, dh)),
        "cache_win_kv": nrm((DEPTH, DEC_BATCH, win_buf, 2, G, dh)),
        "state_rwkv": nrm((DEPTH, DEC_BATCH, A_HEADS, dh, dh), 0.5),
        "state_rwkv_shift": nrm((DEPTH, DEC_BATCH, A_PROJ)),
        "state_mlstm_C": nrm((DEPTH, DEC_BATCH, B_HEADS, dh, dh), 0.5),
        "state_mlstm_n": nrm((DEPTH, DEC_BATCH, B_HEADS, dh)),
        "state_mlstm_m": nrm((DEPTH, DEC_BATCH, B_HEADS)),
        "state_mlstm_conv": nrm((DEPTH, DEC_BATCH, MLSTM_CONV - 1, 2 * B_WIDTH)),
        "state_ffn_conv": nrm((DEPTH, DEC_BATCH, FFN_CONV - 1, D_FF)),
        "page_table": page_table,
        "w_in": nrm((DEPTH, D_MODEL, D_IN), D_MODEL ** -0.5),
        "w_out": nrm((DEPTH, D_MIX, D_MODEL), BETA * D_MIX ** -0.5),
        "ada_w": nrm((DEPTH, D_MODEL, 6 * D_MODEL), 0.5 * D_MODEL ** -0.5),
        "ada_b": nrm((DEPTH, 6 * D_MODEL), 0.01),
        "ln_g": 1.0 + nrm((DEPTH, 2, D_MODEL), 0.02),
        "ln_b": nrm((DEPTH, 2, D_MODEL), 0.02),
        "rwkv_mu": jax.random.uniform(nk(), (DEPTH, A_PROJ), F32),
        "rwkv_w0": jnp.linspace(-6.5, -1.5, A_WIDTH, dtype=F32) + nrm((DEPTH, A_WIDTH), 0.1),
        "rwkv_w2": nrm((DEPTH, RWKV_LORA_W, A_WIDTH), 0.5 * RWKV_LORA_W ** -0.5),
        "rwkv_a0": nrm((DEPTH, A_WIDTH), 0.1),
        "rwkv_a2": nrm((DEPTH, RWKV_LORA_A, A_WIDTH), RWKV_LORA_A ** -0.5),
        "rwkv_g2": nrm((DEPTH, RWKV_LORA_G, A_WIDTH), RWKV_LORA_G ** -0.5),
        "rwkv_k_k": 0.85 + nrm((DEPTH, A_WIDTH), 0.05),
        "rwkv_k_a": 1.0 + nrm((DEPTH, A_WIDTH), 0.05),
        "rwkv_r_k": nrm((DEPTH, A_HEADS, dh), 0.1),
        "rwkv_ln_g": 1.0 + nrm((DEPTH, A_WIDTH), 0.02),
        "rwkv_ln_b": nrm((DEPTH, A_WIDTH), 0.02),
        "mlstm_conv_w": nrm((DEPTH, MLSTM_CONV, 2 * B_WIDTH), MLSTM_CONV ** -0.5),
        "mlstm_conv_b": nrm((DEPTH, 2 * B_WIDTH), 0.02),
        "mlstm_i_b": nrm((DEPTH, B_HEADS), 0.1),
        "mlstm_f_b": jnp.linspace(3.0, 6.0, B_HEADS, dtype=F32) + nrm((DEPTH, B_HEADS), 0.1),
        "mlstm_norm_g": 1.0 + nrm((DEPTH, B_WIDTH), 0.02),
        "nsa_pool_w": (1.0 + nrm((DEPTH, CMP_BLOCK, 2, G, dh), 0.1)) / CMP_BLOCK,
        "nsa_cmp_w": nrm((DEPTH, 2, G, dh, dh), dh ** -0.5),
        "nsa_gate_b": nrm((DEPTH, 3 * C_HEADS), 0.1),
        "ffn_up": nrm((DEPTH, D_MODEL, 2 * D_FF), D_MODEL ** -0.5),
        "ffn_conv_w": nrm((DEPTH, FFN_CONV, D_FF), FFN_CONV ** -0.5),
        "ffn_conv_b": nrm((DEPTH, D_FF), 0.02),
        "ffn_down": nrm((DEPTH, D_FF, D_MODEL), BETA * D_FF ** -0.5),
    }


def reference(x_prompt, x_sample, c_prompt, c_sample, cache_nsa_kv, cache_win_kv, state_rwkv, state_rwkv_shift,
              state_mlstm_C, state_mlstm_n, state_mlstm_m, state_mlstm_conv, state_ffn_conv, page_table,
              w_in, w_out, ada_w, ada_b, ln_g, ln_b, rwkv_mu, rwkv_w0, rwkv_w2, rwkv_a0, rwkv_a2, rwkv_g2,
              rwkv_k_k, rwkv_k_a, rwkv_r_k, rwkv_ln_g, rwkv_ln_b, mlstm_conv_w, mlstm_conv_b, mlstm_i_b,
              mlstm_f_b, mlstm_norm_g, nsa_pool_w, nsa_cmp_w, nsa_gate_b, ffn_up, ffn_conv_w, ffn_conv_b, ffn_down):
    W = dict(w_in=w_in, w_out=w_out, ada_w=ada_w, ada_b=ada_b, ln_g=ln_g, ln_b=ln_b, rwkv_mu=rwkv_mu,
             rwkv_w0=rwkv_w0, rwkv_w2=rwkv_w2, rwkv_a0=rwkv_a0, rwkv_a2=rwkv_a2, rwkv_g2=rwkv_g2,
             rwkv_k_k=rwkv_k_k, rwkv_k_a=rwkv_k_a, rwkv_r_k=rwkv_r_k, rwkv_ln_g=rwkv_ln_g, rwkv_ln_b=rwkv_ln_b,
             mlstm_conv_w=mlstm_conv_w, mlstm_conv_b=mlstm_conv_b, mlstm_i_b=mlstm_i_b, mlstm_f_b=mlstm_f_b,
             mlstm_norm_g=mlstm_norm_g, nsa_pool_w=nsa_pool_w, nsa_cmp_w=nsa_cmp_w, nsa_gate_b=nsa_gate_b,
             ffn_up=ffn_up, ffn_conv_w=ffn_conv_w, ffn_conv_b=ffn_conv_b, ffn_down=ffn_down)
    G, dh = C_KV_HEADS, HEAD_DIM
    Bp = x_prompt.shape[0]
    Bs = x_sample.shape[0]
    st_p = dict(
        rwkv=jnp.zeros((DEPTH, Bp, A_HEADS, dh, dh), F32),
        rwkv_shift=jnp.zeros((DEPTH, Bp, A_PROJ), F32),
        mlstm_C=jnp.zeros((DEPTH, Bp, B_HEADS, dh, dh), F32),
        mlstm_n=jnp.zeros((DEPTH, Bp, B_HEADS, dh), F32),
        mlstm_m=jnp.zeros((DEPTH, Bp, B_HEADS), F32),
        mlstm_conv=jnp.zeros((DEPTH, Bp, MLSTM_CONV - 1, 2 * B_WIDTH), F32),
        ffn_conv=jnp.zeros((DEPTH, Bp, FFN_CONV - 1, D_FF), F32),
        win_kv=jnp.zeros((DEPTH, Bp, 0, 2, G, dh), F32))
    past_p = lambda l: jnp.zeros((Bp, 0, 4, G, dh), x_prompt.dtype)
    st_s = dict(rwkv=state_rwkv, rwkv_shift=state_rwkv_shift, mlstm_C=state_mlstm_C, mlstm_n=state_mlstm_n,
                mlstm_m=state_mlstm_m, mlstm_conv=state_mlstm_conv, ffn_conv=state_ffn_conv, win_kv=cache_win_kv)
    past_s = lambda l: cache_nsa_kv[l][page_table].reshape(Bs, -1, 4, G, dh)
    y_prompt, new_p = _trunk(x_prompt, c_prompt, st_p, past_p, W)
    y_sample, new_s = _trunk(x_sample, c_sample, st_s, past_s, W)
    return (y_prompt, y_sample,
            new_p["nsa_kv"], new_s["nsa_kv"],
            new_p["win_kv"], new_s["win_kv"],
            new_p["rwkv"], new_s["rwkv"],
            new_p["rwkv_shift"], new_s["rwkv_shift"],
            new_p["mlstm_C"], new_s["mlstm_C"],
            new_p["mlstm_n"], new_s["mlstm_n"],
            new_p["mlstm_m"], new_s["mlstm_m"],
            new_p["mlstm_conv"], new_s["mlstm_conv"],
            new_p["ffn_conv"], new_s["ffn_conv"])
```

```python
import functools
import math

import numpy as np
import jax
import jax.numpy as jnp
from jax import lax
from jax.experimental import pallas as pl
from jax.experimental.pallas import tpu as pltpu

F32 = jnp.float32
BF16 = jnp.bfloat16

D_MODEL = 1024
DEPTH = 4
HEAD_DIM = 64
A_WIDTH = 256
B_WIDTH = 256
C_WIDTH = 512
A_HEADS = 4
B_HEADS = 4
C_HEADS = 8
C_KV_HEADS = 2
C_GROUP = 4
PAGE_SIZE = 128
MLSTM_CONV = 4
CMP_BLOCK = 32
CMP_STRIDE = 16
SEL_BLOCK = 64
SEL_TOP = 16
WINDOW = 512
D_FF = 2752
D_FF_PAD = 2816
FFN_CONV = 3
ALPHA = (2 * DEPTH) ** 0.25
FORCE_BONUS = 1e4
NEG_INF = -1e30
LANES = 128
SUBLANES = 8
ROWS = 256
VMEM_LIMIT = 56 * 1024 * 1024

NN = (((1,), (0,)), ((), ()))
NT = (((1,), (1,)), ((), ()))


def _dot(a, b, dn=NN):
    return lax.dot_general(a, b, dn, preferred_element_type=F32)


def _split2(a):
    hi = a.astype(BF16)
    lo = (a - hi.astype(F32)).astype(BF16)
    return hi, lo


def _mm(a, b, passes=1, dn=NN):
    if passes == 1:
        return _dot(a.astype(BF16), b.astype(BF16), dn)
    ah, al = _split2(a)
    bh, bl = _split2(b)
    return _dot(ah, bh, dn) + (_dot(al, bh, dn) + _dot(ah, bl, dn))


def _mm_sel(sel, x, dn=NN):
    s = sel.astype(BF16)
    x1 = x.astype(BF16)
    r1 = x - x1.astype(F32)
    x2 = r1.astype(BF16)
    x3 = (r1 - x2.astype(F32)).astype(BF16)
    return _dot(s, x1, dn) + (_dot(s, x2, dn) + _dot(s, x3, dn))


def _mm_xsel(x, sel, dn=NN):
    s = sel.astype(BF16)
    x1 = x.astype(BF16)
    r1 = x - x1.astype(F32)
    x2 = r1.astype(BF16)
    x3 = (r1 - x2.astype(F32)).astype(BF16)
    return _dot(x1, s, dn) + (_dot(x2, s, dn) + _dot(x3, s, dn))


def _sigmoid(x):
    return 1.0 / (1.0 + jnp.exp(-x))


def _silu(x):
    return x * _sigmoid(x)


def _softplus(x):
    return jnp.maximum(x, 0.0) + jnp.log(1.0 + jnp.exp(-jnp.abs(x)))


def _log_sigmoid(x):
    return -_softplus(-x)


def _iota(shape, axis):
    return lax.broadcasted_iota(jnp.int32, shape, axis)


def _block_masks(rows, chunk):
    sh = int(math.log2(chunk))
    r = _iota((rows, rows), 0)
    s = _iota((rows, rows), 1)
    same = jnp.right_shift(r, sh) == jnp.right_shift(s, sh)
    return same, same & (s <= r), same & (s < r)


def _head_ones(width):
    r = _iota((width, width), 0)
    s = _iota((width, width), 1)
    return (jnp.right_shift(r, 6) == jnp.right_shift(s, 6)).astype(F32)


def _expand_mat(rows, chunk, nseq):
    sh = int(math.log2(chunk))
    r = _iota((rows, nseq * HEAD_DIM), 0)
    c = _iota((rows, nseq * HEAD_DIM), 1)
    return ((jnp.right_shift(c, 6) == jnp.right_shift(r, sh)) & ((r & (chunk - 1)) == 0)).astype(F32)


def _seq_lane_mask(rows, chunk, nseq):
    sh = int(math.log2(chunk))
    r = _iota((rows, nseq * HEAD_DIM), 0)
    c = _iota((rows, nseq * HEAD_DIM), 1)
    return jnp.right_shift(c, 6) == jnp.right_shift(r, sh)


def _fold_mat(nseq):
    r = _iota((nseq * HEAD_DIM, HEAD_DIM), 0)
    c = _iota((nseq * HEAD_DIM, HEAD_DIM), 1)
    return ((r & (HEAD_DIM - 1)) == c).astype(F32)


def _shifted_rows(pre, cur, nshift):
    bb, L, C = cur.shape
    full = jnp.concatenate([pre, cur], axis=1).reshape(bb * (L + SUBLANES), C)
    out = []
    for k in range(1, nshift + 1):
        sh = pltpu.roll(full, k, axis=0).reshape(bb, L + SUBLANES, C)
        out.append(sh[:, SUBLANES:, :])
    return out


def _layernorm(z, g, b):
    mu = jnp.mean(z, axis=-1, keepdims=True)
    zc = z - mu
    var = jnp.mean(zc * zc, axis=-1, keepdims=True)
    return zc * lax.rsqrt(var + 1e-5) * g + b


def _cparams(sem):
    return pltpu.CompilerParams(dimension_semantics=sem, vmem_limit_bytes=VMEM_LIMIT)


def _ada_kernel(c_ref, w_ref, b_ref, o_ref):
    c = c_ref[...]
    o_ref[...] = _mm(_silu(c), w_ref[...], 3) + b_ref[...]


def _ada(c_all, ada_w, ada_b):
    nb = c_all.shape[0]
    tn = 1536
    return pl.pallas_call(
        _ada_kernel,
        grid=(DEPTH, 6 * D_MODEL // tn),
        in_specs=[
            pl.BlockSpec((nb, D_MODEL), lambda l, n: (0, 0)),
            pl.BlockSpec((None, D_MODEL, tn), lambda l, n: (l, 0, n)),
            pl.BlockSpec((None, 1, tn), lambda l, n: (l, 0, n)),
        ],
        out_specs=pl.BlockSpec((None, nb, tn), lambda l, n: (l, 0, n)),
        out_shape=jax.ShapeDtypeStruct((DEPTH, nb, 6 * D_MODEL), F32),
        compiler_params=_cparams(("arbitrary", "arbitrary")),
        name="ada_mod",
    )(c_all, ada_w, ada_b.reshape(DEPTH, 1, 6 * D_MODEL))


def _modmm_kernel(x_ref, sh_ref, sc_ref, w_ref, *o_refs, splits, bf16_outs):
    x = x_ref[...]
    bb, L, D = x.shape
    h = (x * (1.0 + sc_ref[...]) + sh_ref[...]).reshape(bb * L, D).astype(BF16)
    o = jnp.dot(h, w_ref[...], preferred_element_type=F32)
    for (a, b), o_ref, as_bf16 in zip(splits, o_refs, bf16_outs):
        piece = o[:, a:b].reshape(bb, L, b - a)
        o_ref[...] = piece.astype(BF16) if as_bf16 else piece


def _modmm(x, mod, sh_col, sc_col, w, splits, bf16_outs, bb, L, name):
    B, T, D = x.shape
    N = w.shape[1]
    kern = functools.partial(_modmm_kernel, splits=splits, bf16_outs=bf16_outs)
    return pl.pallas_call(
        kern,
        grid=(B // bb, T // L),
        in_specs=[
            pl.BlockSpec((bb, L, D), lambda i, j: (i, j, 0)),
            pl.BlockSpec((bb, 1, D), lambda i, j: (i, 0, sh_col)),
            pl.BlockSpec((bb, 1, D), lambda i, j: (i, 0, sc_col)),
            pl.BlockSpec((D, N), lambda i, j: (0, 0)),
        ],
        out_specs=[pl.BlockSpec((bb, L, b - a), lambda i, j: (i, j, 0)) for a, b in splits],
        out_shape=[jax.ShapeDtypeStruct((B, T, b - a), BF16 if q else F32) for (a, b), q in zip(splits, bf16_outs)],
        compiler_params=_cparams(("arbitrary", "arbitrary")),
        name=name,
    )(x, mod, mod, w)


def _outproj_kernel(ya_ref, yb_ref, yc_ref, x_ref, g_ref, w_ref, lg_ref, lb_ref, o_ref):
    x = x_ref[...]
    bb, L, D = x.shape
    rows = bb * L
    ya = ya_ref[...].reshape(rows, A_WIDTH).astype(BF16)
    yb = yb_ref[...].reshape(rows, B_WIDTH).astype(BF16)
    yc = yc_ref[...].reshape(rows, C_WIDTH).astype(BF16)
    y = (jnp.dot(ya, w_ref[0:A_WIDTH, :], preferred_element_type=F32)
         + jnp.dot(yb, w_ref[A_WIDTH:A_WIDTH + B_WIDTH, :], preferred_element_type=F32)
         + jnp.dot(yc, w_ref[A_WIDTH + B_WIDTH:, :], preferred_element_type=F32))
    z = ALPHA * x + (1.0 + g_ref[...]) * y.reshape(bb, L, D)
    o_ref[...] = _layernorm(z, lg_ref[...], lb_ref[...])


def _outproj(ya, yb, yc, x, mod, w_out, ln_g, ln_b, bb, L):
    B, T, D = x.shape
    blk = lambda w: pl.BlockSpec((bb, L, w), lambda i, j: (i, j, 0))
    return pl.pallas_call(
        _outproj_kernel,
        grid=(B // bb, T // L),
        in_specs=[
            blk(A_WIDTH), blk(B_WIDTH), blk(C_WIDTH), blk(D),
            pl.BlockSpec((bb, 1, D), lambda i, j: (i, 0, 2)),
            pl.BlockSpec((D, D), lambda i, j: (0, 0)),
            pl.BlockSpec((1, D), lambda i, j: (0, 0)),
            pl.BlockSpec((1, D), lambda i, j: (0, 0)),
        ],
        out_specs=blk(D),
        out_shape=jax.ShapeDtypeStruct((B, T, D), F32),
        compiler_params=_cparams(("arbitrary", "arbitrary")),
        name="outproj_ln",
    )(ya, yb, yc, x, mod, w_out, ln_g, ln_b)


def _ffn_down_kernel(ug_ref, uv_ref, halo_ref, st_ref, x_ref, g_ref, cw_ref, cb_ref, w_ref, lg_ref, lb_ref, o_ref):
    ug = ug_ref[...]
    bb, L, N = ug.shape
    first = pl.program_id(1) == 0
    pre = jnp.where(first, st_ref[...], halo_ref[...])
    u1, u2 = _shifted_rows(pre, ug, FFN_CONV - 1)
    cw = cw_ref[...]
    conv = cb_ref[...] + ug * cw[2:3, :] + u1 * cw[1:2, :] + u2 * cw[0:1, :]
    a = (_silu(conv) * uv_ref[...]).reshape(bb * L, N).astype(BF16)
    y = jnp.dot(a, w_ref[...], preferred_element_type=F32)
    x = x_ref[...]
    z = ALPHA * x + (1.0 + g_ref[...]) * y.reshape(x.shape)
    o_ref[...] = _layernorm(z, lg_ref[...], lb_ref[...])


def _ffn_down(u, st8, x, mod, conv_w, conv_b, w_down, ln_g, ln_b, bb, L):
    B, T, D = x.shape
    N = D_FF_PAD
    lb8 = L // SUBLANES
    return pl.pallas_call(
        _ffn_down_kernel,
        grid=(B // bb, T // L),
        in_specs=[
            pl.BlockSpec((bb, L, N), lambda i, j: (i, j, 0)),
            pl.BlockSpec((bb, L, N), lambda i, j: (i, j, 1)),
            pl.BlockSpec((bb, SUBLANES, N), lambda i, j: (i, jnp.maximum(j * lb8 - 1, 0), 0)),
            pl.BlockSpec((bb, SUBLANES, N), lambda i, j: (i, 0, 0)),
            pl.BlockSpec((bb, L, D), lambda i, j: (i, j, 0)),
            pl.BlockSpec((bb, 1, D), lambda i, j: (i, 0, 5)),
            pl.BlockSpec((SUBLANES, N), lambda i, j: (0, 0)),
            pl.BlockSpec((1, N), lambda i, j: (0, 0)),
            pl.BlockSpec((N, D), lambda i, j: (0, 0)),
            pl.BlockSpec((1, D), lambda i, j: (0, 0)),
            pl.BlockSpec((1, D), lambda i, j: (0, 0)),
        ],
        out_specs=pl.BlockSpec((bb, L, D), lambda i, j: (i, j, 0)),
        out_shape=jax.ShapeDtypeStruct((B, T, D), F32),
        compiler_params=_cparams(("arbitrary", "arbitrary")),
        name="ffn_down_ln",
    )(u, u, u, st8, x, mod, conv_w, conv_b, w_down, ln_g, ln_b)


def _neumann_solve(n_mat, y, chunk, passes):
    p = n_mat
    y = y + _mm(p, y, passes)
    for _ in range(int(math.log2(chunk)) - 1):
        p = _mm(p, p, passes)
        y = y + _mm(p, y, passes)
    return y


def _rwkv_kernel(pa_ref, halo_ref, sh_ref, st0_ref, mu_ref, pv_ref, lw_ref, ya_ref, sto_ref, st_ref, *, chunk):
    c = pl.program_id(1)
    bb, L, _ = pa_ref.shape
    R = bb * L
    W = A_WIDTH

    @pl.when(c == 0)
    def _():
        st_ref[...] = st0_ref[...]

    pa = pa_ref[...]
    pre = jnp.where(c == 0, sh_ref[...], halo_ref[...])
    (prev,) = _shifted_rows(pre, pa, 1)
    x = pa.reshape(R, 4 * W)
    pm = x + (prev.reshape(R, 4 * W) - x) * mu_ref[...]
    r = pm[:, 0:W]
    k = pm[:, W:2 * W]
    v = pm[:, 2 * W:3 * W]
    lo = pm[:, 3 * W:4 * W]
    lane = _iota((R, W), 1)
    z = jnp.where(lane < 64, jnp.tanh(lo), jnp.where(lane < 128, lo, _sigmoid(lo)))
    lora = _mm(z, lw_ref[...], 3)
    pv = pv_ref[...]
    w0, a0, k_k, k_a, r_k, ln_g, ln_b = (pv[i:i + 1, :] for i in range(7))
    w = -_softplus(-(w0 + lora[:, 0:W])) - 0.5
    a = _sigmoid(a0 + lora[:, W:2 * W])
    g = lora[:, 2 * W:3 * W]
    ones_h = _head_ones(W)
    kk = k * k_k
    kk = kk / jnp.maximum(jnp.sqrt(_mm_xsel(kk * kk, ones_h)), 1e-12)
    k2 = k * (1.0 + (a - 1.0) * k_a)
    lw = -jnp.exp(w)
    same, incl, strict = _block_masks(R, L)
    cum = _mm_sel(incl.astype(F32), lw)
    tot = _mm_sel(same.astype(F32), lw)
    e_neg = jnp.exp(-cum)
    e_rem = jnp.exp(tot - cum)
    kb = kk * a
    a_t = -kk * jnp.exp(cum - lw)
    b_t = kb * e_neg
    k_t = k2 * e_neg
    r_t = r * jnp.exp(cum)
    bh_t = jnp.transpose(kb * e_rem)
    kh_t = jnp.transpose(k2 * e_rem)
    gam_t = jnp.transpose(jnp.exp(tot))
    expand = _expand_mat(R, L, bb)
    lmask = _seq_lane_mask(R, L, bb)
    lmask2 = jnp.concatenate([lmask, lmask], axis=0)
    fold = _fold_mat(bb)
    tile = jnp.transpose(fold)
    outs = []
    for h in range(A_HEADS):
        hs = slice(h * HEAD_DIM, (h + 1) * HEAD_DIM)
        st_h = st_ref[h]
        A, Bt, Kt, Rt, V = a_t[:, hs], b_t[:, hs], k_t[:, hs], r_t[:, hs], v[:, hs]
        X = jnp.concatenate([A, Rt], axis=0)
        Z = jnp.concatenate([Bt, Kt], axis=0)
        G = _mm(X, Z, 3, NT)
        m_ab = jnp.where(strict, G[0:R, 0:R], 0.0)
        m_ak = jnp.where(strict, G[0:R, R:2 * R], 0.0)
        m_rb = jnp.where(incl, G[R:2 * R, 0:R], 0.0)
        m_rk = jnp.where(incl, G[R:2 * R, R:2 * R], 0.0)
        P = jnp.where(lmask2, _mm(X, st_h, 3), 0.0)
        PS = _mm_xsel(P, fold)
        Y = PS[0:R] + _mm(m_ak, V, 3)
        U = _neumann_solve(m_ab, Y, L, 3)
        O = PS[R:2 * R] + _mm(jnp.concatenate([m_rb, m_rk], axis=1), jnp.concatenate([U, V], axis=0), 3)
        outs.append(O)
        UV = jnp.concatenate([U, V], axis=0)
        UVb = jnp.where(lmask2, _mm_xsel(UV, tile), 0.0)
        lhs = jnp.concatenate([bh_t[hs, :], kh_t[hs, :]], axis=1)
        gam = _mm_xsel(gam_t[hs, :], expand)
        st_ref[h] = gam * st_h + _mm(lhs, UVb, 3)
    o = jnp.concatenate(outs, axis=1)
    inv = 1.0 / HEAD_DIM
    mu = _mm_xsel(o, ones_h) * inv
    oc = o - mu
    var = _mm_xsel(oc * oc, ones_h) * inv
    y = oc * lax.rsqrt(var + 64e-5) * ln_g + ln_b
    y = y + _mm_xsel(r * k2 * r_k, ones_h) * v
    ya_ref[...] = (y * g).reshape(bb, L, W)

    @pl.when(c == pl.num_programs(1) - 1)
    def _():
        sto_ref[...] = st_ref[...]


def _rwkv(pa, shift8, st0, mu, pvec, lora_w, bb, L):
    B, T, _ = pa.shape
    lb8 = L // SUBLANES
    kern = functools.partial(_rwkv_kernel, chunk=L)
    return pl.pallas_call(
        kern,
        grid=(B // bb, T // L),
        in_specs=[
            pl.BlockSpec((bb, L, 4 * A_WIDTH), lambda i, j: (i, j, 0)),
            pl.BlockSpec((bb, SUBLANES, 4 * A_WIDTH), lambda i, j: (i, jnp.maximum(j * lb8 - 1, 0), 0)),
            pl.BlockSpec((bb, SUBLANES, 4 * A_WIDTH), lambda i, j: (i, 0, 0)),
            pl.BlockSpec((A_HEADS, HEAD_DIM, bb * HEAD_DIM), lambda i, j: (0, 0, i)),
            pl.BlockSpec((1, 4 * A_WIDTH), lambda i, j: (0, 0)),
            pl.BlockSpec((SUBLANES, A_WIDTH), lambda i, j: (0, 0)),
            pl.BlockSpec((A_WIDTH, 3 * A_WIDTH), lambda i, j: (0, 0)),
        ],
        out_specs=[
            pl.BlockSpec((bb, L, A_WIDTH), lambda i, j: (i, j, 0)),
            pl.BlockSpec((A_HEADS, HEAD_DIM, bb * HEAD_DIM), lambda i, j: (0, 0, i)),
        ],
        out_shape=[
            jax.ShapeDtypeStruct((B, T, A_WIDTH), F32),
            jax.ShapeDtypeStruct((A_HEADS, HEAD_DIM, B * HEAD_DIM), F32),
        ],
        scratch_shapes=[pltpu.VMEM((A_HEADS, HEAD_DIM, bb * HEAD_DIM), F32)],
        compiler_params=_cparams(("arbitrary", "arbitrary")),
        name="rwkv7",
    )(pa, pa, shift8, st0, mu, pvec, lora_w)


def _prep_rwkv(mu, w0, w2, a0, a2, g2, k_k, k_a, r_k, ln_g, ln_b):
    zero = jnp.zeros((A_WIDTH,), F32)
    pvec = jnp.stack([w0, a0, k_k, k_a, r_k.reshape(A_WIDTH), ln_g, ln_b, zero])
    lora = jnp.zeros((A_WIDTH, 3 * A_WIDTH), F32)
    lora = lora.at[0:64, 0:A_WIDTH].set(w2)
    lora = lora.at[64:128, A_WIDTH:2 * A_WIDTH].set(a2)
    lora = lora.at[128:256, 2 * A_WIDTH:].set(g2)
    return mu.reshape(1, 4 * A_WIDTH), pvec, lora


def _state_to_lanes(s):
    B, H = s.shape[:2]
    return jnp.transpose(s, (1, 3, 0, 2)).reshape(H, HEAD_DIM, B * HEAD_DIM)


def _state_from_lanes(st, B):
    H = st.shape[0]
    return jnp.transpose(st.reshape(H, HEAD_DIM, B, HEAD_DIM), (2, 0, 3, 1))


def _mlstm_kernel(pb_ref, halo_ref, cv_ref, pif_ref, gt_ref, ct0_ref, n0_ref, m0_ref, cw_ref, cb_ref, bif_ref,
                  brow_ref, ng_ref, yb_ref, cto_ref, no_ref, mo_ref, ct_ref, nt_ref, m_ref):
    c = pl.program_id(1)
    bb, L, _ = pb_ref.shape
    R = bb * L
    W = B_WIDTH
    neg = -jnp.inf

    @pl.when(c == 0)
    def _():
        ct_ref[...] = ct0_ref[...]
        nt_ref[...] = n0_ref[...]
        m_ref[...] = jnp.broadcast_to(m0_ref[...], m_ref.shape)

    pb = pb_ref[...]
    qk_in = pb[:, :, 0:2 * W]
    pre = jnp.where(c == 0, cv_ref[...], halo_ref[...])
    s1, s2, s3 = _shifted_rows(pre, qk_in, MLSTM_CONV - 1)
    cw = cw_ref[...]
    conv = cb_ref[...] + qk_in * cw[3:4, :] + s1 * cw[2:3, :] + s2 * cw[1:2, :] + s3 * cw[0:1, :]
    qk = _silu(conv).reshape(R, 2 * W)
    q = qk[:, 0:W]
    k = qk[:, W:2 * W] * (HEAD_DIM ** -0.5)
    v = pb[:, :, 2 * W:3 * W].reshape(R, W)
    og = pb[:, :, 3 * W:4 * W].reshape(R, W)

    same, incl, _ = _block_masks(R, L)
    same_f = same.astype(F32)
    incl_f = incl.astype(F32)
    gc = pif_ref[...].reshape(R, LANES) + bif_ref[...]
    lane = _iota((R, LANES), 1)
    lfc = jnp.where((lane >= B_HEADS) & (lane < 2 * B_HEADS), _log_sigmoid(gc), 0.0)
    bcum_c = _mm_sel(incl_f, lfc)
    btot_c = _mm_sel(same_f, lfc)
    gr = gt_ref[...] + brow_ref[...]
    row = _iota((SUBLANES, R), 0)
    lfr = jnp.where(row >= B_HEADS, _log_sigmoid(gr), 0.0)
    bcum_r = _mm_xsel(lfr, incl_f, NT)
    btot_r = _mm_xsel(lfr, same_f)
    m_col = jnp.broadcast_to(m_ref[:, 0:1, :], (bb, L, LANES)).reshape(R, LANES)

    lmask = _seq_lane_mask(R, L, bb)
    fold = _fold_mat(bb)
    tile = jnp.transpose(fold)
    expand = _expand_mat(R, L, bb)
    sh = int(math.log2(L))
    rl = _iota((R, LANES), 0)
    blockind = (jnp.right_shift(rl, sh) == lane).astype(F32)
    firstind = ((jnp.right_shift(rl, sh) == lane) & ((rl & (L - 1)) == 0)).astype(F32)

    houts, kws, wcs = [], [], []
    m_new_all = jnp.zeros((R, LANES), F32)
    for h in range(B_HEADS):
        hs = slice(h * HEAD_DIM, (h + 1) * HEAD_DIM)
        Q, K, V = q[:, hs], k[:, hs], v[:, hs]
        b_c = bcum_c[:, B_HEADS + h:B_HEADS + h + 1]
        b_r = bcum_r[B_HEADS + h:B_HEADS + h + 1, :]
        i_r = gr[h:h + 1, :]
        i_c = gc[:, h:h + 1]
        m_c = m_col[:, h:h + 1]
        D = jnp.where(incl, b_c - b_r + i_r, neg)
        inter = b_c + m_c
        mt = jnp.maximum(inter, jnp.max(D, axis=1, keepdims=True))
        S = _mm(Q, K, 1, NT) * jnp.exp(D - mt)
        iw = jnp.exp(inter - mt)
        QC = _mm_xsel(jnp.where(lmask, _mm(Q, ct_ref[h], 3), 0.0), fold)
        num = _mm(S, V, 1) + iw * QC
        qn = jnp.sum(_mm(Q, nt_ref[h], 3) * blockind, axis=1, keepdims=True)
        den = jnp.sum(S, axis=1, keepdims=True) + iw * qn
        houts.append(num / jnp.maximum(jnp.abs(den), jnp.exp(-mt)))
        bl_c = btot_c[:, B_HEADS + h:B_HEADS + h + 1]
        bl_r = btot_r[B_HEADS + h:B_HEADS + h + 1, :]
        gs_c = bl_c - b_c + i_c
        gmax = jnp.max(jnp.where(same, bl_r - b_r + i_r, neg), axis=1, keepdims=True)
        m_new = jnp.maximum(bl_c + m_c, gmax)
        kws.append(K * jnp.exp(gs_c - m_new))
        wcs.append(jnp.exp(bl_c + m_c - m_new))
        m_new_all = jnp.where(lane == h, m_new, m_new_all)

    kw_t = jnp.transpose(jnp.concatenate(kws, axis=1))
    for h in range(B_HEADS):
        hs = slice(h * HEAD_DIM, (h + 1) * HEAD_DIM)
        vb = jnp.where(lmask, _mm_xsel(v[:, hs], tile), 0.0)
        wc_row = jnp.sum(wcs[h] * expand, axis=0, keepdims=True)
        ct_ref[h] = wc_row * ct_ref[h] + _mm(kw_t[hs, :], vb, 3)
        wc_lane = jnp.sum(wcs[h] * firstind, axis=0, keepdims=True)
        nt_ref[h] = wc_lane * nt_ref[h] + _mm_xsel(kw_t[hs, :], blockind)
    m_ref[...] = m_new_all.reshape(bb, L, LANES)[:, 0:SUBLANES, :]

    hcat = jnp.concatenate(houts, axis=1)
    ones_h = _head_ones(W)
    inv = 1.0 / HEAD_DIM
    mu = _mm_xsel(hcat, ones_h) * inv
    hc = hcat - mu
    var = _mm_xsel(hc * hc, ones_h) * inv
    hn = hc * lax.rsqrt(var + 1e-5) * ng_ref[...]
    yb_ref[...] = (_sigmoid(og) * hn).reshape(bb, L, W)

    @pl.when(c == pl.num_programs(1) - 1)
    def _():
        cto_ref[...] = ct_ref[...]
        no_ref[...] = nt_ref[...]
        mo_ref[...] = m_ref[...]


def _mlstm(pb, conv8, pif, ct0, n0, m0, conv_w8, conv_b, bias_if, norm_g, bb, L):
    B, T, _ = pb.shape
    R = bb * L
    nbi, nch = B // bb, T // L
    lb8 = L // SUBLANES
    g_t = pif[:, :, 0:SUBLANES].reshape(nbi, bb, nch, L, SUBLANES).transpose(0, 2, 4, 1, 3).reshape(nbi, nch, SUBLANES, R)
    bias_row = jnp.broadcast_to(bias_if[0, 0:SUBLANES].reshape(SUBLANES, 1), (SUBLANES, R))
    n_in = jnp.pad(n0.reshape(nbi, bb, B_HEADS, HEAD_DIM).transpose(0, 2, 3, 1), ((0, 0), (0, 0), (0, 0), (0, LANES - bb)))
    m_in = jnp.pad(m0, ((0, 0), (0, LANES - B_HEADS))).reshape(B, 1, LANES)
    yb, ct, nt, mo = pl.pallas_call(
        _mlstm_kernel,
        grid=(nbi, nch),
        in_specs=[
            pl.BlockSpec((bb, L, 4 * B_WIDTH), lambda i, j: (i, j, 0)),
            pl.BlockSpec((bb, SUBLANES, 2 * B_WIDTH), lambda i, j: (i, jnp.maximum(j * lb8 - 1, 0), 0)),
            pl.BlockSpec((bb, SUBLANES, 2 * B_WIDTH), lambda i, j: (i, 0, 0)),
            pl.BlockSpec((bb, L, LANES), lambda i, j: (i, j, 0)),
            pl.BlockSpec((None, None, SUBLANES, R), lambda i, j: (i, j, 0, 0)),
            pl.BlockSpec((B_HEADS, HEAD_DIM, bb * HEAD_DIM), lambda i, j: (0, 0, i)),
            pl.BlockSpec((None, B_HEADS, HEAD_DIM, LANES), lambda i, j: (i, 0, 0, 0)),
            pl.BlockSpec((bb, 1, LANES), lambda i, j: (i, 0, 0)),
            pl.BlockSpec((SUBLANES, 2 * B_WIDTH), lambda i, j: (0, 0)),
            pl.BlockSpec((1, 2 * B_WIDTH), lambda i, j: (0, 0)),
            pl.BlockSpec((1, LANES), lambda i, j: (0, 0)),
            pl.BlockSpec((SUBLANES, R), lambda i, j: (0, 0)),
            pl.BlockSpec((1, B_WIDTH), lambda i, j: (0, 0)),
        ],
        out_specs=[
            pl.BlockSpec((bb, L, B_WIDTH), lambda i, j: (i, j, 0)),
            pl.BlockSpec((B_HEADS, HEAD_DIM, bb * HEAD_DIM), lambda i, j: (0, 0, i)),
            pl.BlockSpec((None, B_HEADS, HEAD_DIM, LANES), lambda i, j: (i, 0, 0, 0)),
            pl.BlockSpec((bb, SUBLANES, LANES), lambda i, j: (i, 0, 0)),
        ],
        out_shape=[
            jax.ShapeDtypeStruct((B, T, B_WIDTH), F32),
            jax.ShapeDtypeStruct((B_HEADS, HEAD_DIM, B * HEAD_DIM), F32),
            jax.ShapeDtypeStruct((nbi, B_HEADS, HEAD_DIM, LANES), F32),
            jax.ShapeDtypeStruct((B, SUBLANES, LANES), F32),
        ],
        scratch_shapes=[
            pltpu.VMEM((B_HEADS, HEAD_DIM, bb * HEAD_DIM), F32),
            pltpu.VMEM((B_HEADS, HEAD_DIM, LANES), F32),
            pltpu.VMEM((bb, SUBLANES, LANES), F32),
        ],
        compiler_params=_cparams(("arbitrary", "arbitrary")),
        name="mlstm",
    )(pb, pb, conv8, pif, g_t, ct0, n_in, m_in, conv_w8, conv_b, bias_if, bias_row, norm_g)
    n_new = nt[:, :, :, 0:bb].transpose(0, 3, 1, 2).reshape(B, B_HEADS, HEAD_DIM)
    return yb, ct, n_new, mo[:, 0, 0:B_HEADS]


def _masked_softmax(s, mask):
    mx = jnp.max(jnp.where(mask, s, NEG_INF), axis=1, keepdims=True)
    e = jnp.where(mask, jnp.exp(s - mx), 0.0)
    return e / jnp.maximum(jnp.sum(e, axis=1, keepdims=True), 1e-30)


def _select_blocks(psum, m_mat, t_col, n_s):
    tq = psum.shape[0]
    imp = _mm_xsel(psum, m_mat)
    blk = _iota((tq, LANES), 1)
    cur = jnp.right_shift(t_col, 6)
    valid = (blk * SEL_BLOCK <= t_col)
    forced = (blk == 0) | (blk == cur) | (blk == cur - 1)
    score = jnp.where(valid, imp + jnp.where(forced, FORCE_BONUS, 0.0), NEG_INF)
    score = jnp.where(blk < n_s, score, -jnp.inf)
    rank = jnp.zeros((tq, LANES), F32)
    for s in range(n_s):
        col = score[:, s:s + 1]
        ahead = (col > score) | ((col == score) & (blk > s))
        rank = rank + jnp.where(ahead, 1.0, 0.0)
    return jnp.where(rank < min(SEL_TOP, n_s), 1.0, 0.0)


def _stack_heads(pq, g):
    base = g * C_GROUP * HEAD_DIM
    parts = [pq[:, base + r * HEAD_DIM: base + (r + 1) * HEAD_DIM] for r in range(C_GROUP)]
    return jnp.concatenate(parts, axis=0) * (HEAD_DIM ** -0.5)


def _to_group_lanes(q, g):
    z = jnp.zeros_like(q)
    return jnp.concatenate([q, z] if g == 0 else [z, q], axis=1)


def _compress_kernel(x_ref, pw0_ref, pw1_ref, cw_ref, kc_ref, a_ref, *, rows_per_step):
    T = x_ref.shape[0]
    rs = rows_per_step
    ng = rs // CMP_STRIDE
    pool = (jnp.right_shift(_iota((ng, rs), 1), 4) == _iota((ng, rs), 0)).astype(F32)
    for c in range(T // rs):
        x = x_ref[c * rs:(c + 1) * rs, :]
        xw = jnp.concatenate([x * pw0_ref[...], x * pw1_ref[...]], axis=1)
        a_ref[c * ng:(c + 1) * ng, :] = _mm_sel(pool, xw)
    ngrp = T // CMP_STRIDE
    W = x_ref.shape[1]
    pooled = a_ref[:, 0:W] + pltpu.roll(a_ref[:, W:2 * W], ngrp - 1, axis=0)
    kc_ref[...] = _mm(pooled, cw_ref[...], 3)


def _compress(pcmp, pwt0, pwt1, cmpw_bd):
    B, T, W = pcmp.shape
    rs = pwt0.shape[0]
    ngrp = T // CMP_STRIDE
    kern = functools.partial(_compress_kernel, rows_per_step=rs)
    return pl.pallas_call(
        kern,
        grid=(B,),
        in_specs=[
            pl.BlockSpec((None, T, W), lambda b: (b, 0, 0)),
            pl.BlockSpec((rs, W), lambda b: (0, 0)),
            pl.BlockSpec((rs, W), lambda b: (0, 0)),
            pl.BlockSpec((W, W), lambda b: (0, 0)),
        ],
        out_specs=pl.BlockSpec((None, ngrp, W), lambda b: (b, 0, 0)),
        out_shape=jax.ShapeDtypeStruct((B, ngrp, W), F32),
        scratch_shapes=[pltpu.VMEM((ngrp, 2 * W), F32)],
        compiler_params=_cparams(("arbitrary",)),
        name="nsa_compress",
    )(pcmp, pwt0, pwt1, cmpw_bd)


def _combine_branches(gates, g, o_c, o_s, o_w, tq):
    outs = []
    for r in range(C_GROUP):
        rs = slice(r * tq, (r + 1) * tq)
        j = (g * C_GROUP + r) * 3
        outs.append(gates[:, j:j + 1] * o_c[rs] + gates[:, j + 1:j + 2] * o_s[rs] + gates[:, j + 2:j + 3] * o_w[rs])
    return outs


def _nsa_prompt_kernel(pq_ref, pg_ref, gb_ref, kc_ref, kv_ref, m_ref, e_ref, o_ref, *, n_s):
    i = pl.program_id(1)
    tq = pq_ref.shape[0]
    ngrp = kc_ref.shape[0]
    kt = 4 * tq
    t0 = i * tq
    rows = C_GROUP * tq
    t_row = t0 + (_iota((rows, 1), 0) & (tq - 1))
    t_col = t0 + _iota((tq, 1), 0)
    pq = pq_ref[...]
    gates = _sigmoid(pg_ref[...] + gb_ref[...])
    kc = kc_ref[...]
    n_end = _iota((rows, ngrp), 1) * CMP_STRIDE + (CMP_BLOCK - 1)
    cmask = n_end <= t_row
    nsteps = (t0 + tq + kt - 1) // kt
    pieces = []
    for g in range(C_KV_HEADS):
        gs = slice(g * HEAD_DIM, (g + 1) * HEAD_DIM)
        q = _stack_heads(pq, g)
        q2 = _to_group_lanes(q, g).astype(BF16)
        p_c = _masked_softmax(_mm(q, kc[:, g * HEAD_DIM:(g + 1) * HEAD_DIM], 3, NT), cmask)
        o_c = _mm(p_c, kc[:, 2 * HEAD_DIM + g * HEAD_DIM: 2 * HEAD_DIM + (g + 1) * HEAD_DIM], 1)
        psum = p_c[0:tq]
        for r in range(1, C_GROUP):
            psum = psum + p_c[r * tq:(r + 1) * tq]
        sel = _select_blocks(psum, m_ref[...], t_col, n_s)
        sel_r = jnp.concatenate([sel] * C_GROUP, axis=0).astype(BF16)

        def body(j, carry):
            m, l, acc = carry
            off = pl.multiple_of(j * kt, kt)
            kk = kv_ref[pl.ds(off, kt), 0:LANES]
            vv = kv_ref[pl.ds(off, kt), LANES:2 * LANES]
            picked = _dot(sel_r, e_ref[:, pl.ds(off, kt)]) > 0.5
            kpos = off + _iota((rows, kt), 1)
            mask = picked & (kpos <= t_row)
            s = _dot(q2, kk, NT)
            m_new = jnp.maximum(m, jnp.max(jnp.where(mask, s, NEG_INF), axis=1, keepdims=True))
            p = jnp.where(mask, jnp.exp(s - m_new), 0.0)
            alpha = jnp.exp(m - m_new)
            l = alpha * l + jnp.sum(p, axis=1, keepdims=True)
            acc = alpha * acc + _dot(p.astype(BF16), vv)
            return m_new, l, acc

        init = (jnp.full((rows, 1), NEG_INF, F32), jnp.zeros((rows, 1), F32), jnp.zeros((rows, LANES), F32))
        _, l, acc = lax.fori_loop(0, nsteps, body, init)
        o_s = acc[:, gs] / jnp.maximum(l, 1e-30)

        n_tiles = WINDOW // tq + 1
        kws, vws, poss = [], [], []
        for cidx in range(n_tiles):
            tile = i - (n_tiles - 1) + cidx
            off = pl.multiple_of(jnp.maximum(tile, 0) * tq, tq)
            kws.append(kv_ref[pl.ds(off, tq), 2 * LANES:3 * LANES])
            vws.append(kv_ref[pl.ds(off, tq), 3 * LANES:4 * LANES])
            poss.append(jnp.where(tile >= 0, tile * tq, -2 * WINDOW * tq) + _iota((rows, tq), 1))
        kw = jnp.concatenate(kws, axis=0)
        vw = jnp.concatenate(vws, axis=0)
        kpos = jnp.concatenate(poss, axis=1)
        wmask = (kpos >= 0) & (kpos <= t_row) & (kpos > t_row - WINDOW)
        p_w = _masked_softmax(_dot(q2, kw, NT), wmask)
        o_w = _dot(p_w.astype(BF16), vw)[:, gs]
        pieces += _combine_branches(gates, g, o_c, o_s, o_w, tq)
    o_ref[...] = jnp.concatenate(pieces, axis=1)


def _nsa_prompt(pq, pg, gate_b, kc, kvb, m_mat, e_mat, tq):
    B, T, _ = pq.shape
    ngrp = kc.shape[1]
    n_s = T // SEL_BLOCK
    kern = functools.partial(_nsa_prompt_kernel, n_s=n_s)
    return pl.pallas_call(
        kern,
        grid=(B, T // tq),
        in_specs=[
            pl.BlockSpec((None, tq, C_WIDTH), lambda b, i: (b, i, 0)),
            pl.BlockSpec((None, tq, LANES), lambda b, i: (b, i, 0)),
            pl.BlockSpec((1, LANES), lambda b, i: (0, 0)),
            pl.BlockSpec((None, ngrp, 4 * HEAD_DIM), lambda b, i: (b, 0, 0)),
            pl.BlockSpec((None, T, 4 * LANES), lambda b, i: (b, 0, 0)),
            pl.BlockSpec((ngrp, LANES), lambda b, i: (0, 0)),
            pl.BlockSpec((LANES, T), lambda b, i: (0, 0)),
        ],
        out_specs=pl.BlockSpec((None, tq, C_WIDTH), lambda b, i: (b, i, 0)),
        out_shape=jax.ShapeDtypeStruct((B, T, C_WIDTH), F32),
        compiler_params=_cparams(("arbitrary", "arbitrary")),
        name="nsa_prompt",
    )(pq, pg, gate_b, kc, kvb, m_mat, e_mat)


def _nsa_sample_kernel(pt_ref, *refs, n_pages, past_len, n_s, n_c):
    pages = refs[:n_pages]
    (pq_ref, pg_ref, gb_ref, pcmp_ref, pslc_ref, pwin_ref, win_ref, pw0_ref, pw1_ref, cw_ref, m_ref, e_ref,
     o_ref, a_ref, k_ref, v_ref) = refs[n_pages:]
    tq = pq_ref.shape[0]
    rows = C_GROUP * tq
    W = 4 * HEAD_DIM
    ng = PAGE_SIZE // CMP_STRIDE
    pool = (jnp.right_shift(_iota((ng, PAGE_SIZE), 1), 4) == _iota((ng, PAGE_SIZE), 0)).astype(F32)
    pw0 = pw0_ref[...]
    pw1 = pw1_ref[...]
    for p in range(n_pages):
        page = pages[p][...]
        xc = page[:, 0:W]
        a_ref[p * ng:(p + 1) * ng, :] = _mm_sel(pool, jnp.concatenate([xc * pw0, xc * pw1], axis=1))
        k_ref[p * PAGE_SIZE:(p + 1) * PAGE_SIZE, :] = page[:, W:W + LANES].astype(BF16)
        v_ref[p * PAGE_SIZE:(p + 1) * PAGE_SIZE, :] = page[:, W + LANES:W + 2 * LANES].astype(BF16)
    xn = pcmp_ref[...]
    an = jnp.sum(jnp.concatenate([xn * pw0[0:tq], xn * pw1[0:tq]], axis=1), axis=0, keepdims=True)
    a_ref[n_pages * ng:(n_pages + 1) * ng, :] = jnp.where(_iota((ng, 2 * W), 0) == 0, an, 0.0)
    zpad = jnp.zeros((PAGE_SIZE - tq, LANES), F32)
    pslc = pslc_ref[...]
    k_ref[past_len:past_len + PAGE_SIZE, :] = jnp.concatenate([pslc[:, 0:LANES], zpad], axis=0).astype(BF16)
    v_ref[past_len:past_len + PAGE_SIZE, :] = jnp.concatenate([pslc[:, LANES:2 * LANES], zpad], axis=0).astype(BF16)
    ngrp = a_ref.shape[0]
    pooled = a_ref[:, 0:W] + pltpu.roll(a_ref[:, W:2 * W], ngrp - 1, axis=0)
    kc = _mm(pooled, cw_ref[...], 3)

    nk = k_ref.shape[0]
    t_row = past_len + (_iota((rows, 1), 0) & (tq - 1))
    t_col = past_len + _iota((tq, 1), 0)
    pq = pq_ref[...]
    gates = _sigmoid(pg_ref[...] + gb_ref[...])
    n_idx = _iota((rows, ngrp), 1)
    cmask = (n_idx * CMP_STRIDE + (CMP_BLOCK - 1) <= t_row) & (n_idx < n_c)
    kpos = _iota((rows, nk), 1)
    win = win_ref[...]
    pwin = pwin_ref[...]
    nwb = win.shape[0]
    zw = jnp.zeros((SUBLANES, LANES), F32)
    kw = jnp.concatenate([win[:, 0:LANES], pwin[:, 0:LANES], zw], axis=0).astype(BF16)
    vw = jnp.concatenate([win[:, LANES:2 * LANES], pwin[:, LANES:2 * LANES], zw], axis=0).astype(BF16)
    jj = _iota((rows, nwb + tq + SUBLANES), 1)
    tl = _iota((rows, 1), 0) & (tq - 1)
    wmask = (jj > tl + (nwb - WINDOW)) & (jj <= tl + nwb) & (jj < nwb + tq)
    pieces = []
    for g in range(C_KV_HEADS):
        gs = slice(g * HEAD_DIM, (g + 1) * HEAD_DIM)
        q = _stack_heads(pq, g)
        q2 = _to_group_lanes(q, g).astype(BF16)
        p_c = _masked_softmax(_mm(q, kc[:, g * HEAD_DIM:(g + 1) * HEAD_DIM], 3, NT), cmask)
        o_c = _mm(p_c, kc[:, 2 * HEAD_DIM + g * HEAD_DIM: 2 * HEAD_DIM + (g + 1) * HEAD_DIM], 1)
        psum = p_c[0:tq]
        for r in range(1, C_GROUP):
            psum = psum + p_c[r * tq:(r + 1) * tq]
        sel = _select_blocks(psum, m_ref[...], t_col, n_s)
        sel_r = jnp.concatenate([sel] * C_GROUP, axis=0).astype(BF16)
        smask = (_dot(sel_r, e_ref[...]) > 0.5) & (kpos <= t_row)
        p_s = _masked_softmax(_dot(q2, k_ref[...], NT), smask)
        o_s = _dot(p_s.astype(BF16), v_ref[...])[:, gs]
        p_w = _masked_softmax(_dot(q2, kw, NT), wmask)
        o_w = _dot(p_w.astype(BF16), vw)[:, gs]
        pieces += _combine_branches(gates, g, o_c, o_s, o_w, tq)
    o_ref[...] = jnp.concatenate(pieces, axis=1)


def _nsa_sample(layer, page_table, cache, pq, pg, gate_b, pcmp, pslc, pwin, win_buf, pwt0, pwt1, cmpw_bd, m_mat, e_mat):
    B, T, _ = pq.shape
    n_pages = page_table.shape[1]
    past_len = n_pages * PAGE_SIZE
    lp = -(-(past_len + T) // SEL_BLOCK) * SEL_BLOCK
    n_s = lp // SEL_BLOCK
    n_c = lp // CMP_STRIDE - CMP_BLOCK // CMP_STRIDE + 1
    ngrp = (n_pages + 1) * (PAGE_SIZE // CMP_STRIDE)
    nk = past_len + PAGE_SIZE
    nwb = win_buf.shape[1]
    kern = functools.partial(_nsa_sample_kernel, n_pages=n_pages, past_len=past_len, n_s=n_s, n_c=n_c)
    page_specs = [
        pl.BlockSpec((None, None, PAGE_SIZE, 4 * LANES), functools.partial(lambda b, pt, p: (layer, pt[b, p], 0, 0), p=p))
        for p in range(n_pages)
    ]
    row = lambda w: pl.BlockSpec((None, T, w), lambda b, pt: (b, 0, 0))
    full = lambda a: pl.BlockSpec(a.shape, lambda b, pt: (0,) * a.ndim)
    grid_spec = pltpu.PrefetchScalarGridSpec(
        num_scalar_prefetch=1,
        grid=(B,),
        in_specs=page_specs + [
            row(C_WIDTH), row(LANES), full(gate_b), row(4 * HEAD_DIM), row(4 * HEAD_DIM), row(4 * HEAD_DIM),
            pl.BlockSpec((None, nwb, 4 * HEAD_DIM), lambda b, pt: (b, 0, 0)),
            full(pwt0), full(pwt1), full(cmpw_bd), full(m_mat), full(e_mat),
        ],
        out_specs=pl.BlockSpec((None, T, C_WIDTH), lambda b, pt: (b, 0, 0)),
        scratch_shapes=[
            pltpu.VMEM((ngrp, 8 * HEAD_DIM), F32),
            pltpu.VMEM((nk, LANES), BF16),
            pltpu.VMEM((nk, LANES), BF16),
        ],
    )
    return pl.pallas_call(
        kern,
        grid_spec=grid_spec,
        out_shape=jax.ShapeDtypeStruct((B, T, C_WIDTH), F32),
        compiler_params=_cparams(("arbitrary",)),
        name="nsa_sample",
    )(page_table, *([cache] * n_pages), pq, pg, gate_b, pcmp, pslc, pwin, win_buf, pwt0, pwt1, cmpw_bd, m_mat, e_mat)


def _nsa_consts(lp, n_keys):
    n_str = lp // CMP_STRIDE
    n_c = n_str - CMP_BLOCK // CMP_STRIDE + 1
    n_s = lp // SEL_BLOCK
    c0 = np.arange(n_str)[:, None] * CMP_STRIDE
    s0 = np.arange(LANES)[None, :] * SEL_BLOCK
    m = (c0 < s0 + SEL_BLOCK) & (c0 + CMP_BLOCK > s0) & (np.arange(n_str)[:, None] < n_c) & (np.arange(LANES)[None, :] < n_s)
    e = (np.arange(n_keys)[None, :] // SEL_BLOCK) == np.arange(LANES)[:, None]
    return m.astype(np.float32), e.astype(np.float32)


def _prep_mlstm(conv_w, conv_b, i_b, f_b, norm_g):
    cw8 = jnp.pad(conv_w, ((0, SUBLANES - MLSTM_CONV), (0, 0)))
    bias_if = jnp.pad(jnp.concatenate([i_b, f_b]), (0, LANES - 2 * B_HEADS)).reshape(1, LANES)
    return cw8, conv_b.reshape(1, 2 * B_WIDTH), bias_if, norm_g.reshape(1, B_WIDTH)


_A0, _B0, _Q0, _CMP0, _SLC0, _WIN0, _G0, _IF0, _PEND = 0, 1024, 2048, 2560, 2816, 3072, 3328, 3456, 3584
_PROJ_SPLITS = ((_A0, _B0), (_B0, _Q0), (_Q0, _CMP0), (_CMP0, _SLC0), (_SLC0, _WIN0), (_WIN0, _G0), (_G0, _IF0),
                (_IF0, _PEND))


def _pad_lanes(a, width):
    return jnp.pad(a, [(0, 0)] * (a.ndim - 1) + [(0, width - a.shape[-1])])


def _prep_layer(P, l):
    w_in = P["w_in"][l]
    wa, wb, wc = w_in[:, 0:1024], w_in[:, 1024:2056], w_in[:, 2056:3360]
    w_in_p = jnp.concatenate([
        wa, wb[:, 0:768], wb[:, 776:1032], wc[:, 0:512], wc[:, 512:1280],
        _pad_lanes(wc[:, 1280:1304], LANES), _pad_lanes(wb[:, 768:776], LANES)], axis=1).astype(BF16)
    up = P["ffn_up"][l]
    ffn_up_p = jnp.concatenate([_pad_lanes(up[:, 0:D_FF], D_FF_PAD), _pad_lanes(up[:, D_FF:], D_FF_PAD)], axis=1).astype(BF16)
    cw = P["nsa_cmp_w"][l]
    cmpw_bd = jnp.zeros((4 * HEAD_DIM, 4 * HEAD_DIM), F32)
    for kv in range(2):
        for g in range(C_KV_HEADS):
            o = (kv * C_KV_HEADS + g) * HEAD_DIM
            cmpw_bd = cmpw_bd.at[o:o + HEAD_DIM, o:o + HEAD_DIM].set(cw[kv, g])
    pool = P["nsa_pool_w"][l].reshape(CMP_BLOCK, 4 * HEAD_DIM)
    return dict(
        w_in=w_in_p,
        w_out=P["w_out"][l].astype(BF16),
        ffn_up=ffn_up_p,
        ffn_down=jnp.pad(P["ffn_down"][l], ((0, D_FF_PAD - D_FF), (0, 0))).astype(BF16),
        ffn_cw=jnp.pad(P["ffn_conv_w"][l], ((0, SUBLANES - FFN_CONV), (0, D_FF_PAD - D_FF))),
        ffn_cb=_pad_lanes(P["ffn_conv_b"][l].reshape(1, D_FF), D_FF_PAD),
        ln1=(P["ln_g"][l, 0].reshape(1, D_MODEL), P["ln_b"][l, 0].reshape(1, D_MODEL)),
        ln2=(P["ln_g"][l, 1].reshape(1, D_MODEL), P["ln_b"][l, 1].reshape(1, D_MODEL)),
        rwkv=_prep_rwkv(P["rwkv_mu"][l], P["rwkv_w0"][l], P["rwkv_w2"][l], P["rwkv_a0"][l], P["rwkv_a2"][l],
                        P["rwkv_g2"][l], P["rwkv_k_k"][l], P["rwkv_k_a"][l], P["rwkv_r_k"][l], P["rwkv_ln_g"][l],
                        P["rwkv_ln_b"][l]),
        mlstm=_prep_mlstm(P["mlstm_conv_w"][l], P["mlstm_conv_b"][l], P["mlstm_i_b"][l], P["mlstm_f_b"][l],
                          P["mlstm_norm_g"][l]),
        pool0=pool[0:CMP_STRIDE], pool1=pool[CMP_STRIDE:CMP_BLOCK], cmpw=cmpw_bd,
        gate_b=_pad_lanes(P["nsa_gate_b"][l].reshape(1, 3 * C_HEADS), LANES),
    )


def _rows8(state):
    return jnp.pad(state, ((0, 0), (SUBLANES - state.shape[1], 0), (0, 0)))


def _last_rows(prev, cur, k):
    if cur.shape[1] >= k:
        return cur[:, cur.shape[1] - k:]
    return jnp.concatenate([prev, cur], axis=1)[:, -k:]


def _trunk(x, mod, st, layers, nsa_fn, dense_tile, rec_tile):
    B, T, _ = x.shape
    dbb, dL = dense_tile
    rbb, rL = rec_tile
    new = {k: [] for k in ("nsa_kv", "win_kv", "rwkv", "rwkv_shift", "mlstm_C", "mlstm_n", "mlstm_m", "mlstm_conv", "ffn_conv")}
    for l, Lw in enumerate(layers):
        m = mod[l]
        pa, pb, pq, pcmp, pslc, pwin, pg, pif, kvb = _modmm(
            x, m, 0, 1, Lw["w_in"], _PROJ_SPLITS + ((_SLC0, _G0),), (False,) * 8 + (True,), dbb, dL, "in_proj")
        ya, rw_st = _rwkv(pa, _rows8(st["rwkv_shift"][l][:, None, :]), _state_to_lanes(st["rwkv"][l]), *Lw["rwkv"], rbb, rL)
        yb, c_st, n_st, m_st = _mlstm(pb, _rows8(st["mlstm_conv"][l]), pif, _state_to_lanes(st["mlstm_C"][l]),
                                      st["mlstm_n"][l], st["mlstm_m"][l], *Lw["mlstm"], rbb, rL)
        yc, win_new = nsa_fn(l, Lw, pq, pg, pcmp, pslc, pwin, kvb)
        x = _outproj(ya, yb, yc, x, m, Lw["w_out"], *Lw["ln1"], dbb, dL)
        (u,) = _modmm(x, m, 3, 4, Lw["ffn_up"], ((0, 2 * D_FF_PAD),), (False,), dbb, dL, "ffn_up")
        st8 = _rows8(_pad_lanes(st["ffn_conv"][l], D_FF_PAD))
        x = _ffn_down(u, st8, x, m, Lw["ffn_cw"], Lw["ffn_cb"], Lw["ffn_down"], *Lw["ln2"], dbb, dL)
        new["nsa_kv"].append(jnp.concatenate([pcmp, pslc], axis=-1).reshape(B, T, 4, C_KV_HEADS, HEAD_DIM))
        new["win_kv"].append(win_new)
        new["rwkv"].append(_state_from_lanes(rw_st, B))
        new["rwkv_shift"].append(pa[:, -1])
        new["mlstm_C"].append(_state_from_lanes(c_st, B))
        new["mlstm_n"].append(n_st)
        new["mlstm_m"].append(m_st)
        new["mlstm_conv"].append(_last_rows(st["mlstm_conv"][l], pb[:, :, 0:2 * B_WIDTH], MLSTM_CONV - 1))
        new["ffn_conv"].append(_last_rows(st["ffn_conv"][l], u[:, :, 0:D_FF], FFN_CONV - 1))
    return x, {k: jnp.stack(v) for k, v in new.items()}


def kernel(x_prompt, x_sample, c_prompt, c_sample, cache_nsa_kv, cache_win_kv, state_rwkv, state_rwkv_shift,
           state_mlstm_C, state_mlstm_n, state_mlstm_m, state_mlstm_conv, state_ffn_conv, page_table,
           w_in, w_out, ada_w, ada_b, ln_g, ln_b, rwkv_mu, rwkv_w0, rwkv_w2, rwkv_a0, rwkv_a2, rwkv_g2,
           rwkv_k_k, rwkv_k_a, rwkv_r_k, rwkv_ln_g, rwkv_ln_b, mlstm_conv_w, mlstm_conv_b, mlstm_i_b,
           mlstm_f_b, mlstm_norm_g, nsa_pool_w, nsa_cmp_w, nsa_gate_b, ffn_up, ffn_conv_w, ffn_conv_b, ffn_down):
    P = dict(w_in=w_in, w_out=w_out, ln_g=ln_g, ln_b=ln_b, rwkv_mu=rwkv_mu, rwkv_w0=rwkv_w0, rwkv_w2=rwkv_w2,
             rwkv_a0=rwkv_a0, rwkv_a2=rwkv_a2, rwkv_g2=rwkv_g2, rwkv_k_k=rwkv_k_k, rwkv_k_a=rwkv_k_a,
             rwkv_r_k=rwkv_r_k, rwkv_ln_g=rwkv_ln_g, rwkv_ln_b=rwkv_ln_b, mlstm_conv_w=mlstm_conv_w,
             mlstm_conv_b=mlstm_conv_b, mlstm_i_b=mlstm_i_b, mlstm_f_b=mlstm_f_b, mlstm_norm_g=mlstm_norm_g,
             nsa_pool_w=nsa_pool_w, nsa_cmp_w=nsa_cmp_w, nsa_gate_b=nsa_gate_b, ffn_up=ffn_up,
             ffn_conv_w=ffn_conv_w, ffn_conv_b=ffn_conv_b, ffn_down=ffn_down)
    Bp, Tp, _ = x_prompt.shape
    Bs, Ts, _ = x_sample.shape
    G, dh = C_KV_HEADS, HEAD_DIM
    layers = [_prep_layer(P, l) for l in range(DEPTH)]

    nb = -(-(Bp + Bs) // SUBLANES) * SUBLANES
    c_all = jnp.pad(jnp.concatenate([c_prompt, c_sample], axis=0), ((0, nb - Bp - Bs), (0, 0)))
    mod = _ada(c_all, ada_w, ada_b)
    mod_p = mod[:, 0:Bp].reshape(DEPTH, Bp, 1, 6 * D_MODEL)
    mod_s = mod[:, Bp:Bp + Bs].reshape(DEPTH, Bs, 1, 6 * D_MODEL)

    st_p = dict(
        rwkv=jnp.zeros((DEPTH, Bp, A_HEADS, dh, dh), F32), rwkv_shift=jnp.zeros((DEPTH, Bp, 4 * A_WIDTH), F32),
        mlstm_C=jnp.zeros((DEPTH, Bp, B_HEADS, dh, dh), F32), mlstm_n=jnp.zeros((DEPTH, Bp, B_HEADS, dh), F32),
        mlstm_m=jnp.zeros((DEPTH, Bp, B_HEADS), F32), mlstm_conv=jnp.zeros((DEPTH, Bp, MLSTM_CONV - 1, 2 * B_WIDTH), F32),
        ffn_conv=jnp.zeros((DEPTH, Bp, FFN_CONV - 1, D_FF), F32))
    m_p, e_p = _nsa_consts(Tp, Tp)
    e_p = jnp.asarray(e_p, BF16)
    tq = 128
    rs = 4 * tq

    def nsa_prompt(l, Lw, pq, pg, pcmp, pslc, pwin, kvb):
        kc = _compress(pcmp, jnp.tile(Lw["pool0"], (rs // CMP_STRIDE, 1)), jnp.tile(Lw["pool1"], (rs // CMP_STRIDE, 1)), Lw["cmpw"])
        yc = _nsa_prompt(pq, pg, Lw["gate_b"], kc, kvb, jnp.asarray(m_p), e_p, tq)
        return yc, pwin[:, -min(WINDOW, Tp):].reshape(Bp, min(WINDOW, Tp), 2, G, dh)

    y_prompt, new_p = _trunk(x_prompt, mod_p, st_p, layers, nsa_prompt, (1, 256), (Bp, ROWS // Bp))

    st_s = dict(rwkv=state_rwkv, rwkv_shift=state_rwkv_shift, mlstm_C=state_mlstm_C, mlstm_n=state_mlstm_n,
                mlstm_m=state_mlstm_m, mlstm_conv=state_mlstm_conv, ffn_conv=state_ffn_conv)
    n_pages = page_table.shape[1]
    past_len = n_pages * PAGE_SIZE
    lp = -(-(past_len + Ts) // SEL_BLOCK) * SEL_BLOCK
    ngrp = (n_pages + 1) * (PAGE_SIZE // CMP_STRIDE)
    m_s, e_s = _nsa_consts(lp, past_len + PAGE_SIZE)
    m_s = jnp.asarray(np.pad(m_s, ((0, ngrp - m_s.shape[0]), (0, 0))))
    e_s = jnp.asarray(e_s, BF16)
    cache = cache_nsa_kv.reshape(DEPTH, cache_nsa_kv.shape[1], PAGE_SIZE, 4 * G * dh)
    nwb = cache_win_kv.shape[2]

    def nsa_sample(l, Lw, pq, pg, pcmp, pslc, pwin, kvb):
        win_buf = cache_win_kv[l].reshape(Bs, nwb, 2 * G * dh)
        yc = _nsa_sample(l, page_table, cache, pq, pg, Lw["gate_b"], pcmp, pslc, pwin, win_buf,
                         jnp.tile(Lw["pool0"], (PAGE_SIZE // CMP_STRIDE, 1)), jnp.tile(Lw["pool1"], (PAGE_SIZE // CMP_STRIDE, 1)),
                         Lw["cmpw"], m_s, e_s)
        win_all = jnp.concatenate([win_buf, pwin], axis=1)
        keep = min(WINDOW, nwb + Ts)
        return yc, win_all[:, -keep:].reshape(Bs, keep, 2, G, dh)

    y_sample, new_s = _trunk(x_sample, mod_s, st_s, layers, nsa_sample, (ROWS // Ts, Ts), (ROWS // Ts, Ts))

    return (y_prompt, y_sample,
            new_p["nsa_kv"], new_s["nsa_kv"], new_p["win_kv"], new_s["win_kv"],
            new_p["rwkv"], new_s["rwkv"], new_p["rwkv_shift"], new_s["rwkv_shift"],
            new_p["mlstm_C"], new_s["mlstm_C"], new_p["mlstm_n"], new_s["mlstm_n"],
            new_p["mlstm_m"], new_s["mlstm_m"], new_p["mlstm_conv"], new_s["mlstm_conv"],
            new_p["ffn_conv"], new_s["ffn_conv"])
```

```python
import functools
import math

import numpy as np
import jax
import jax.numpy as jnp
from jax import lax
from jax.experimental import pallas as pl
from jax.experimental.pallas import tpu as pltpu

F32 = jnp.float32
BF16 = jnp.bfloat16

D_MODEL = 1024
DEPTH = 4
HEAD_DIM = 64
A_WIDTH = 256
B_WIDTH = 256
C_WIDTH = 512
A_HEADS = 4
B_HEADS = 4
C_HEADS = 8
C_KV_HEADS = 2
C_GROUP = 4
PAGE_SIZE = 128
MLSTM_CONV = 4
CMP_BLOCK = 32
CMP_STRIDE = 16
SEL_BLOCK = 64
SEL_TOP = 16
WINDOW = 512
D_FF = 2752
D_FF_PAD = 2816
FFN_CONV = 3
ALPHA = (2 * DEPTH) ** 0.25
FORCE_BONUS = 1e4
NEG_INF = -1e30
LANES = 128
SUBLANES = 8
ROWS = 256
VMEM_LIMIT = 56 * 1024 * 1024

NN = (((1,), (0,)), ((), ()))
NT = (((1,), (1,)), ((), ()))


def _dot(a, b, dn=NN):
    return lax.dot_general(a, b, dn, preferred_element_type=F32)


def _split2(a):
    hi = a.astype(BF16)
    lo = (a - hi.astype(F32)).astype(BF16)
    return hi, lo


def _mm(a, b, passes=1, dn=NN):
    if passes == 1:
        return _dot(a.astype(BF16), b.astype(BF16), dn)
    ah, al = _split2(a)
    bh, bl = _split2(b)
    return _dot(ah, bh, dn) + (_dot(al, bh, dn) + _dot(ah, bl, dn))


def _mm_sel(sel, x, dn=NN):
    s = sel.astype(BF16)
    x1 = x.astype(BF16)
    r1 = x - x1.astype(F32)
    x2 = r1.astype(BF16)
    x3 = (r1 - x2.astype(F32)).astype(BF16)
    return _dot(s, x1, dn) + (_dot(s, x2, dn) + _dot(s, x3, dn))


def _mm_xsel(x, sel, dn=NN):
    s = sel.astype(BF16)
    x1 = x.astype(BF16)
    r1 = x - x1.astype(F32)
    x2 = r1.astype(BF16)
    x3 = (r1 - x2.astype(F32)).astype(BF16)
    return _dot(x1, s, dn) + (_dot(x2, s, dn) + _dot(x3, s, dn))


def _sigmoid(x):
    return 1.0 / (1.0 + jnp.exp(-x))


def _silu(x):
    return x * _sigmoid(x)


def _softplus(x):
    return jnp.maximum(x, 0.0) + jnp.log(1.0 + jnp.exp(-jnp.abs(x)))


def _log_sigmoid(x):
    return -_softplus(-x)


def _iota(shape, axis):
    return lax.broadcasted_iota(jnp.int32, shape, axis)


def _block_masks(rows, chunk):
    sh = int(math.log2(chunk))
    r = _iota((rows, rows), 0)
    s = _iota((rows, rows), 1)
    same = jnp.right_shift(r, sh) == jnp.right_shift(s, sh)
    return same, same & (s <= r), same & (s < r)


def _head_ones(width):
    r = _iota((width, width), 0)
    s = _iota((width, width), 1)
    return (jnp.right_shift(r, 6) == jnp.right_shift(s, 6)).astype(F32)


def _expand_mat(rows, chunk, nseq):
    sh = int(math.log2(chunk))
    r = _iota((rows, nseq * HEAD_DIM), 0)
    c = _iota((rows, nseq * HEAD_DIM), 1)
    return ((jnp.right_shift(c, 6) == jnp.right_shift(r, sh)) & ((r & (chunk - 1)) == 0)).astype(F32)


def _seq_lane_mask(rows, chunk, nseq):
    sh = int(math.log2(chunk))
    r = _iota((rows, nseq * HEAD_DIM), 0)
    c = _iota((rows, nseq * HEAD_DIM), 1)
    return jnp.right_shift(c, 6) == jnp.right_shift(r, sh)


def _fold_mat(nseq):
    r = _iota((nseq * HEAD_DIM, HEAD_DIM), 0)
    c = _iota((nseq * HEAD_DIM, HEAD_DIM), 1)
    return ((r & (HEAD_DIM - 1)) == c).astype(F32)


def _shifted_rows(pre, cur, nshift):
    bb, L, C = cur.shape
    full = jnp.concatenate([pre, cur], axis=1).reshape(bb * (L + SUBLANES), C)
    out = []
    for k in range(1, nshift + 1):
        sh = pltpu.roll(full, k, axis=0).reshape(bb, L + SUBLANES, C)
        out.append(sh[:, SUBLANES:, :])
    return out


def _layernorm(z, g, b):
    mu = jnp.mean(z, axis=-1, keepdims=True)
    zc = z - mu
    var = jnp.mean(zc * zc, axis=-1, keepdims=True)
    return zc * lax.rsqrt(var + 1e-5) * g + b


def _cparams(sem):
    return pltpu.CompilerParams(dimension_semantics=sem, vmem_limit_bytes=VMEM_LIMIT)


def _ada_kernel(c_ref, w_ref, b_ref, o_ref):
    c = c_ref[...]
    o_ref[...] = _mm(_silu(c), w_ref[...], 3) + b_ref[...]


def _ada(c_all, ada_w, ada_b):
    nb = c_all.shape[0]
    tn = 1536
    return pl.pallas_call(
        _ada_kernel,
        grid=(DEPTH, 6 * D_MODEL // tn),
        in_specs=[
            pl.BlockSpec((nb, D_MODEL), lambda l, n: (0, 0)),
            pl.BlockSpec((None, D_MODEL, tn), lambda l, n: (l, 0, n)),
            pl.BlockSpec((None, 1, tn), lambda l, n: (l, 0, n)),
        ],
        out_specs=pl.BlockSpec((None, nb, tn), lambda l, n: (l, 0, n)),
        out_shape=jax.ShapeDtypeStruct((DEPTH, nb, 6 * D_MODEL), F32),
        compiler_params=_cparams(("arbitrary", "arbitrary")),
        name="ada_mod",
    )(c_all, ada_w, ada_b.reshape(DEPTH, 1, 6 * D_MODEL))


def _modmm_kernel(x_ref, sh_ref, sc_ref, w_ref, *o_refs, splits, bf16_outs):
    x = x_ref[...]
    bb, L, D = x.shape
    h = (x * (1.0 + sc_ref[...]) + sh_ref[...]).reshape(bb * L, D).astype(BF16)
    o = jnp.dot(h, w_ref[...], preferred_element_type=F32)
    for (a, b), o_ref, as_bf16 in zip(splits, o_refs, bf16_outs):
        piece = o[:, a:b].reshape(bb, L, b - a)
        o_ref[...] = piece.astype(BF16) if as_bf16 else piece


def _modmm(x, mod, sh_col, sc_col, w, splits, bf16_outs, bb, L, name):
    B, T, D = x.shape
    N = w.shape[1]
    kern = functools.partial(_modmm_kernel, splits=splits, bf16_outs=bf16_outs)
    return pl.pallas_call(
        kern,
        grid=(B // bb, T // L),
        in_specs=[
            pl.BlockSpec((bb, L, D), lambda i, j: (i, j, 0)),
            pl.BlockSpec((bb, 1, D), lambda i, j: (i, 0, sh_col)),
            pl.BlockSpec((bb, 1, D), lambda i, j: (i, 0, sc_col)),
            pl.BlockSpec((D, N), lambda i, j: (0, 0)),
        ],
        out_specs=[pl.BlockSpec((bb, L, b - a), lambda i, j: (i, j, 0)) for a, b in splits],
        out_shape=[jax.ShapeDtypeStruct((B, T, b - a), BF16 if q else F32) for (a, b), q in zip(splits, bf16_outs)],
        compiler_params=_cparams(("arbitrary", "arbitrary")),
        name=name,
    )(x, mod, mod, w)


def _outproj_kernel(ya_ref, yb_ref, yc_ref, x_ref, g_ref, w_ref, lg_ref, lb_ref, o_ref):
    x = x_ref[...]
    bb, L, D = x.shape
    rows = bb * L
    ya = ya_ref[...].reshape(rows, A_WIDTH).astype(BF16)
    yb = yb_ref[...].reshape(rows, B_WIDTH).astype(BF16)
    yc = yc_ref[...].reshape(rows, C_WIDTH).astype(BF16)
    y = (jnp.dot(ya, w_ref[0:A_WIDTH, :], preferred_element_type=F32)
         + jnp.dot(yb, w_ref[A_WIDTH:A_WIDTH + B_WIDTH, :], preferred_element_type=F32)
         + jnp.dot(yc, w_ref[A_WIDTH + B_WIDTH:, :], preferred_element_type=F32))
    z = ALPHA * x + (1.0 + g_ref[...]) * y.reshape(bb, L, D)
    o_ref[...] = _layernorm(z, lg_ref[...], lb_ref[...])


def _outproj(ya, yb, yc, x, mod, w_out, ln_g, ln_b, bb, L):
    B, T, D = x.shape
    blk = lambda w: pl.BlockSpec((bb, L, w), lambda i, j: (i, j, 0))
    return pl.pallas_call(
        _outproj_kernel,
        grid=(B // bb, T // L),
        in_specs=[
            blk(A_WIDTH), blk(B_WIDTH), blk(C_WIDTH), blk(D),
            pl.BlockSpec((bb, 1, D), lambda i, j: (i, 0, 2)),
            pl.BlockSpec((D, D), lambda i, j: (0, 0)),
            pl.BlockSpec((1, D), lambda i, j: (0, 0)),
            pl.BlockSpec((1, D), lambda i, j: (0, 0)),
        ],
        out_specs=blk(D),
        out_shape=jax.ShapeDtypeStruct((B, T, D), F32),
        compiler_params=_cparams(("arbitrary", "arbitrary")),
        name="outproj_ln",
    )(ya, yb, yc, x, mod, w_out, ln_g, ln_b)


def _ffn_down_kernel(ug_ref, uv_ref, halo_ref, st_ref, x_ref, g_ref, cw_ref, cb_ref, w_ref, lg_ref, lb_ref, o_ref):
    ug = ug_ref[...]
    bb, L, N = ug.shape
    first = pl.program_id(1) == 0
    pre = jnp.where(first, st_ref[...], halo_ref[...])
    u1, u2 = _shifted_rows(pre, ug, FFN_CONV - 1)
    cw = cw_ref[...]
    conv = cb_ref[...] + ug * cw[2:3, :] + u1 * cw[1:2, :] + u2 * cw[0:1, :]
    a = (_silu(conv) * uv_ref[...]).reshape(bb * L, N).astype(BF16)
    y = jnp.dot(a, w_ref[...], preferred_element_type=F32)
    x = x_ref[...]
    z = ALPHA * x + (1.0 + g_ref[...]) * y.reshape(x.shape)
    o_ref[...] = _layernorm(z, lg_ref[...], lb_ref[...])


def _ffn_down(u, st8, x, mod, conv_w, conv_b, w_down, ln_g, ln_b, bb, L):
    B, T, D = x.shape
    N = D_FF_PAD
    lb8 = L // SUBLANES
    return pl.pallas_call(
        _ffn_down_kernel,
        grid=(B // bb, T // L),
        in_specs=[
            pl.BlockSpec((bb, L, N), lambda i, j: (i, j, 0)),
            pl.BlockSpec((bb, L, N), lambda i, j: (i, j, 1)),
            pl.BlockSpec((bb, SUBLANES, N), lambda i, j: (i, jnp.maximum(j * lb8 - 1, 0), 0)),
            pl.BlockSpec((bb, SUBLANES, N), lambda i, j: (i, 0, 0)),
            pl.BlockSpec((bb, L, D), lambda i, j: (i, j, 0)),
            pl.BlockSpec((bb, 1, D), lambda i, j: (i, 0, 5)),
            pl.BlockSpec((SUBLANES, N), lambda i, j: (0, 0)),
            pl.BlockSpec((1, N), lambda i, j: (0, 0)),
            pl.BlockSpec((N, D), lambda i, j: (0, 0)),
            pl.BlockSpec((1, D), lambda i, j: (0, 0)),
            pl.BlockSpec((1, D), lambda i, j: (0, 0)),
        ],
        out_specs=pl.BlockSpec((bb, L, D), lambda i, j: (i, j, 0)),
        out_shape=jax.ShapeDtypeStruct((B, T, D), F32),
        compiler_params=_cparams(("arbitrary", "arbitrary")),
        name="ffn_down_ln",
    )(u, u, u, st8, x, mod, conv_w, conv_b, w_down, ln_g, ln_b)


def _unit_lower_inverse(n_mat, chunk):
    rows = n_mat.shape[0]
    eye = (_iota((rows, rows), 0) == _iota((rows, rows), 1)).astype(F32)
    p = n_mat.astype(BF16)
    t = eye + n_mat
    for _ in range(int(math.log2(chunk)) - 1):
        p32 = _dot(p, p)
        p = p32.astype(BF16)
        t = t + _dot(p, t.astype(BF16))
    resid = (eye - t) + _mm(n_mat, t, 3)
    return t + _dot(t.astype(BF16), resid.astype(BF16))


def _rwkv_kernel(pa_ref, halo_ref, sh_ref, st0_ref, mu_ref, pv_ref, lw_ref, ya_ref, sto_ref, st_ref, *, chunk):
    c = pl.program_id(1)
    bb, L, _ = pa_ref.shape
    R = bb * L
    W = A_WIDTH

    @pl.when(c == 0)
    def _():
        st_ref[...] = st0_ref[...]

    pa = pa_ref[...]
    pre = jnp.where(c == 0, sh_ref[...], halo_ref[...])
    (prev,) = _shifted_rows(pre, pa, 1)
    x = pa.reshape(R, 4 * W)
    pm = x + (prev.reshape(R, 4 * W) - x) * mu_ref[...]
    r = pm[:, 0:W]
    k = pm[:, W:2 * W]
    v = pm[:, 2 * W:3 * W]
    lo = pm[:, 3 * W:4 * W]
    lane = _iota((R, W), 1)
    z = jnp.where(lane < 64, jnp.tanh(lo), jnp.where(lane < 128, lo, _sigmoid(lo)))
    lora = _mm(z, lw_ref[...], 3)
    pv = pv_ref[...]
    w0, a0, k_k, k_a, r_k, ln_g, ln_b = (pv[i:i + 1, :] for i in range(7))
    w = -_softplus(-(w0 + lora[:, 0:W])) - 0.5
    a = _sigmoid(a0 + lora[:, W:2 * W])
    g = lora[:, 2 * W:3 * W]
    ones_h = _head_ones(W)
    kk = k * k_k
    kk = kk / jnp.maximum(jnp.sqrt(_mm_xsel(kk * kk, ones_h)), 1e-12)
    k2 = k * (1.0 + (a - 1.0) * k_a)
    lw = -jnp.exp(w)
    same, incl, strict = _block_masks(R, L)
    cum = _mm_sel(incl.astype(F32), lw)
    tot = _mm_sel(same.astype(F32), lw)
    e_neg = jnp.exp(-cum)
    e_rem = jnp.exp(tot - cum)
    kb = kk * a
    a_t = -kk * jnp.exp(cum - lw)
    b_t = kb * e_neg
    k_t = k2 * e_neg
    r_t = r * jnp.exp(cum)
    bh_t = jnp.transpose(kb * e_rem)
    kh_t = jnp.transpose(k2 * e_rem)
    gam_t = jnp.transpose(jnp.exp(tot))
    per_seq = bb <= 4
    if not per_seq:
        expand = _expand_mat(R, L, bb)
        lmask = _seq_lane_mask(R, L, bb)
        lmask2 = jnp.concatenate([lmask, lmask], axis=0)
        fold = _fold_mat(bb)
        tile = jnp.transpose(fold)
    outs = []
    for h in range(A_HEADS):
        hs = slice(h * HEAD_DIM, (h + 1) * HEAD_DIM)
        st_h = st_ref[h]
        A, Bt, Kt, Rt, V = a_t[:, hs], b_t[:, hs], k_t[:, hs], r_t[:, hs], v[:, hs]
        m_ab = jnp.where(strict, _mm(A, Bt, 3, NT), 0.0)
        m_ak = jnp.where(strict, _mm(A, Kt, 1, NT), 0.0)
        g_r = _mm(Rt, jnp.concatenate([Bt, Kt], axis=0), 1, NT)
        m_rb = jnp.where(incl, g_r[:, 0:R], 0.0)
        m_rk = jnp.where(incl, g_r[:, R:2 * R], 0.0)
        if per_seq:
            ps = [_mm(jnp.concatenate([A[b * L:(b + 1) * L], Rt[b * L:(b + 1) * L]], axis=0),
                      st_h[:, b * HEAD_DIM:(b + 1) * HEAD_DIM], 1) for b in range(bb)]
            ps_a = jnp.concatenate([p[0:L] for p in ps], axis=0)
            ps_r = jnp.concatenate([p[L:2 * L] for p in ps], axis=0)
        else:
            X = jnp.concatenate([A, Rt], axis=0)
            PS = _mm_xsel(jnp.where(lmask2, _mm(X, st_h, 1), 0.0), fold)
            ps_a, ps_r = PS[0:R], PS[R:2 * R]
        Y = ps_a + _mm(m_ak, V, 1)
        U = _mm(_unit_lower_inverse(m_ab, L), Y, 3)
        UV = jnp.concatenate([U, V], axis=0)
        outs.append(ps_r + _mm(jnp.concatenate([m_rb, m_rk], axis=1), UV, 1))
        if per_seq:
            new = []
            for b in range(bb):
                ts = slice(b * L, (b + 1) * L)
                lhs = jnp.concatenate([bh_t[hs, ts], kh_t[hs, ts]], axis=1)
                uv_b = jnp.concatenate([U[ts], V[ts]], axis=0)
                new.append(gam_t[hs, b * L:b * L + 1] * st_h[:, b * HEAD_DIM:(b + 1) * HEAD_DIM] + _mm(lhs, uv_b, 1))
            st_ref[h] = jnp.concatenate(new, axis=1)
        else:
            UVb = jnp.where(lmask2, _mm_xsel(UV, tile), 0.0)
            lhs = jnp.concatenate([bh_t[hs, :], kh_t[hs, :]], axis=1)
            gam = _mm_xsel(gam_t[hs, :], expand)
            st_ref[h] = gam * st_h + _mm(lhs, UVb, 1)
    o = jnp.concatenate(outs, axis=1)
    inv = 1.0 / HEAD_DIM
    mu = _mm_xsel(o, ones_h) * inv
    oc = o - mu
    var = _mm_xsel(oc * oc, ones_h) * inv
    y = oc * lax.rsqrt(var + 64e-5) * ln_g + ln_b
    y = y + _mm_xsel(r * k2 * r_k, ones_h) * v
    ya_ref[...] = (y * g).reshape(bb, L, W)

    @pl.when(c == pl.num_programs(1) - 1)
    def _():
        sto_ref[...] = st_ref[...]


def _rwkv(pa, shift8, st0, mu, pvec, lora_w, bb, L):
    B, T, _ = pa.shape
    lb8 = L // SUBLANES
    kern = functools.partial(_rwkv_kernel, chunk=L)
    return pl.pallas_call(
        kern,
        grid=(B // bb, T // L),
        in_specs=[
            pl.BlockSpec((bb, L, 4 * A_WIDTH), lambda i, j: (i, j, 0)),
            pl.BlockSpec((bb, SUBLANES, 4 * A_WIDTH), lambda i, j: (i, jnp.maximum(j * lb8 - 1, 0), 0)),
            pl.BlockSpec((bb, SUBLANES, 4 * A_WIDTH), lambda i, j: (i, 0, 0)),
            pl.BlockSpec((A_HEADS, HEAD_DIM, bb * HEAD_DIM), lambda i, j: (0, 0, i)),
            pl.BlockSpec((1, 4 * A_WIDTH), lambda i, j: (0, 0)),
            pl.BlockSpec((SUBLANES, A_WIDTH), lambda i, j: (0, 0)),
            pl.BlockSpec((A_WIDTH, 3 * A_WIDTH), lambda i, j: (0, 0)),
        ],
        out_specs=[
            pl.BlockSpec((bb, L, A_WIDTH), lambda i, j: (i, j, 0)),
            pl.BlockSpec((A_HEADS, HEAD_DIM, bb * HEAD_DIM), lambda i, j: (0, 0, i)),
        ],
        out_shape=[
            jax.ShapeDtypeStruct((B, T, A_WIDTH), F32),
            jax.ShapeDtypeStruct((A_HEADS, HEAD_DIM, B * HEAD_DIM), F32),
        ],
        scratch_shapes=[pltpu.VMEM((A_HEADS, HEAD_DIM, bb * HEAD_DIM), F32)],
        compiler_params=_cparams(("arbitrary", "arbitrary")),
        name="rwkv7",
    )(pa, pa, shift8, st0, mu, pvec, lora_w)


def _prep_rwkv(mu, w0, w2, a0, a2, g2, k_k, k_a, r_k, ln_g, ln_b):
    zero = jnp.zeros((A_WIDTH,), F32)
    pvec = jnp.stack([w0, a0, k_k, k_a, r_k.reshape(A_WIDTH), ln_g, ln_b, zero])
    lora = jnp.zeros((A_WIDTH, 3 * A_WIDTH), F32)
    lora = lora.at[0:64, 0:A_WIDTH].set(w2)
    lora = lora.at[64:128, A_WIDTH:2 * A_WIDTH].set(a2)
    lora = lora.at[128:256, 2 * A_WIDTH:].set(g2)
    return mu.reshape(1, 4 * A_WIDTH), pvec, lora


def _state_to_lanes(s):
    B, H = s.shape[:2]
    return jnp.transpose(s, (1, 3, 0, 2)).reshape(H, HEAD_DIM, B * HEAD_DIM)


def _state_from_lanes(st, B):
    H = st.shape[0]
    return jnp.transpose(st.reshape(H, HEAD_DIM, B, HEAD_DIM), (2, 0, 3, 1))


def _mlstm_kernel(pb_ref, halo_ref, cv_ref, pif_ref, gt_ref, ct0_ref, n0_ref, m0_ref, cw_ref, cb_ref, bif_ref,
                  brow_ref, ng_ref, yb_ref, cto_ref, no_ref, mo_ref, ct_ref, nt_ref, m_ref):
    c = pl.program_id(1)
    bb, L, _ = pb_ref.shape
    R = bb * L
    W = B_WIDTH
    neg = -jnp.inf

    @pl.when(c == 0)
    def _():
        ct_ref[...] = ct0_ref[...]
        nt_ref[...] = n0_ref[...]
        m_ref[...] = jnp.broadcast_to(m0_ref[...], m_ref.shape)

    pb = pb_ref[...]
    qk_in = pb[:, :, 0:2 * W]
    pre = jnp.where(c == 0, cv_ref[...], halo_ref[...])
    s1, s2, s3 = _shifted_rows(pre, qk_in, MLSTM_CONV - 1)
    cw = cw_ref[...]
    conv = cb_ref[...] + qk_in * cw[3:4, :] + s1 * cw[2:3, :] + s2 * cw[1:2, :] + s3 * cw[0:1, :]
    qk = _silu(conv).reshape(R, 2 * W)
    q = qk[:, 0:W]
    k = qk[:, W:2 * W] * (HEAD_DIM ** -0.5)
    v = pb[:, :, 2 * W:3 * W].reshape(R, W)
    og = pb[:, :, 3 * W:4 * W].reshape(R, W)

    same, incl, _ = _block_masks(R, L)
    same_f = same.astype(F32)
    incl_f = incl.astype(F32)
    gc = pif_ref[...].reshape(R, LANES) + bif_ref[...]
    lane = _iota((R, LANES), 1)
    lfc = jnp.where((lane >= B_HEADS) & (lane < 2 * B_HEADS), _log_sigmoid(gc), 0.0)
    bcum_c = _mm_sel(incl_f, lfc)
    btot_c = _mm_sel(same_f, lfc)
    gr = gt_ref[...] + brow_ref[...]
    row = _iota((SUBLANES, R), 0)
    lfr = jnp.where(row >= B_HEADS, _log_sigmoid(gr), 0.0)
    bcum_r = _mm_xsel(lfr, incl_f, NT)
    btot_r = _mm_xsel(lfr, same_f)
    m_col = jnp.broadcast_to(m_ref[:, 0:1, :], (bb, L, LANES)).reshape(R, LANES)

    lmask = _seq_lane_mask(R, L, bb)
    fold = _fold_mat(bb)
    tile = jnp.transpose(fold)
    expand = _expand_mat(R, L, bb)
    sh = int(math.log2(L))
    rl = _iota((R, LANES), 0)
    blockind = (jnp.right_shift(rl, sh) == lane).astype(F32)
    firstind = ((jnp.right_shift(rl, sh) == lane) & ((rl & (L - 1)) == 0)).astype(F32)

    houts, kws, wcs = [], [], []
    m_new_all = jnp.zeros((R, LANES), F32)
    for h in range(B_HEADS):
        hs = slice(h * HEAD_DIM, (h + 1) * HEAD_DIM)
        Q, K, V = q[:, hs], k[:, hs], v[:, hs]
        b_c = bcum_c[:, B_HEADS + h:B_HEADS + h + 1]
        b_r = bcum_r[B_HEADS + h:B_HEADS + h + 1, :]
        i_r = gr[h:h + 1, :]
        i_c = gc[:, h:h + 1]
        m_c = m_col[:, h:h + 1]
        D = jnp.where(incl, b_c - b_r + i_r, neg)
        inter = b_c + m_c
        mt = jnp.maximum(inter, jnp.max(D, axis=1, keepdims=True))
        S = _mm(Q, K, 1, NT) * jnp.exp(D - mt)
        iw = jnp.exp(inter - mt)
        QC = _mm_xsel(jnp.where(lmask, _mm(Q, ct_ref[h], 3), 0.0), fold)
        num = _mm(S, V, 1) + iw * QC
        qn = jnp.sum(_mm(Q, nt_ref[h], 3) * blockind, axis=1, keepdims=True)
        den = jnp.sum(S, axis=1, keepdims=True) + iw * qn
        houts.append(num / jnp.maximum(jnp.abs(den), jnp.exp(-mt)))
        bl_c = btot_c[:, B_HEADS + h:B_HEADS + h + 1]
        bl_r = btot_r[B_HEADS + h:B_HEADS + h + 1, :]
        gs_c = bl_c - b_c + i_c
        gmax = jnp.max(jnp.where(same, bl_r - b_r + i_r, neg), axis=1, keepdims=True)
        m_new = jnp.maximum(bl_c + m_c, gmax)
        kws.append(K * jnp.exp(gs_c - m_new))
        wcs.append(jnp.exp(bl_c + m_c - m_new))
        m_new_all = jnp.where(lane == h, m_new, m_new_all)

    kw_t = jnp.transpose(jnp.concatenate(kws, axis=1))
    for h in range(B_HEADS):
        hs = slice(h * HEAD_DIM, (h + 1) * HEAD_DIM)
        vb = jnp.where(lmask, _mm_xsel(v[:, hs], tile), 0.0)
        wc_row = jnp.sum(wcs[h] * expand, axis=0, keepdims=True)
        ct_ref[h] = wc_row * ct_ref[h] + _mm(kw_t[hs, :], vb, 3)
        wc_lane = jnp.sum(wcs[h] * firstind, axis=0, keepdims=True)
        nt_ref[h] = wc_lane * nt_ref[h] + _mm_xsel(kw_t[hs, :], blockind)
    m_ref[...] = m_new_all.reshape(bb, L, LANES)[:, 0:SUBLANES, :]

    hcat = jnp.concatenate(houts, axis=1)
    ones_h = _head_ones(W)
    inv = 1.0 / HEAD_DIM
    mu = _mm_xsel(hcat, ones_h) * inv
    hc = hcat - mu
    var = _mm_xsel(hc * hc, ones_h) * inv
    hn = hc * lax.rsqrt(var + 1e-5) * ng_ref[...]
    yb_ref[...] = (_sigmoid(og) * hn).reshape(bb, L, W)

    @pl.when(c == pl.num_programs(1) - 1)
    def _():
        cto_ref[...] = ct_ref[...]
        no_ref[...] = nt_ref[...]
        mo_ref[...] = m_ref[...]


def _mlstm(pb, conv8, pif, ct0, n0, m0, conv_w8, conv_b, bias_if, norm_g, bb, L):
    B, T, _ = pb.shape
    R = bb * L
    nbi, nch = B // bb, T // L
    lb8 = L // SUBLANES
    g_t = pif[:, :, 0:SUBLANES].reshape(nbi, bb, nch, L, SUBLANES).transpose(0, 2, 4, 1, 3).reshape(nbi, nch, SUBLANES, R)
    bias_row = jnp.broadcast_to(bias_if[0, 0:SUBLANES].reshape(SUBLANES, 1), (SUBLANES, R))
    n_in = jnp.pad(n0.reshape(nbi, bb, B_HEADS, HEAD_DIM).transpose(0, 2, 3, 1), ((0, 0), (0, 0), (0, 0), (0, LANES - bb)))
    m_in = jnp.pad(m0, ((0, 0), (0, LANES - B_HEADS))).reshape(B, 1, LANES)
    yb, ct, nt, mo = pl.pallas_call(
        _mlstm_kernel,
        grid=(nbi, nch),
        in_specs=[
            pl.BlockSpec((bb, L, 4 * B_WIDTH), lambda i, j: (i, j, 0)),
            pl.BlockSpec((bb, SUBLANES, 2 * B_WIDTH), lambda i, j: (i, jnp.maximum(j * lb8 - 1, 0), 0)),
            pl.BlockSpec((bb, SUBLANES, 2 * B_WIDTH), lambda i, j: (i, 0, 0)),
            pl.BlockSpec((bb, L, LANES), lambda i, j: (i, j, 0)),
            pl.BlockSpec((None, None, SUBLANES, R), lambda i, j: (i, j, 0, 0)),
            pl.BlockSpec((B_HEADS, HEAD_DIM, bb * HEAD_DIM), lambda i, j: (0, 0, i)),
            pl.BlockSpec((None, B_HEADS, HEAD_DIM, LANES), lambda i, j: (i, 0, 0, 0)),
            pl.BlockSpec((bb, 1, LANES), lambda i, j: (i, 0, 0)),
            pl.BlockSpec((SUBLANES, 2 * B_WIDTH), lambda i, j: (0, 0)),
            pl.BlockSpec((1, 2 * B_WIDTH), lambda i, j: (0, 0)),
            pl.BlockSpec((1, LANES), lambda i, j: (0, 0)),
            pl.BlockSpec((SUBLANES, R), lambda i, j: (0, 0)),
            pl.BlockSpec((1, B_WIDTH), lambda i, j: (0, 0)),
        ],
        out_specs=[
            pl.BlockSpec((bb, L, B_WIDTH), lambda i, j: (i, j, 0)),
            pl.BlockSpec((B_HEADS, HEAD_DIM, bb * HEAD_DIM), lambda i, j: (0, 0, i)),
            pl.BlockSpec((None, B_HEADS, HEAD_DIM, LANES), lambda i, j: (i, 0, 0, 0)),
            pl.BlockSpec((bb, SUBLANES, LANES), lambda i, j: (i, 0, 0)),
        ],
        out_shape=[
            jax.ShapeDtypeStruct((B, T, B_WIDTH), F32),
            jax.ShapeDtypeStruct((B_HEADS, HEAD_DIM, B * HEAD_DIM), F32),
            jax.ShapeDtypeStruct((nbi, B_HEADS, HEAD_DIM, LANES), F32),
            jax.ShapeDtypeStruct((B, SUBLANES, LANES), F32),
        ],
        scratch_shapes=[
            pltpu.VMEM((B_HEADS, HEAD_DIM, bb * HEAD_DIM), F32),
            pltpu.VMEM((B_HEADS, HEAD_DIM, LANES), F32),
            pltpu.VMEM((bb, SUBLANES, LANES), F32),
        ],
        compiler_params=_cparams(("arbitrary", "arbitrary")),
        name="mlstm",
    )(pb, pb, conv8, pif, g_t, ct0, n_in, m_in, conv_w8, conv_b, bias_if, bias_row, norm_g)
    n_new = nt[:, :, :, 0:bb].transpose(0, 3, 1, 2).reshape(B, B_HEADS, HEAD_DIM)
    return yb, ct, n_new, mo[:, 0, 0:B_HEADS]


def _masked_softmax(s, mask):
    mx = jnp.max(jnp.where(mask, s, NEG_INF), axis=1, keepdims=True)
    e = jnp.where(mask, jnp.exp(s - mx), 0.0)
    return e / jnp.maximum(jnp.sum(e, axis=1, keepdims=True), 1e-30)


def _select_blocks(psum, m_mat, t_col, n_s):
    tq = psum.shape[0]
    imp = _mm_xsel(psum, m_mat)
    blk = _iota((tq, LANES), 1)
    cur = jnp.right_shift(t_col, 6)
    valid = (blk * SEL_BLOCK <= t_col)
    forced = (blk == 0) | (blk == cur) | (blk == cur - 1)
    score = jnp.where(valid, imp + jnp.where(forced, FORCE_BONUS, 0.0), NEG_INF)
    score = jnp.where(blk < n_s, score, -jnp.inf)
    n_sel = min(SEL_TOP, n_s)
    if tq == LANES and n_s % SUBLANES == 0:
        sc = jnp.transpose(score)[0:n_s, :]
        idx = _iota((n_s, tq), 0)
        rank = jnp.zeros((n_s, tq), F32)
        for s in range(n_s):
            row = sc[s:s + 1, :]
            ahead = (row > sc) | ((row == sc) & (idx > s))
            rank = rank + jnp.where(ahead, 1.0, 0.0)
        sel_t = jnp.where(rank < n_sel, 1.0, 0.0)
        if n_s < LANES:
            sel_t = jnp.concatenate([sel_t, jnp.zeros((LANES - n_s, tq), F32)], axis=0)
        return jnp.transpose(sel_t)
    rank = jnp.zeros((tq, LANES), F32)
    for s in range(n_s):
        col = score[:, s:s + 1]
        ahead = (col > score) | ((col == score) & (blk > s))
        rank = rank + jnp.where(ahead, 1.0, 0.0)
    return jnp.where(rank < n_sel, 1.0, 0.0)


def _stack_heads(pq, g):
    base = g * C_GROUP * HEAD_DIM
    parts = [pq[:, base + r * HEAD_DIM: base + (r + 1) * HEAD_DIM] for r in range(C_GROUP)]
    return jnp.concatenate(parts, axis=0) * (HEAD_DIM ** -0.5)


def _to_group_lanes(q, g):
    z = jnp.zeros_like(q)
    return jnp.concatenate([q, z] if g == 0 else [z, q], axis=1)


def _compress_kernel(x_ref, pw0_ref, pw1_ref, cw_ref, kc_ref, a_ref, *, rows_per_step):
    T = x_ref.shape[0]
    rs = rows_per_step
    ng = rs // CMP_STRIDE
    pool = (jnp.right_shift(_iota((ng, rs), 1), 4) == _iota((ng, rs), 0)).astype(F32)
    for c in range(T // rs):
        x = x_ref[c * rs:(c + 1) * rs, :]
        xw = jnp.concatenate([x * pw0_ref[...], x * pw1_ref[...]], axis=1)
        a_ref[c * ng:(c + 1) * ng, :] = _mm_sel(pool, xw)
    ngrp = T // CMP_STRIDE
    W = x_ref.shape[1]
    pooled = a_ref[:, 0:W] + pltpu.roll(a_ref[:, W:2 * W], ngrp - 1, axis=0)
    kc_ref[...] = _mm(pooled, cw_ref[...], 3)


def _compress(pcmp, pwt0, pwt1, cmpw_bd):
    B, T, W = pcmp.shape
    rs = pwt0.shape[0]
    ngrp = T // CMP_STRIDE
    kern = functools.partial(_compress_kernel, rows_per_step=rs)
    return pl.pallas_call(
        kern,
        grid=(B,),
        in_specs=[
            pl.BlockSpec((None, T, W), lambda b: (b, 0, 0)),
            pl.BlockSpec((rs, W), lambda b: (0, 0)),
            pl.BlockSpec((rs, W), lambda b: (0, 0)),
            pl.BlockSpec((W, W), lambda b: (0, 0)),
        ],
        out_specs=pl.BlockSpec((None, ngrp, W), lambda b: (b, 0, 0)),
        out_shape=jax.ShapeDtypeStruct((B, ngrp, W), F32),
        scratch_shapes=[pltpu.VMEM((ngrp, 2 * W), F32)],
        compiler_params=_cparams(("arbitrary",)),
        name="nsa_compress",
    )(pcmp, pwt0, pwt1, cmpw_bd)


def _combine_branches(gates, g, o_c, o_s, o_w, tq):
    outs = []
    for r in range(C_GROUP):
        rs = slice(r * tq, (r + 1) * tq)
        j = (g * C_GROUP + r) * 3
        outs.append(gates[:, j:j + 1] * o_c[rs] + gates[:, j + 1:j + 2] * o_s[rs] + gates[:, j + 2:j + 3] * o_w[rs])
    return outs


def _nsa_prompt_kernel(pq_ref, pg_ref, gb_ref, kc_ref, kv_ref, m_ref, e_ref, wb_ref, o_ref, *, n_s):
    i = pl.program_id(1)
    tq = pq_ref.shape[0]
    ngrp = kc_ref.shape[0]
    kt = 4 * tq
    t0 = i * tq
    rows = C_GROUP * tq
    tl = _iota((rows, 1), 0) & (tq - 1)
    t_row = t0 + tl
    t_col = t0 + _iota((tq, 1), 0)
    pq = pq_ref[...]
    gates = _sigmoid(pg_ref[...] + gb_ref[...])
    kc = kc_ref[...]
    n_end = _iota((rows, ngrp), 1) * CMP_STRIDE + (CMP_BLOCK - 1)
    cmask = n_end <= t_row
    n_full = t0 // kt
    diag_bias = jnp.where(_iota((rows, kt), 1) <= (t0 - n_full * kt) + tl, 0.0, NEG_INF)
    n_tiles = WINDOW // tq + 1
    pieces = []
    for g in range(C_KV_HEADS):
        gs = slice(g * HEAD_DIM, (g + 1) * HEAD_DIM)
        q = _stack_heads(pq, g)
        q2 = _to_group_lanes(q, g).astype(BF16)
        p_c = _masked_softmax(_mm(q, kc[:, g * HEAD_DIM:(g + 1) * HEAD_DIM], 3, NT), cmask)
        o_c = _mm(p_c, kc[:, 2 * HEAD_DIM + g * HEAD_DIM: 2 * HEAD_DIM + (g + 1) * HEAD_DIM], 1)
        psum = p_c[0:tq]
        for r in range(1, C_GROUP):
            psum = psum + p_c[r * tq:(r + 1) * tq]
        sel = _select_blocks(psum, m_ref[...], t_col, n_s).astype(BF16)

        def step(j, carry, extra):
            m, l, acc = carry
            off = pl.multiple_of(j * kt, kt)
            kk = kv_ref[pl.ds(off, kt), 0:LANES]
            vv = kv_ref[pl.ds(off, kt), LANES:2 * LANES]
            sel_bias = (_dot(sel, e_ref[:, pl.ds(off, kt)]) - 1.0) * (-NEG_INF)
            s = (_dot(q2, kk, NT).reshape(C_GROUP, tq, kt) + sel_bias[None]).reshape(rows, kt)
            if extra is not None:
                s = s + extra
            m_new = jnp.maximum(m, jnp.max(s, axis=1, keepdims=True))
            p = jnp.exp(s - m_new)
            alpha = jnp.exp(m - m_new)
            l = alpha * l + jnp.sum(p, axis=1, keepdims=True)
            acc = alpha * acc + _dot(p.astype(BF16), vv)
            return m_new, l, acc

        init = (jnp.full((rows, 1), NEG_INF, F32), jnp.zeros((rows, 1), F32), jnp.zeros((rows, LANES), F32))
        carry = lax.fori_loop(0, n_full, lambda j, c: step(j, c, None), init)
        _, l, acc = step(n_full, carry, diag_bias)
        o_s = acc[:, gs] / l

        s_tiles, vws = [], []
        for cidx in range(n_tiles):
            tile = i - (n_tiles - 1) + cidx
            off = pl.multiple_of(jnp.maximum(tile, 0) * tq, tq)
            kw = kv_ref[pl.ds(off, tq), 2 * LANES:3 * LANES]
            vws.append(kv_ref[pl.ds(off, tq), 3 * LANES:4 * LANES])
            tile_bias = jnp.where(tile >= 0, 0.0, NEG_INF)
            s_tiles.append(_dot(q2, kw, NT) + (wb_ref[:, cidx * tq:(cidx + 1) * tq] + tile_bias))
        s_w = jnp.concatenate(s_tiles, axis=1)
        e_w = jnp.exp(s_w - jnp.max(s_w, axis=1, keepdims=True))
        p_w = e_w / jnp.sum(e_w, axis=1, keepdims=True)
        o_w = _dot(p_w.astype(BF16), jnp.concatenate(vws, axis=0))[:, gs]
        pieces += _combine_branches(gates, g, o_c, o_s, o_w, tq)
    o_ref[...] = jnp.concatenate(pieces, axis=1)


def _nsa_prompt(pq, pg, gate_b, kc, kvb, m_mat, e_mat, tq):
    B, T, _ = pq.shape
    ngrp = kc.shape[1]
    n_s = T // SEL_BLOCK
    n_tiles = WINDOW // tq + 1
    kp = np.arange(n_tiles * tq)[None, :] - (n_tiles - 1) * tq
    tloc = (np.arange(C_GROUP * tq) % tq)[:, None]
    wbias = jnp.asarray(np.where((kp <= tloc) & (kp > tloc - WINDOW), 0.0, NEG_INF).astype(np.float32))
    kern = functools.partial(_nsa_prompt_kernel, n_s=n_s)
    return pl.pallas_call(
        kern,
        grid=(B, T // tq),
        in_specs=[
            pl.BlockSpec((None, tq, C_WIDTH), lambda b, i: (b, i, 0)),
            pl.BlockSpec((None, tq, LANES), lambda b, i: (b, i, 0)),
            pl.BlockSpec((1, LANES), lambda b, i: (0, 0)),
            pl.BlockSpec((None, ngrp, 4 * HEAD_DIM), lambda b, i: (b, 0, 0)),
            pl.BlockSpec((None, T, 4 * LANES), lambda b, i: (b, 0, 0)),
            pl.BlockSpec((ngrp, LANES), lambda b, i: (0, 0)),
            pl.BlockSpec((LANES, T), lambda b, i: (0, 0)),
            pl.BlockSpec((C_GROUP * tq, n_tiles * tq), lambda b, i: (0, 0)),
        ],
        out_specs=pl.BlockSpec((None, tq, C_WIDTH), lambda b, i: (b, i, 0)),
        out_shape=jax.ShapeDtypeStruct((B, T, C_WIDTH), F32),
        compiler_params=_cparams(("arbitrary", "arbitrary")),
        name="nsa_prompt",
    )(pq, pg, gate_b, kc, kvb, m_mat, e_mat, wbias)


def _nsa_sample_kernel(pt_ref, *refs, n_pages, past_len, n_s, n_c):
    pages = refs[:n_pages]
    (pq_ref, pg_ref, gb_ref, pcmp_ref, pslc_ref, pwin_ref, win_ref, pw0_ref, pw1_ref, cw_ref, m_ref, e_ref,
     o_ref, k_ref, v_ref) = refs[n_pages:]
    tq = pq_ref.shape[0]
    rows = C_GROUP * tq
    W = 4 * HEAD_DIM
    ng = PAGE_SIZE // CMP_STRIDE
    ngrp = m_ref.shape[0]
    pw0 = pw0_ref[...]
    pw1 = pw1_ref[...]

    def tail_t(x):
        return jnp.transpose(jnp.concatenate([x, jnp.zeros((PAGE_SIZE - tq, x.shape[1]), F32)], axis=0))

    xs = [pages[p][0:W, :] for p in range(n_pages)] + [tail_t(pcmp_ref[...])]
    a_acc = jnp.zeros((2 * W, ngrp), F32)
    for p0 in range(0, n_pages + 1, 2):
        grp = xs[p0:p0 + 2]
        xw = jnp.concatenate([jnp.concatenate([x * pw0, x * pw1], axis=0) for x in grp], axis=1)
        kdim = xw.shape[1]
        sel = (jnp.right_shift(_iota((kdim, ngrp), 0), 4) + p0 * ng == _iota((kdim, ngrp), 1)).astype(BF16)
        hi = xw.astype(BF16)
        lo = (xw - hi.astype(F32)).astype(BF16)
        a_acc = a_acc + (_dot(hi, sel) + _dot(lo, sel))
    pooled_t = a_acc[0:W, :] + pltpu.roll(a_acc[W:2 * W, :], ngrp - 1, axis=1)
    kc_t = _mm(cw_ref[...], pooled_t, 3)
    for p in range(n_pages):
        k_ref[:, p * PAGE_SIZE:(p + 1) * PAGE_SIZE] = pages[p][W:W + LANES, :].astype(BF16)
        v_ref[:, p * PAGE_SIZE:(p + 1) * PAGE_SIZE] = pages[p][W + LANES:W + 2 * LANES, :].astype(BF16)
    pslc_t = tail_t(pslc_ref[...])
    k_ref[:, past_len:past_len + PAGE_SIZE] = pslc_t[0:LANES].astype(BF16)
    v_ref[:, past_len:past_len + PAGE_SIZE] = pslc_t[LANES:2 * LANES].astype(BF16)

    nk = k_ref.shape[1]
    t_row = past_len + (_iota((rows, 1), 0) & (tq - 1))
    t_col = past_len + _iota((tq, 1), 0)
    pq = pq_ref[...]
    gates = _sigmoid(pg_ref[...] + gb_ref[...])
    n_idx = _iota((rows, ngrp), 1)
    cmask = (n_idx * CMP_STRIDE + (CMP_BLOCK - 1) <= t_row) & (n_idx < n_c)
    kpos = _iota((rows, nk), 1)
    win_t = win_ref[...]
    pwin_t = tail_t(pwin_ref[...])
    nwb = win_t.shape[1]
    kw_t = jnp.concatenate([win_t[0:LANES], pwin_t[0:LANES]], axis=1).astype(BF16)
    vw_t = jnp.concatenate([win_t[LANES:2 * LANES], pwin_t[LANES:2 * LANES]], axis=1).astype(BF16)
    jj = _iota((rows, nwb + PAGE_SIZE), 1)
    tl = _iota((rows, 1), 0) & (tq - 1)
    wmask = (jj > tl + (nwb - WINDOW)) & (jj <= tl + nwb) & (jj < nwb + tq)
    pieces = []
    for g in range(C_KV_HEADS):
        gs = slice(g * HEAD_DIM, (g + 1) * HEAD_DIM)
        q = _stack_heads(pq, g)
        q2 = _to_group_lanes(q, g).astype(BF16)
        p_c = _masked_softmax(_mm(q, kc_t[g * HEAD_DIM:(g + 1) * HEAD_DIM, :], 3), cmask)
        o_c = _mm(p_c, kc_t[2 * HEAD_DIM + g * HEAD_DIM: 2 * HEAD_DIM + (g + 1) * HEAD_DIM, :], 1, NT)
        psum = p_c[0:tq]
        for r in range(1, C_GROUP):
            psum = psum + p_c[r * tq:(r + 1) * tq]
        sel = _select_blocks(psum, m_ref[...], t_col, n_s)
        sel_r = jnp.concatenate([sel] * C_GROUP, axis=0).astype(BF16)
        smask = (_dot(sel_r, e_ref[...]) > 0.5) & (kpos <= t_row)
        p_s = _masked_softmax(_dot(q2, k_ref[...]), smask)
        o_s = _dot(p_s.astype(BF16), v_ref[...], NT)[:, gs]
        p_w = _masked_softmax(_dot(q2, kw_t), wmask)
        o_w = _dot(p_w.astype(BF16), vw_t, NT)[:, gs]
        pieces += _combine_branches(gates, g, o_c, o_s, o_w, tq)
    o_ref[...] = jnp.concatenate(pieces, axis=1)


def _nsa_sample(layer, page_table, cache_t, pq, pg, gate_b, pcmp, pslc, pwin, win_t, pwt0, pwt1, cmpw_t, m_mat, e_mat):
    B, T, _ = pq.shape
    n_pages = page_table.shape[1]
    past_len = n_pages * PAGE_SIZE
    lp = -(-(past_len + T) // SEL_BLOCK) * SEL_BLOCK
    n_s = lp // SEL_BLOCK
    n_c = lp // CMP_STRIDE - CMP_BLOCK // CMP_STRIDE + 1
    ngrp = m_mat.shape[0]
    nk = past_len + PAGE_SIZE
    nwb = win_t.shape[3]
    kern = functools.partial(_nsa_sample_kernel, n_pages=n_pages, past_len=past_len, n_s=n_s, n_c=n_c)
    page_specs = [
        pl.BlockSpec((None, None, 4 * LANES, PAGE_SIZE), functools.partial(lambda b, pt, p: (layer, pt[b, p], 0, 0), p=p))
        for p in range(n_pages)
    ]
    row = lambda w: pl.BlockSpec((None, T, w), lambda b, pt: (b, 0, 0))
    full = lambda a: pl.BlockSpec(a.shape, lambda b, pt: (0,) * a.ndim)
    grid_spec = pltpu.PrefetchScalarGridSpec(
        num_scalar_prefetch=1,
        grid=(B,),
        in_specs=page_specs + [
            row(C_WIDTH), row(LANES), full(gate_b), row(4 * HEAD_DIM), row(4 * HEAD_DIM), row(4 * HEAD_DIM),
            pl.BlockSpec((None, None, 4 * HEAD_DIM, nwb), lambda b, pt: (layer, b, 0, 0)),
            full(pwt0), full(pwt1), full(cmpw_t), full(m_mat), full(e_mat),
        ],
        out_specs=pl.BlockSpec((None, T, C_WIDTH), lambda b, pt: (b, 0, 0)),
        scratch_shapes=[
            pltpu.VMEM((LANES, nk), BF16),
            pltpu.VMEM((LANES, nk), BF16),
        ],
    )
    return pl.pallas_call(
        kern,
        grid_spec=grid_spec,
        out_shape=jax.ShapeDtypeStruct((B, T, C_WIDTH), F32),
        compiler_params=_cparams(("arbitrary",)),
        name="nsa_sample",
    )(page_table, *([cache_t] * n_pages), pq, pg, gate_b, pcmp, pslc, pwin, win_t, pwt0, pwt1, cmpw_t, m_mat, e_mat)


def _nsa_consts(lp, n_keys):
    n_str = lp // CMP_STRIDE
    n_c = n_str - CMP_BLOCK // CMP_STRIDE + 1
    n_s = lp // SEL_BLOCK
    c0 = np.arange(n_str)[:, None] * CMP_STRIDE
    s0 = np.arange(LANES)[None, :] * SEL_BLOCK
    m = (c0 < s0 + SEL_BLOCK) & (c0 + CMP_BLOCK > s0) & (np.arange(n_str)[:, None] < n_c) & (np.arange(LANES)[None, :] < n_s)
    e = (np.arange(n_keys)[None, :] // SEL_BLOCK) == np.arange(LANES)[:, None]
    return m.astype(np.float32), e.astype(np.float32)


def _prep_mlstm(conv_w, conv_b, i_b, f_b, norm_g):
    cw8 = jnp.pad(conv_w, ((0, SUBLANES - MLSTM_CONV), (0, 0)))
    bias_if = jnp.pad(jnp.concatenate([i_b, f_b]), (0, LANES - 2 * B_HEADS)).reshape(1, LANES)
    return cw8, conv_b.reshape(1, 2 * B_WIDTH), bias_if, norm_g.reshape(1, B_WIDTH)


_A0, _B0, _Q0, _CMP0, _SLC0, _WIN0, _G0, _IF0, _PEND = 0, 1024, 2048, 2560, 2816, 3072, 3328, 3456, 3584
_PROJ_SPLITS = ((_A0, _B0), (_B0, _Q0), (_Q0, _CMP0), (_CMP0, _SLC0), (_SLC0, _WIN0), (_WIN0, _G0), (_G0, _IF0),
                (_IF0, _PEND))


def _pad_lanes(a, width):
    return jnp.pad(a, [(0, 0)] * (a.ndim - 1) + [(0, width - a.shape[-1])])


def _prep_layer(P, l):
    w_in = P["w_in"][l]
    wa, wb, wc = w_in[:, 0:1024], w_in[:, 1024:2056], w_in[:, 2056:3360]
    w_in_p = jnp.concatenate([
        wa, wb[:, 0:768], wb[:, 776:1032], wc[:, 0:512], wc[:, 512:1280],
        _pad_lanes(wc[:, 1280:1304], LANES), _pad_lanes(wb[:, 768:776], LANES)], axis=1).astype(BF16)
    up = P["ffn_up"][l]
    ffn_up_p = jnp.concatenate([_pad_lanes(up[:, 0:D_FF], D_FF_PAD), _pad_lanes(up[:, D_FF:], D_FF_PAD)], axis=1).astype(BF16)
    cw = P["nsa_cmp_w"][l]
    cmpw_bd = jnp.zeros((4 * HEAD_DIM, 4 * HEAD_DIM), F32)
    for kv in range(2):
        for g in range(C_KV_HEADS):
            o = (kv * C_KV_HEADS + g) * HEAD_DIM
            cmpw_bd = cmpw_bd.at[o:o + HEAD_DIM, o:o + HEAD_DIM].set(cw[kv, g])
    pool = P["nsa_pool_w"][l].reshape(CMP_BLOCK, 4 * HEAD_DIM)
    return dict(
        w_in=w_in_p,
        w_out=P["w_out"][l].astype(BF16),
        ffn_up=ffn_up_p,
        ffn_down=jnp.pad(P["ffn_down"][l], ((0, D_FF_PAD - D_FF), (0, 0))).astype(BF16),
        ffn_cw=jnp.pad(P["ffn_conv_w"][l], ((0, SUBLANES - FFN_CONV), (0, D_FF_PAD - D_FF))),
        ffn_cb=_pad_lanes(P["ffn_conv_b"][l].reshape(1, D_FF), D_FF_PAD),
        ln1=(P["ln_g"][l, 0].reshape(1, D_MODEL), P["ln_b"][l, 0].reshape(1, D_MODEL)),
        ln2=(P["ln_g"][l, 1].reshape(1, D_MODEL), P["ln_b"][l, 1].reshape(1, D_MODEL)),
        rwkv=_prep_rwkv(P["rwkv_mu"][l], P["rwkv_w0"][l], P["rwkv_w2"][l], P["rwkv_a0"][l], P["rwkv_a2"][l],
                        P["rwkv_g2"][l], P["rwkv_k_k"][l], P["rwkv_k_a"][l], P["rwkv_r_k"][l], P["rwkv_ln_g"][l],
                        P["rwkv_ln_b"][l]),
        mlstm=_prep_mlstm(P["mlstm_conv_w"][l], P["mlstm_conv_b"][l], P["mlstm_i_b"][l], P["mlstm_f_b"][l],
                          P["mlstm_norm_g"][l]),
        pool0=pool[0:CMP_STRIDE], pool1=pool[CMP_STRIDE:CMP_BLOCK], cmpw=cmpw_bd,
        gate_b=_pad_lanes(P["nsa_gate_b"][l].reshape(1, 3 * C_HEADS), LANES),
    )


def _rows8(state):
    return jnp.pad(state, ((0, 0), (SUBLANES - state.shape[1], 0), (0, 0)))


def _last_rows(prev, cur, k):
    if cur.shape[1] >= k:
        return cur[:, cur.shape[1] - k:]
    return jnp.concatenate([prev, cur], axis=1)[:, -k:]


def _trunk(x, mod, st, layers, nsa_fn, dense_tile, rec_tile):
    B, T, _ = x.shape
    dbb, dL = dense_tile
    rbb, rL = rec_tile
    new = {k: [] for k in ("nsa_kv", "win_kv", "rwkv", "rwkv_shift", "mlstm_C", "mlstm_n", "mlstm_m", "mlstm_conv", "ffn_conv")}
    for l, Lw in enumerate(layers):
        m = mod[l]
        pa, pb, pq, pcmp, pslc, pwin, pg, pif, kvb = _modmm(
            x, m, 0, 1, Lw["w_in"], _PROJ_SPLITS + ((_SLC0, _G0),), (False,) * 8 + (True,), dbb, dL, "in_proj")
        ya, rw_st = _rwkv(pa, _rows8(st["rwkv_shift"][l][:, None, :]), _state_to_lanes(st["rwkv"][l]), *Lw["rwkv"], rbb, rL)
        yb, c_st, n_st, m_st = _mlstm(pb, _rows8(st["mlstm_conv"][l]), pif, _state_to_lanes(st["mlstm_C"][l]),
                                      st["mlstm_n"][l], st["mlstm_m"][l], *Lw["mlstm"], rbb, rL)
        yc, win_new = nsa_fn(l, Lw, pq, pg, pcmp, pslc, pwin, kvb)
        x = _outproj(ya, yb, yc, x, m, Lw["w_out"], *Lw["ln1"], dbb, dL)
        (u,) = _modmm(x, m, 3, 4, Lw["ffn_up"], ((0, 2 * D_FF_PAD),), (False,), dbb, dL, "ffn_up")
        st8 = _rows8(_pad_lanes(st["ffn_conv"][l], D_FF_PAD))
        x = _ffn_down(u, st8, x, m, Lw["ffn_cw"], Lw["ffn_cb"], Lw["ffn_down"], *Lw["ln2"], dbb, dL)
        new["nsa_kv"].append(jnp.concatenate([pcmp, pslc], axis=-1).reshape(B, T, 4, C_KV_HEADS, HEAD_DIM))
        new["win_kv"].append(win_new)
        new["rwkv"].append(_state_from_lanes(rw_st, B))
        new["rwkv_shift"].append(pa[:, -1])
        new["mlstm_C"].append(_state_from_lanes(c_st, B))
        new["mlstm_n"].append(n_st)
        new["mlstm_m"].append(m_st)
        new["mlstm_conv"].append(_last_rows(st["mlstm_conv"][l], pb[:, :, 0:2 * B_WIDTH], MLSTM_CONV - 1))
        new["ffn_conv"].append(_last_rows(st["ffn_conv"][l], u[:, :, 0:D_FF], FFN_CONV - 1))
    return x, {k: jnp.stack(v) for k, v in new.items()}


def kernel(x_prompt, x_sample, c_prompt, c_sample, cache_nsa_kv, cache_win_kv, state_rwkv, state_rwkv_shift,
           state_mlstm_C, state_mlstm_n, state_mlstm_m, state_mlstm_conv, state_ffn_conv, page_table,
           w_in, w_out, ada_w, ada_b, ln_g, ln_b, rwkv_mu, rwkv_w0, rwkv_w2, rwkv_a0, rwkv_a2, rwkv_g2,
           rwkv_k_k, rwkv_k_a, rwkv_r_k, rwkv_ln_g, rwkv_ln_b, mlstm_conv_w, mlstm_conv_b, mlstm_i_b,
           mlstm_f_b, mlstm_norm_g, nsa_pool_w, nsa_cmp_w, nsa_gate_b, ffn_up, ffn_conv_w, ffn_conv_b, ffn_down):
    P = dict(w_in=w_in, w_out=w_out, ln_g=ln_g, ln_b=ln_b, rwkv_mu=rwkv_mu, rwkv_w0=rwkv_w0, rwkv_w2=rwkv_w2,
             rwkv_a0=rwkv_a0, rwkv_a2=rwkv_a2, rwkv_g2=rwkv_g2, rwkv_k_k=rwkv_k_k, rwkv_k_a=rwkv_k_a,
             rwkv_r_k=rwkv_r_k, rwkv_ln_g=rwkv_ln_g, rwkv_ln_b=rwkv_ln_b, mlstm_conv_w=mlstm_conv_w,
             mlstm_conv_b=mlstm_conv_b, mlstm_i_b=mlstm_i_b, mlstm_f_b=mlstm_f_b, mlstm_norm_g=mlstm_norm_g,
             nsa_pool_w=nsa_pool_w, nsa_cmp_w=nsa_cmp_w, nsa_gate_b=nsa_gate_b, ffn_up=ffn_up,
             ffn_conv_w=ffn_conv_w, ffn_conv_b=ffn_conv_b, ffn_down=ffn_down)
    Bp, Tp, _ = x_prompt.shape
    Bs, Ts, _ = x_sample.shape
    G, dh = C_KV_HEADS, HEAD_DIM
    layers = [_prep_layer(P, l) for l in range(DEPTH)]

    nb = -(-(Bp + Bs) // SUBLANES) * SUBLANES
    c_all = jnp.pad(jnp.concatenate([c_prompt, c_sample], axis=0), ((0, nb - Bp - Bs), (0, 0)))
    mod = _ada(c_all, ada_w, ada_b)
    mod_p = mod[:, 0:Bp].reshape(DEPTH, Bp, 1, 6 * D_MODEL)
    mod_s = mod[:, Bp:Bp + Bs].reshape(DEPTH, Bs, 1, 6 * D_MODEL)

    st_p = dict(
        rwkv=jnp.zeros((DEPTH, Bp, A_HEADS, dh, dh), F32), rwkv_shift=jnp.zeros((DEPTH, Bp, 4 * A_WIDTH), F32),
        mlstm_C=jnp.zeros((DEPTH, Bp, B_HEADS, dh, dh), F32), mlstm_n=jnp.zeros((DEPTH, Bp, B_HEADS, dh), F32),
        mlstm_m=jnp.zeros((DEPTH, Bp, B_HEADS), F32), mlstm_conv=jnp.zeros((DEPTH, Bp, MLSTM_CONV - 1, 2 * B_WIDTH), F32),
        ffn_conv=jnp.zeros((DEPTH, Bp, FFN_CONV - 1, D_FF), F32))
    m_p, e_p = _nsa_consts(Tp, Tp)
    e_p = jnp.asarray(e_p, BF16)
    tq = 128
    rs = 4 * tq

    def nsa_prompt(l, Lw, pq, pg, pcmp, pslc, pwin, kvb):
        kc = _compress(pcmp, jnp.tile(Lw["pool0"], (rs // CMP_STRIDE, 1)), jnp.tile(Lw["pool1"], (rs // CMP_STRIDE, 1)), Lw["cmpw"])
        yc = _nsa_prompt(pq, pg, Lw["gate_b"], kc, kvb, jnp.asarray(m_p), e_p, tq)
        return yc, pwin[:, -min(WINDOW, Tp):].reshape(Bp, min(WINDOW, Tp), 2, G, dh)

    y_prompt, new_p = _trunk(x_prompt, mod_p, st_p, layers, nsa_prompt, (1, 256), (Bp, ROWS // Bp))

    st_s = dict(rwkv=state_rwkv, rwkv_shift=state_rwkv_shift, mlstm_C=state_mlstm_C, mlstm_n=state_mlstm_n,
                mlstm_m=state_mlstm_m, mlstm_conv=state_mlstm_conv, ffn_conv=state_ffn_conv)
    n_pages = page_table.shape[1]
    past_len = n_pages * PAGE_SIZE
    lp = -(-(past_len + Ts) // SEL_BLOCK) * SEL_BLOCK
    ngrp = 2 * LANES
    assert (n_pages + 1) * (PAGE_SIZE // CMP_STRIDE) <= ngrp
    m_s, e_s = _nsa_consts(lp, past_len + PAGE_SIZE)
    m_s = jnp.asarray(np.pad(m_s, ((0, ngrp - m_s.shape[0]), (0, 0))))
    e_s = jnp.asarray(e_s, BF16)
    cache_t = jnp.transpose(cache_nsa_kv, (0, 1, 3, 4, 5, 2)).reshape(DEPTH, cache_nsa_kv.shape[1], 4 * G * dh, PAGE_SIZE)
    nwb = cache_win_kv.shape[2]
    win_t_all = jnp.transpose(cache_win_kv, (0, 1, 3, 4, 5, 2)).reshape(DEPTH, Bs, 2 * G * dh, nwb)
    reps = PAGE_SIZE // CMP_STRIDE

    def nsa_sample(l, Lw, pq, pg, pcmp, pslc, pwin, kvb):
        yc = _nsa_sample(l, page_table, cache_t, pq, pg, Lw["gate_b"], pcmp, pslc, pwin, win_t_all,
                         jnp.tile(Lw["pool0"].T, (1, reps)), jnp.tile(Lw["pool1"].T, (1, reps)),
                         Lw["cmpw"].T, m_s, e_s)
        win_all = jnp.concatenate([cache_win_kv[l], pwin.reshape(Bs, Ts, 2, G, dh)], axis=1)
        keep = min(WINDOW, nwb + Ts)
        return yc, win_all[:, -keep:]

    y_sample, new_s = _trunk(x_sample, mod_s, st_s, layers, nsa_sample, (ROWS // Ts, Ts), (ROWS // Ts, Ts))

    return (y_prompt, y_sample,
            new_p["nsa_kv"], new_s["nsa_kv"], new_p["win_kv"], new_s["win_kv"],
            new_p["rwkv"], new_s["rwkv"], new_p["rwkv_shift"], new_s["rwkv_shift"],
            new_p["mlstm_C"], new_s["mlstm_C"], new_p["mlstm_n"], new_s["mlstm_n"],
            new_p["mlstm_m"], new_s["mlstm_m"], new_p["mlstm_conv"], new_s["mlstm_conv"],
            new_p["ffn_conv"], new_s["ffn_conv"])
```

```python
import functools
import math

import numpy as np
import jax
import jax.numpy as jnp
from jax import lax
from jax.experimental import pallas as pl
from jax.experimental.pallas import tpu as pltpu

F32 = jnp.float32
BF16 = jnp.bfloat16

D_MODEL = 1024
DEPTH = 4
HEAD_DIM = 64
A_WIDTH = 256
B_WIDTH = 256
C_WIDTH = 512
A_HEADS = 4
B_HEADS = 4
C_HEADS = 8
C_KV_HEADS = 2
C_GROUP = 4
PAGE_SIZE = 128
MLSTM_CONV = 4
CMP_BLOCK = 32
CMP_STRIDE = 16
SEL_BLOCK = 64
SEL_TOP = 16
WINDOW = 512
D_FF = 2752
D_FF_PAD = 2816
FFN_CONV = 3
ALPHA = (2 * DEPTH) ** 0.25
FORCE_BONUS = 1e4
NEG_INF = -1e30
LANES = 128
SUBLANES = 8
ROWS = 256
VMEM_LIMIT = 56 * 1024 * 1024

NN = (((1,), (0,)), ((), ()))
NT = (((1,), (1,)), ((), ()))


def _dot(a, b, dn=NN):
    return lax.dot_general(a, b, dn, preferred_element_type=F32)


def _split2(a):
    hi = a.astype(BF16)
    lo = (a - hi.astype(F32)).astype(BF16)
    return hi, lo


def _mm(a, b, passes=1, dn=NN):
    if passes == 1:
        return _dot(a.astype(BF16), b.astype(BF16), dn)
    ah, al = _split2(a)
    bh, bl = _split2(b)
    return _dot(ah, bh, dn) + (_dot(al, bh, dn) + _dot(ah, bl, dn))


def _mm_sel(sel, x, dn=NN):
    s = sel.astype(BF16)
    x1 = x.astype(BF16)
    r1 = x - x1.astype(F32)
    x2 = r1.astype(BF16)
    x3 = (r1 - x2.astype(F32)).astype(BF16)
    return _dot(s, x1, dn) + (_dot(s, x2, dn) + _dot(s, x3, dn))


def _mm_xsel(x, sel, dn=NN):
    s = sel.astype(BF16)
    x1 = x.astype(BF16)
    r1 = x - x1.astype(F32)
    x2 = r1.astype(BF16)
    x3 = (r1 - x2.astype(F32)).astype(BF16)
    return _dot(x1, s, dn) + (_dot(x2, s, dn) + _dot(x3, s, dn))


def _sigmoid(x):
    return 1.0 / (1.0 + jnp.exp(-x))


def _silu(x):
    return x * _sigmoid(x)


def _softplus(x):
    return jnp.maximum(x, 0.0) + jnp.log(1.0 + jnp.exp(-jnp.abs(x)))


def _log_sigmoid(x):
    return -_softplus(-x)


def _iota(shape, axis):
    return lax.broadcasted_iota(jnp.int32, shape, axis)


def _block_masks(rows, chunk):
    sh = int(math.log2(chunk))
    r = _iota((rows, rows), 0)
    s = _iota((rows, rows), 1)
    same = jnp.right_shift(r, sh) == jnp.right_shift(s, sh)
    return same, same & (s <= r), same & (s < r)


def _head_ones(width):
    r = _iota((width, width), 0)
    s = _iota((width, width), 1)
    return (jnp.right_shift(r, 6) == jnp.right_shift(s, 6)).astype(F32)


def _expand_mat(rows, chunk, nseq):
    sh = int(math.log2(chunk))
    r = _iota((rows, nseq * HEAD_DIM), 0)
    c = _iota((rows, nseq * HEAD_DIM), 1)
    return ((jnp.right_shift(c, 6) == jnp.right_shift(r, sh)) & ((r & (chunk - 1)) == 0)).astype(F32)


def _seq_lane_mask(rows, chunk, nseq):
    sh = int(math.log2(chunk))
    r = _iota((rows, nseq * HEAD_DIM), 0)
    c = _iota((rows, nseq * HEAD_DIM), 1)
    return jnp.right_shift(c, 6) == jnp.right_shift(r, sh)


def _fold_mat(nseq):
    r = _iota((nseq * HEAD_DIM, HEAD_DIM), 0)
    c = _iota((nseq * HEAD_DIM, HEAD_DIM), 1)
    return ((r & (HEAD_DIM - 1)) == c).astype(F32)


def _shifted_rows(pre, cur, nshift):
    bb, L, C = cur.shape
    full = jnp.concatenate([pre, cur], axis=1).reshape(bb * (L + SUBLANES), C)
    out = []
    for k in range(1, nshift + 1):
        sh = pltpu.roll(full, k, axis=0).reshape(bb, L + SUBLANES, C)
        out.append(sh[:, SUBLANES:, :])
    return out


def _layernorm(z, g, b):
    mu = jnp.mean(z, axis=-1, keepdims=True)
    zc = z - mu
    var = jnp.mean(zc * zc, axis=-1, keepdims=True)
    return zc * lax.rsqrt(var + 1e-5) * g + b


def _cparams(sem):
    return pltpu.CompilerParams(dimension_semantics=sem, vmem_limit_bytes=VMEM_LIMIT)


def _ada_kernel(c_ref, w_ref, b_ref, o_ref):
    c = c_ref[...]
    o_ref[...] = _mm(_silu(c), w_ref[...], 3) + b_ref[...]


def _ada(c_all, ada_w, ada_b):
    nb = c_all.shape[0]
    tn = 1536
    return pl.pallas_call(
        _ada_kernel,
        grid=(DEPTH, 6 * D_MODEL // tn),
        in_specs=[
            pl.BlockSpec((nb, D_MODEL), lambda l, n: (0, 0)),
            pl.BlockSpec((None, D_MODEL, tn), lambda l, n: (l, 0, n)),
            pl.BlockSpec((None, 1, tn), lambda l, n: (l, 0, n)),
        ],
        out_specs=pl.BlockSpec((None, nb, tn), lambda l, n: (l, 0, n)),
        out_shape=jax.ShapeDtypeStruct((DEPTH, nb, 6 * D_MODEL), F32),
        compiler_params=_cparams(("arbitrary", "arbitrary")),
        name="ada_mod",
    )(c_all, ada_w, ada_b.reshape(DEPTH, 1, 6 * D_MODEL))


def _modmm_kernel(x_ref, sh_ref, sc_ref, w_ref, *o_refs, splits, bf16_outs):
    x = x_ref[...]
    bb, L, D = x.shape
    h = (x * (1.0 + sc_ref[...]) + sh_ref[...]).reshape(bb * L, D).astype(BF16)
    o = jnp.dot(h, w_ref[...], preferred_element_type=F32)
    for (a, b), o_ref, as_bf16 in zip(splits, o_refs, bf16_outs):
        piece = o[:, a:b].reshape(bb, L, b - a)
        o_ref[...] = piece.astype(BF16) if as_bf16 else piece


def _modmm(x, mod, sh_col, sc_col, w, splits, bf16_outs, bb, L, name):
    B, T, D = x.shape
    N = w.shape[1]
    kern = functools.partial(_modmm_kernel, splits=splits, bf16_outs=bf16_outs)
    return pl.pallas_call(
        kern,
        grid=(B // bb, T // L),
        in_specs=[
            pl.BlockSpec((bb, L, D), lambda i, j: (i, j, 0)),
            pl.BlockSpec((bb, 1, D), lambda i, j: (i, 0, sh_col)),
            pl.BlockSpec((bb, 1, D), lambda i, j: (i, 0, sc_col)),
            pl.BlockSpec((D, N), lambda i, j: (0, 0)),
        ],
        out_specs=[pl.BlockSpec((bb, L, b - a), lambda i, j: (i, j, 0)) for a, b in splits],
        out_shape=[jax.ShapeDtypeStruct((B, T, b - a), BF16 if q else F32) for (a, b), q in zip(splits, bf16_outs)],
        compiler_params=_cparams(("arbitrary", "arbitrary")),
        name=name,
    )(x, mod, mod, w)


def _outproj_kernel(ya_ref, yb_ref, yc_ref, x_ref, g_ref, w_ref, lg_ref, lb_ref, o_ref):
    x = x_ref[...]
    bb, L, D = x.shape
    rows = bb * L
    ya = ya_ref[...].reshape(rows, A_WIDTH).astype(BF16)
    yb = yb_ref[...].reshape(rows, B_WIDTH).astype(BF16)
    yc = yc_ref[...].reshape(rows, C_WIDTH).astype(BF16)
    y = (jnp.dot(ya, w_ref[0:A_WIDTH, :], preferred_element_type=F32)
         + jnp.dot(yb, w_ref[A_WIDTH:A_WIDTH + B_WIDTH, :], preferred_element_type=F32)
         + jnp.dot(yc, w_ref[A_WIDTH + B_WIDTH:, :], preferred_element_type=F32))
    z = ALPHA * x + (1.0 + g_ref[...]) * y.reshape(bb, L, D)
    o_ref[...] = _layernorm(z, lg_ref[...], lb_ref[...])


def _outproj(ya, yb, yc, x, mod, w_out, ln_g, ln_b, bb, L):
    B, T, D = x.shape
    blk = lambda w: pl.BlockSpec((bb, L, w), lambda i, j: (i, j, 0))
    return pl.pallas_call(
        _outproj_kernel,
        grid=(B // bb, T // L),
        in_specs=[
            blk(A_WIDTH), blk(B_WIDTH), blk(C_WIDTH), blk(D),
            pl.BlockSpec((bb, 1, D), lambda i, j: (i, 0, 2)),
            pl.BlockSpec((D, D), lambda i, j: (0, 0)),
            pl.BlockSpec((1, D), lambda i, j: (0, 0)),
            pl.BlockSpec((1, D), lambda i, j: (0, 0)),
        ],
        out_specs=blk(D),
        out_shape=jax.ShapeDtypeStruct((B, T, D), F32),
        compiler_params=_cparams(("arbitrary", "arbitrary")),
        name="outproj_ln",
    )(ya, yb, yc, x, mod, w_out, ln_g, ln_b)


def _ffn_down_kernel(ug_ref, uv_ref, halo_ref, st_ref, x_ref, g_ref, cw_ref, cb_ref, w_ref, lg_ref, lb_ref, o_ref):
    ug = ug_ref[...]
    bb, L, N = ug.shape
    first = pl.program_id(1) == 0
    pre = jnp.where(first, st_ref[...], halo_ref[...])
    u1, u2 = _shifted_rows(pre, ug, FFN_CONV - 1)
    cw = cw_ref[...]
    conv = cb_ref[...] + ug * cw[2:3, :] + u1 * cw[1:2, :] + u2 * cw[0:1, :]
    a = (_silu(conv) * uv_ref[...]).reshape(bb * L, N).astype(BF16)
    y = jnp.dot(a, w_ref[...], preferred_element_type=F32)
    x = x_ref[...]
    z = ALPHA * x + (1.0 + g_ref[...]) * y.reshape(x.shape)
    o_ref[...] = _layernorm(z, lg_ref[...], lb_ref[...])


def _ffn_down(u, st8, x, mod, conv_w, conv_b, w_down, ln_g, ln_b, bb, L):
    B, T, D = x.shape
    N = D_FF_PAD
    lb8 = L // SUBLANES
    return pl.pallas_call(
        _ffn_down_kernel,
        grid=(B // bb, T // L),
        in_specs=[
            pl.BlockSpec((bb, L, N), lambda i, j: (i, j, 0)),
            pl.BlockSpec((bb, L, N), lambda i, j: (i, j, 1)),
            pl.BlockSpec((bb, SUBLANES, N), lambda i, j: (i, jnp.maximum(j * lb8 - 1, 0), 0)),
            pl.BlockSpec((bb, SUBLANES, N), lambda i, j: (i, 0, 0)),
            pl.BlockSpec((bb, L, D), lambda i, j: (i, j, 0)),
            pl.BlockSpec((bb, 1, D), lambda i, j: (i, 0, 5)),
            pl.BlockSpec((SUBLANES, N), lambda i, j: (0, 0)),
            pl.BlockSpec((1, N), lambda i, j: (0, 0)),
            pl.BlockSpec((N, D), lambda i, j: (0, 0)),
            pl.BlockSpec((1, D), lambda i, j: (0, 0)),
            pl.BlockSpec((1, D), lambda i, j: (0, 0)),
        ],
        out_specs=pl.BlockSpec((bb, L, D), lambda i, j: (i, j, 0)),
        out_shape=jax.ShapeDtypeStruct((B, T, D), F32),
        compiler_params=_cparams(("arbitrary", "arbitrary")),
        name="ffn_down_ln",
    )(u, u, u, st8, x, mod, conv_w, conv_b, w_down, ln_g, ln_b)


def _unit_lower_inverse(n_mat, chunk):
    rows = n_mat.shape[0]
    eye = (_iota((rows, rows), 0) == _iota((rows, rows), 1)).astype(F32)
    p = n_mat.astype(BF16)
    t = eye + n_mat
    for _ in range(int(math.log2(chunk)) - 1):
        p32 = _dot(p, p)
        p = p32.astype(BF16)
        t = t + _dot(p, t.astype(BF16))
    resid = (eye - t) + _mm(n_mat, t, 3)
    return t + _dot(t.astype(BF16), resid.astype(BF16))


def _rwkv_kernel(pa_ref, halo_ref, sh_ref, st0_ref, mu_ref, pv_ref, lw_ref, ya_ref, sto_ref, st_ref, *, chunk):
    c = pl.program_id(1)
    bb, L, _ = pa_ref.shape
    R = bb * L
    W = A_WIDTH

    @pl.when(c == 0)
    def _():
        st_ref[...] = st0_ref[...]

    pa = pa_ref[...]
    pre = jnp.where(c == 0, sh_ref[...], halo_ref[...])
    (prev,) = _shifted_rows(pre, pa, 1)
    x = pa.reshape(R, 4 * W)
    pm = x + (prev.reshape(R, 4 * W) - x) * mu_ref[...]
    r = pm[:, 0:W]
    k = pm[:, W:2 * W]
    v = pm[:, 2 * W:3 * W]
    lo = pm[:, 3 * W:4 * W]
    lane = _iota((R, W), 1)
    z = jnp.where(lane < 64, jnp.tanh(lo), jnp.where(lane < 128, lo, _sigmoid(lo)))
    lora = _mm(z, lw_ref[...], 3)
    pv = pv_ref[...]
    w0, a0, k_k, k_a, r_k, ln_g, ln_b = (pv[i:i + 1, :] for i in range(7))
    w = -_softplus(-(w0 + lora[:, 0:W])) - 0.5
    a = _sigmoid(a0 + lora[:, W:2 * W])
    g = lora[:, 2 * W:3 * W]
    ones_h = _head_ones(W)
    kk = k * k_k
    kk = kk / jnp.maximum(jnp.sqrt(_mm_xsel(kk * kk, ones_h)), 1e-12)
    k2 = k * (1.0 + (a - 1.0) * k_a)
    lw = -jnp.exp(w)
    same, incl, strict = _block_masks(R, L)
    cum = _mm_sel(incl.astype(F32), lw)
    tot = _mm_sel(same.astype(F32), lw)
    e_neg = jnp.exp(-cum)
    e_rem = jnp.exp(tot - cum)
    kb = kk * a
    a_t = -kk * jnp.exp(cum - lw)
    b_t = kb * e_neg
    k_t = k2 * e_neg
    r_t = r * jnp.exp(cum)
    bh_t = jnp.transpose(kb * e_rem)
    kh_t = jnp.transpose(k2 * e_rem)
    gam_t = jnp.transpose(jnp.exp(tot))
    per_seq = bb <= 4
    if not per_seq:
        expand = _expand_mat(R, L, bb)
        lmask = _seq_lane_mask(R, L, bb)
        lmask2 = jnp.concatenate([lmask, lmask], axis=0)
        fold = _fold_mat(bb)
        tile = jnp.transpose(fold)
    outs = []
    for h in range(A_HEADS):
        hs = slice(h * HEAD_DIM, (h + 1) * HEAD_DIM)
        st_h = st_ref[h]
        A, Bt, Kt, Rt, V = a_t[:, hs], b_t[:, hs], k_t[:, hs], r_t[:, hs], v[:, hs]
        m_ab = jnp.where(strict, _mm(A, Bt, 3, NT), 0.0)
        m_ak = jnp.where(strict, _mm(A, Kt, 1, NT), 0.0)
        g_r = _mm(Rt, jnp.concatenate([Bt, Kt], axis=0), 1, NT)
        m_rb = jnp.where(incl, g_r[:, 0:R], 0.0)
        m_rk = jnp.where(incl, g_r[:, R:2 * R], 0.0)
        if per_seq:
            ps = [_mm(jnp.concatenate([A[b * L:(b + 1) * L], Rt[b * L:(b + 1) * L]], axis=0),
                      st_h[:, b * HEAD_DIM:(b + 1) * HEAD_DIM], 1) for b in range(bb)]
            ps_a = jnp.concatenate([p[0:L] for p in ps], axis=0)
            ps_r = jnp.concatenate([p[L:2 * L] for p in ps], axis=0)
        else:
            X = jnp.concatenate([A, Rt], axis=0)
            PS = _mm_xsel(jnp.where(lmask2, _mm(X, st_h, 1), 0.0), fold)
            ps_a, ps_r = PS[0:R], PS[R:2 * R]
        Y = ps_a + _mm(m_ak, V, 1)
        U = _mm(_unit_lower_inverse(m_ab, L), Y, 3)
        UV = jnp.concatenate([U, V], axis=0)
        outs.append(ps_r + _mm(jnp.concatenate([m_rb, m_rk], axis=1), UV, 1))
        if per_seq:
            new = []
            for b in range(bb):
                ts = slice(b * L, (b + 1) * L)
                lhs = jnp.concatenate([bh_t[hs, ts], kh_t[hs, ts]], axis=1)
                uv_b = jnp.concatenate([U[ts], V[ts]], axis=0)
                new.append(gam_t[hs, b * L:b * L + 1] * st_h[:, b * HEAD_DIM:(b + 1) * HEAD_DIM] + _mm(lhs, uv_b, 1))
            st_ref[h] = jnp.concatenate(new, axis=1)
        else:
            UVb = jnp.where(lmask2, _mm_xsel(UV, tile), 0.0)
            lhs = jnp.concatenate([bh_t[hs, :], kh_t[hs, :]], axis=1)
            gam = _mm_xsel(gam_t[hs, :], expand)
            st_ref[h] = gam * st_h + _mm(lhs, UVb, 1)
    o = jnp.concatenate(outs, axis=1)
    inv = 1.0 / HEAD_DIM
    mu = _mm_xsel(o, ones_h) * inv
    oc = o - mu
    var = _mm_xsel(oc * oc, ones_h) * inv
    y = oc * lax.rsqrt(var + 64e-5) * ln_g + ln_b
    y = y + _mm_xsel(r * k2 * r_k, ones_h) * v
    ya_ref[...] = (y * g).reshape(bb, L, W)

    @pl.when(c == pl.num_programs(1) - 1)
    def _():
        sto_ref[...] = st_ref[...]


def _rwkv(pa, shift8, st0, mu, pvec, lora_w, bb, L):
    B, T, _ = pa.shape
    lb8 = L // SUBLANES
    kern = functools.partial(_rwkv_kernel, chunk=L)
    return pl.pallas_call(
        kern,
        grid=(B // bb, T // L),
        in_specs=[
            pl.BlockSpec((bb, L, 4 * A_WIDTH), lambda i, j: (i, j, 0)),
            pl.BlockSpec((bb, SUBLANES, 4 * A_WIDTH), lambda i, j: (i, jnp.maximum(j * lb8 - 1, 0), 0)),
            pl.BlockSpec((bb, SUBLANES, 4 * A_WIDTH), lambda i, j: (i, 0, 0)),
            pl.BlockSpec((A_HEADS, HEAD_DIM, bb * HEAD_DIM), lambda i, j: (0, 0, i)),
            pl.BlockSpec((1, 4 * A_WIDTH), lambda i, j: (0, 0)),
            pl.BlockSpec((SUBLANES, A_WIDTH), lambda i, j: (0, 0)),
            pl.BlockSpec((A_WIDTH, 3 * A_WIDTH), lambda i, j: (0, 0)),
        ],
        out_specs=[
            pl.BlockSpec((bb, L, A_WIDTH), lambda i, j: (i, j, 0)),
            pl.BlockSpec((A_HEADS, HEAD_DIM, bb * HEAD_DIM), lambda i, j: (0, 0, i)),
        ],
        out_shape=[
            jax.ShapeDtypeStruct((B, T, A_WIDTH), F32),
            jax.ShapeDtypeStruct((A_HEADS, HEAD_DIM, B * HEAD_DIM), F32),
        ],
        scratch_shapes=[pltpu.VMEM((A_HEADS, HEAD_DIM, bb * HEAD_DIM), F32)],
        compiler_params=_cparams(("arbitrary", "arbitrary")),
        name="rwkv7",
    )(pa, pa, shift8, st0, mu, pvec, lora_w)


def _prep_rwkv(mu, w0, w2, a0, a2, g2, k_k, k_a, r_k, ln_g, ln_b):
    zero = jnp.zeros((A_WIDTH,), F32)
    pvec = jnp.stack([w0, a0, k_k, k_a, r_k.reshape(A_WIDTH), ln_g, ln_b, zero])
    lora = jnp.zeros((A_WIDTH, 3 * A_WIDTH), F32)
    lora = lora.at[0:64, 0:A_WIDTH].set(w2)
    lora = lora.at[64:128, A_WIDTH:2 * A_WIDTH].set(a2)
    lora = lora.at[128:256, 2 * A_WIDTH:].set(g2)
    return mu.reshape(1, 4 * A_WIDTH), pvec, lora


def _state_to_lanes(s):
    B, H = s.shape[:2]
    return jnp.transpose(s, (1, 3, 0, 2)).reshape(H, HEAD_DIM, B * HEAD_DIM)


def _state_from_lanes(st, B):
    H = st.shape[0]
    return jnp.transpose(st.reshape(H, HEAD_DIM, B, HEAD_DIM), (2, 0, 3, 1))


def _mlstm_kernel(pb_ref, halo_ref, cv_ref, pif_ref, gt_ref, ct0_ref, n0_ref, m0_ref, cw_ref, cb_ref, bif_ref,
                  brow_ref, ng_ref, yb_ref, cto_ref, no_ref, mo_ref, ct_ref, nt_ref, m_ref):
    c = pl.program_id(1)
    bb, L, _ = pb_ref.shape
    R = bb * L
    W = B_WIDTH
    neg = -jnp.inf

    @pl.when(c == 0)
    def _():
        ct_ref[...] = ct0_ref[...]
        nt_ref[...] = n0_ref[...]
        m_ref[...] = jnp.broadcast_to(m0_ref[...], m_ref.shape)

    pb = pb_ref[...]
    qk_in = pb[:, :, 0:2 * W]
    pre = jnp.where(c == 0, cv_ref[...], halo_ref[...])
    s1, s2, s3 = _shifted_rows(pre, qk_in, MLSTM_CONV - 1)
    cw = cw_ref[...]
    conv = cb_ref[...] + qk_in * cw[3:4, :] + s1 * cw[2:3, :] + s2 * cw[1:2, :] + s3 * cw[0:1, :]
    qk = _silu(conv).reshape(R, 2 * W)
    q = qk[:, 0:W]
    k = qk[:, W:2 * W] * (HEAD_DIM ** -0.5)
    v = pb[:, :, 2 * W:3 * W].reshape(R, W)
    og = pb[:, :, 3 * W:4 * W].reshape(R, W)

    same, incl, _ = _block_masks(R, L)
    same_f = same.astype(F32)
    incl_f = incl.astype(F32)
    gc = pif_ref[...].reshape(R, LANES) + bif_ref[...]
    lane = _iota((R, LANES), 1)
    lfc = jnp.where((lane >= B_HEADS) & (lane < 2 * B_HEADS), _log_sigmoid(gc), 0.0)
    bcum_c = _mm_sel(incl_f, lfc)
    btot_c = _mm_sel(same_f, lfc)
    gr = gt_ref[...] + brow_ref[...]
    row = _iota((SUBLANES, R), 0)
    lfr = jnp.where(row >= B_HEADS, _log_sigmoid(gr), 0.0)
    bcum_r = _mm_xsel(lfr, incl_f, NT)
    btot_r = _mm_xsel(lfr, same_f)
    m_col = jnp.broadcast_to(m_ref[:, 0:1, :], (bb, L, LANES)).reshape(R, LANES)

    lmask = _seq_lane_mask(R, L, bb)
    fold = _fold_mat(bb)
    tile = jnp.transpose(fold)
    expand = _expand_mat(R, L, bb)
    sh = int(math.log2(L))
    rl = _iota((R, LANES), 0)
    blockind = (jnp.right_shift(rl, sh) == lane).astype(F32)
    firstind = ((jnp.right_shift(rl, sh) == lane) & ((rl & (L - 1)) == 0)).astype(F32)

    houts, kws, wcs = [], [], []
    m_new_all = jnp.zeros((R, LANES), F32)
    for h in range(B_HEADS):
        hs = slice(h * HEAD_DIM, (h + 1) * HEAD_DIM)
        Q, K, V = q[:, hs], k[:, hs], v[:, hs]
        b_c = bcum_c[:, B_HEADS + h:B_HEADS + h + 1]
        b_r = bcum_r[B_HEADS + h:B_HEADS + h + 1, :]
        i_r = gr[h:h + 1, :]
        i_c = gc[:, h:h + 1]
        m_c = m_col[:, h:h + 1]
        D = jnp.where(incl, b_c - b_r + i_r, neg)
        inter = b_c + m_c
        mt = jnp.maximum(inter, jnp.max(D, axis=1, keepdims=True))
        S = _mm(Q, K, 1, NT) * jnp.exp(D - mt)
        iw = jnp.exp(inter - mt)
        QC = _mm_xsel(jnp.where(lmask, _mm(Q, ct_ref[h], 3), 0.0), fold)
        num = _mm(S, V, 1) + iw * QC
        qn = jnp.sum(_mm(Q, nt_ref[h], 3) * blockind, axis=1, keepdims=True)
        den = jnp.sum(S, axis=1, keepdims=True) + iw * qn
        houts.append(num / jnp.maximum(jnp.abs(den), jnp.exp(-mt)))
        bl_c = btot_c[:, B_HEADS + h:B_HEADS + h + 1]
        bl_r = btot_r[B_HEADS + h:B_HEADS + h + 1, :]
        gs_c = bl_c - b_c + i_c
        gmax = jnp.max(jnp.where(same, bl_r - b_r + i_r, neg), axis=1, keepdims=True)
        m_new = jnp.maximum(bl_c + m_c, gmax)
        kws.append(K * jnp.exp(gs_c - m_new))
        wcs.append(jnp.exp(bl_c + m_c - m_new))
        m_new_all = jnp.where(lane == h, m_new, m_new_all)

    kw_t = jnp.transpose(jnp.concatenate(kws, axis=1))
    for h in range(B_HEADS):
        hs = slice(h * HEAD_DIM, (h + 1) * HEAD_DIM)
        vb = jnp.where(lmask, _mm_xsel(v[:, hs], tile), 0.0)
        wc_row = jnp.sum(wcs[h] * expand, axis=0, keepdims=True)
        ct_ref[h] = wc_row * ct_ref[h] + _mm(kw_t[hs, :], vb, 3)
        wc_lane = jnp.sum(wcs[h] * firstind, axis=0, keepdims=True)
        nt_ref[h] = wc_lane * nt_ref[h] + _mm_xsel(kw_t[hs, :], blockind)
    m_ref[...] = m_new_all.reshape(bb, L, LANES)[:, 0:SUBLANES, :]

    hcat = jnp.concatenate(houts, axis=1)
    ones_h = _head_ones(W)
    inv = 1.0 / HEAD_DIM
    mu = _mm_xsel(hcat, ones_h) * inv
    hc = hcat - mu
    var = _mm_xsel(hc * hc, ones_h) * inv
    hn = hc * lax.rsqrt(var + 1e-5) * ng_ref[...]
    yb_ref[...] = (_sigmoid(og) * hn).reshape(bb, L, W)

    @pl.when(c == pl.num_programs(1) - 1)
    def _():
        cto_ref[...] = ct_ref[...]
        no_ref[...] = nt_ref[...]
        mo_ref[...] = m_ref[...]


def _mlstm(pb, conv8, pif, ct0, n0, m0, conv_w8, conv_b, bias_if, norm_g, bb, L):
    B, T, _ = pb.shape
    R = bb * L
    nbi, nch = B // bb, T // L
    lb8 = L // SUBLANES
    g_t = pif[:, :, 0:SUBLANES].reshape(nbi, bb, nch, L, SUBLANES).transpose(0, 2, 4, 1, 3).reshape(nbi, nch, SUBLANES, R)
    bias_row = jnp.broadcast_to(bias_if[0, 0:SUBLANES].reshape(SUBLANES, 1), (SUBLANES, R))
    n_in = jnp.pad(n0.reshape(nbi, bb, B_HEADS, HEAD_DIM).transpose(0, 2, 3, 1), ((0, 0), (0, 0), (0, 0), (0, LANES - bb)))
    m_in = jnp.pad(m0, ((0, 0), (0, LANES - B_HEADS))).reshape(B, 1, LANES)
    yb, ct, nt, mo = pl.pallas_call(
        _mlstm_kernel,
        grid=(nbi, nch),
        in_specs=[
            pl.BlockSpec((bb, L, 4 * B_WIDTH), lambda i, j: (i, j, 0)),
            pl.BlockSpec((bb, SUBLANES, 2 * B_WIDTH), lambda i, j: (i, jnp.maximum(j * lb8 - 1, 0), 0)),
            pl.BlockSpec((bb, SUBLANES, 2 * B_WIDTH), lambda i, j: (i, 0, 0)),
            pl.BlockSpec((bb, L, LANES), lambda i, j: (i, j, 0)),
            pl.BlockSpec((None, None, SUBLANES, R), lambda i, j: (i, j, 0, 0)),
            pl.BlockSpec((B_HEADS, HEAD_DIM, bb * HEAD_DIM), lambda i, j: (0, 0, i)),
            pl.BlockSpec((None, B_HEADS, HEAD_DIM, LANES), lambda i, j: (i, 0, 0, 0)),
            pl.BlockSpec((bb, 1, LANES), lambda i, j: (i, 0, 0)),
            pl.BlockSpec((SUBLANES, 2 * B_WIDTH), lambda i, j: (0, 0)),
            pl.BlockSpec((1, 2 * B_WIDTH), lambda i, j: (0, 0)),
            pl.BlockSpec((1, LANES), lambda i, j: (0, 0)),
            pl.BlockSpec((SUBLANES, R), lambda i, j: (0, 0)),
            pl.BlockSpec((1, B_WIDTH), lambda i, j: (0, 0)),
        ],
        out_specs=[
            pl.BlockSpec((bb, L, B_WIDTH), lambda i, j: (i, j, 0)),
            pl.BlockSpec((B_HEADS, HEAD_DIM, bb * HEAD_DIM), lambda i, j: (0, 0, i)),
            pl.BlockSpec((None, B_HEADS, HEAD_DIM, LANES), lambda i, j: (i, 0, 0, 0)),
            pl.BlockSpec((bb, SUBLANES, LANES), lambda i, j: (i, 0, 0)),
        ],
        out_shape=[
            jax.ShapeDtypeStruct((B, T, B_WIDTH), F32),
            jax.ShapeDtypeStruct((B_HEADS, HEAD_DIM, B * HEAD_DIM), F32),
            jax.ShapeDtypeStruct((nbi, B_HEADS, HEAD_DIM, LANES), F32),
            jax.ShapeDtypeStruct((B, SUBLANES, LANES), F32),
        ],
        scratch_shapes=[
            pltpu.VMEM((B_HEADS, HEAD_DIM, bb * HEAD_DIM), F32),
            pltpu.VMEM((B_HEADS, HEAD_DIM, LANES), F32),
            pltpu.VMEM((bb, SUBLANES, LANES), F32),
        ],
        compiler_params=_cparams(("arbitrary", "arbitrary")),
        name="mlstm",
    )(pb, pb, conv8, pif, g_t, ct0, n_in, m_in, conv_w8, conv_b, bias_if, bias_row, norm_g)
    n_new = nt[:, :, :, 0:bb].transpose(0, 3, 1, 2).reshape(B, B_HEADS, HEAD_DIM)
    return yb, ct, n_new, mo[:, 0, 0:B_HEADS]


def _masked_softmax(s, mask):
    mx = jnp.max(jnp.where(mask, s, NEG_INF), axis=1, keepdims=True)
    e = jnp.where(mask, jnp.exp(s - mx), 0.0)
    return e / jnp.maximum(jnp.sum(e, axis=1, keepdims=True), 1e-30)


def _select_blocks(psum, m_mat, t_col, n_s, blocks_major=False):
    tq = psum.shape[0]
    imp = _mm_xsel(psum, m_mat)
    blk = _iota((tq, LANES), 1)
    cur = jnp.right_shift(t_col, 6)
    valid = (blk * SEL_BLOCK <= t_col)
    forced = (blk == 0) | (blk == cur) | (blk == cur - 1)
    score = jnp.where(valid, imp + jnp.where(forced, FORCE_BONUS, 0.0), NEG_INF)
    score = jnp.where(blk < n_s, score, -jnp.inf)
    n_sel = min(SEL_TOP, n_s)
    if tq == LANES and n_s % SUBLANES == 0:
        sc = jnp.transpose(score)[0:n_s, :]
        idx = _iota((n_s, tq), 0)
        rank = jnp.zeros((n_s, tq), F32)
        for s in range(n_s):
            row = sc[s:s + 1, :]
            ahead = (row > sc) | ((row == sc) & (idx > s))
            rank = rank + jnp.where(ahead, 1.0, 0.0)
        sel_t = jnp.where(rank < n_sel, 1.0, 0.0)
        if n_s < LANES:
            sel_t = jnp.concatenate([sel_t, jnp.zeros((LANES - n_s, tq), F32)], axis=0)
        return sel_t if blocks_major else jnp.transpose(sel_t)
    rank = jnp.zeros((tq, LANES), F32)
    for s in range(n_s):
        col = score[:, s:s + 1]
        ahead = (col > score) | ((col == score) & (blk > s))
        rank = rank + jnp.where(ahead, 1.0, 0.0)
    sel = jnp.where(rank < n_sel, 1.0, 0.0)
    return jnp.transpose(sel) if blocks_major else sel


def _stack_heads(pq, g):
    base = g * C_GROUP * HEAD_DIM
    parts = [pq[:, base + r * HEAD_DIM: base + (r + 1) * HEAD_DIM] for r in range(C_GROUP)]
    return jnp.concatenate(parts, axis=0) * (HEAD_DIM ** -0.5)


def _to_group_lanes(q, g):
    z = jnp.zeros_like(q)
    return jnp.concatenate([q, z] if g == 0 else [z, q], axis=1)


def _compress_kernel(x_ref, pw0_ref, pw1_ref, cw_ref, kc_ref, a_ref, *, rows_per_step):
    T = x_ref.shape[0]
    rs = rows_per_step
    ng = rs // CMP_STRIDE
    pool = (jnp.right_shift(_iota((ng, rs), 1), 4) == _iota((ng, rs), 0)).astype(F32)
    for c in range(T // rs):
        x = x_ref[c * rs:(c + 1) * rs, :]
        xw = jnp.concatenate([x * pw0_ref[...], x * pw1_ref[...]], axis=1)
        a_ref[c * ng:(c + 1) * ng, :] = _mm_sel(pool, xw)
    ngrp = T // CMP_STRIDE
    W = x_ref.shape[1]
    pooled = a_ref[:, 0:W] + pltpu.roll(a_ref[:, W:2 * W], ngrp - 1, axis=0)
    kc_ref[...] = _mm(pooled, cw_ref[...], 3)


def _compress(pcmp, pwt0, pwt1, cmpw_bd):
    B, T, W = pcmp.shape
    rs = pwt0.shape[0]
    ngrp = T // CMP_STRIDE
    kern = functools.partial(_compress_kernel, rows_per_step=rs)
    return pl.pallas_call(
        kern,
        grid=(B,),
        in_specs=[
            pl.BlockSpec((None, T, W), lambda b: (b, 0, 0)),
            pl.BlockSpec((rs, W), lambda b: (0, 0)),
            pl.BlockSpec((rs, W), lambda b: (0, 0)),
            pl.BlockSpec((W, W), lambda b: (0, 0)),
        ],
        out_specs=pl.BlockSpec((None, ngrp, W), lambda b: (b, 0, 0)),
        out_shape=jax.ShapeDtypeStruct((B, ngrp, W), F32),
        scratch_shapes=[pltpu.VMEM((ngrp, 2 * W), F32)],
        compiler_params=_cparams(("arbitrary",)),
        name="nsa_compress",
    )(pcmp, pwt0, pwt1, cmpw_bd)


def _combine_branches(gates, g, o_c, o_s, o_w, tq):
    outs = []
    for r in range(C_GROUP):
        rs = slice(r * tq, (r + 1) * tq)
        j = (g * C_GROUP + r) * 3
        outs.append(gates[:, j:j + 1] * o_c[rs] + gates[:, j + 1:j + 2] * o_s[rs] + gates[:, j + 2:j + 3] * o_w[rs])
    return outs


def _nsa_prompt_kernel(pq_ref, pg_ref, gb_ref, kc_ref, kv_ref, vt_ref, m_ref, et_ref, wb_ref, o_ref, *, n_s):
    i = pl.program_id(1)
    tq = pq_ref.shape[0]
    ngrp = kc_ref.shape[0]
    kt = 4 * tq
    t0 = i * tq
    rows = C_GROUP * tq
    tl = _iota((rows, 1), 0) & (tq - 1)
    t_row = t0 + tl
    t_col = t0 + _iota((tq, 1), 0)
    pq = pq_ref[...]
    gates = _sigmoid(pg_ref[...] + gb_ref[...])
    kc = kc_ref[...]
    n_end = _iota((rows, ngrp), 1) * CMP_STRIDE + (CMP_BLOCK - 1)
    cmask = n_end <= t_row
    n_full = t0 // kt
    tl_lane = _iota((1, rows), 1) & (tq - 1)
    diag_bias = jnp.where(_iota((kt, rows), 0) <= (t0 - n_full * kt) + tl_lane, 0.0, NEG_INF)
    n_tiles = WINDOW // tq + 1
    pieces = []
    for g in range(C_KV_HEADS):
        gs = slice(g * HEAD_DIM, (g + 1) * HEAD_DIM)
        q = _stack_heads(pq, g)
        q2 = _to_group_lanes(q, g).astype(BF16)
        p_c = _masked_softmax(_mm(q, kc[:, g * HEAD_DIM:(g + 1) * HEAD_DIM], 3, NT), cmask)
        o_c = _mm(p_c, kc[:, 2 * HEAD_DIM + g * HEAD_DIM: 2 * HEAD_DIM + (g + 1) * HEAD_DIM], 1)
        psum = p_c[0:tq]
        for r in range(1, C_GROUP):
            psum = psum + p_c[r * tq:(r + 1) * tq]
        sel_t = _select_blocks(psum, m_ref[...], t_col, n_s, blocks_major=True).astype(BF16)
        q2_t = jnp.transpose(_to_group_lanes(q, g)).astype(BF16)

        def step(j, carry, extra):
            m, l, acc = carry
            off = pl.multiple_of(j * kt, kt)
            kk = kv_ref[pl.ds(off, kt), 0:LANES]
            vv_t = vt_ref[:, pl.ds(off, kt)]
            sel_bias = (_dot(et_ref[pl.ds(off, kt), :], sel_t) - 1.0) * (-NEG_INF)
            s = _dot(kk, q2_t) + jnp.concatenate([sel_bias] * C_GROUP, axis=1)
            if extra is not None:
                s = s + extra
            m_new = jnp.maximum(m, jnp.max(s, axis=0, keepdims=True))
            p = jnp.exp(s - m_new)
            alpha = jnp.exp(m - m_new)
            l = alpha * l + jnp.sum(p, axis=0, keepdims=True)
            acc = alpha * acc + _dot(vv_t, p.astype(BF16))
            return m_new, l, acc

        init = (jnp.full((1, rows), NEG_INF, F32), jnp.zeros((1, rows), F32), jnp.zeros((LANES, rows), F32))
        carry = lax.fori_loop(0, n_full, lambda j, c: step(j, c, None), init)
        _, l, acc = step(n_full, carry, diag_bias)
        o_s = jnp.transpose(acc / l)[:, gs]

        s_tiles, vws = [], []
        for cidx in range(n_tiles):
            tile = i - (n_tiles - 1) + cidx
            off = pl.multiple_of(jnp.maximum(tile, 0) * tq, tq)
            kw = kv_ref[pl.ds(off, tq), 2 * LANES:3 * LANES]
            vws.append(kv_ref[pl.ds(off, tq), 3 * LANES:4 * LANES])
            tile_bias = jnp.where(tile >= 0, 0.0, NEG_INF)
            s_tiles.append(_dot(q2, kw, NT) + (wb_ref[:, cidx * tq:(cidx + 1) * tq] + tile_bias))
        s_w = jnp.concatenate(s_tiles, axis=1)
        e_w = jnp.exp(s_w - jnp.max(s_w, axis=1, keepdims=True))
        p_w = e_w / jnp.sum(e_w, axis=1, keepdims=True)
        o_w = _dot(p_w.astype(BF16), jnp.concatenate(vws, axis=0))[:, gs]
        pieces += _combine_branches(gates, g, o_c, o_s, o_w, tq)
    o_ref[...] = jnp.concatenate(pieces, axis=1)


def _nsa_prompt(pq, pg, gate_b, kc, kvb, m_mat, e_mat, tq):
    B, T, _ = pq.shape
    ngrp = kc.shape[1]
    n_s = T // SEL_BLOCK
    n_tiles = WINDOW // tq + 1
    kp = np.arange(n_tiles * tq)[None, :] - (n_tiles - 1) * tq
    tloc = (np.arange(C_GROUP * tq) % tq)[:, None]
    wbias = jnp.asarray(np.where((kp <= tloc) & (kp > tloc - WINDOW), 0.0, NEG_INF).astype(np.float32))
    kern = functools.partial(_nsa_prompt_kernel, n_s=n_s)
    return pl.pallas_call(
        kern,
        grid=(B, T // tq),
        in_specs=[
            pl.BlockSpec((None, tq, C_WIDTH), lambda b, i: (b, i, 0)),
            pl.BlockSpec((None, tq, LANES), lambda b, i: (b, i, 0)),
            pl.BlockSpec((1, LANES), lambda b, i: (0, 0)),
            pl.BlockSpec((None, ngrp, 4 * HEAD_DIM), lambda b, i: (b, 0, 0)),
            pl.BlockSpec((None, T, 4 * LANES), lambda b, i: (b, 0, 0)),
            pl.BlockSpec((None, LANES, T), lambda b, i: (b, 0, 0)),
            pl.BlockSpec((ngrp, LANES), lambda b, i: (0, 0)),
            pl.BlockSpec((T, LANES), lambda b, i: (0, 0)),
            pl.BlockSpec((C_GROUP * tq, n_tiles * tq), lambda b, i: (0, 0)),
        ],
        out_specs=pl.BlockSpec((None, tq, C_WIDTH), lambda b, i: (b, i, 0)),
        out_shape=jax.ShapeDtypeStruct((B, T, C_WIDTH), F32),
        compiler_params=_cparams(("arbitrary", "arbitrary")),
        name="nsa_prompt",
    )(pq, pg, gate_b, kc, kvb, jnp.swapaxes(kvb[:, :, LANES:2 * LANES], 1, 2), m_mat, jnp.transpose(e_mat), wbias)


def _nsa_sample_kernel(pt_ref, *refs, n_pages, nseq, past_len, n_s, n_c):
    for s in range(nseq):
        _nsa_sample_seq(s, refs[s * n_pages:(s + 1) * n_pages], *refs[nseq * n_pages:], past_len=past_len, n_s=n_s, n_c=n_c)


def _nsa_sample_seq(s, pages, pq_ref, pg_ref, gb_ref, pcmp_ref, pslc_ref, pwin_ref, win_ref, pw0_ref, pw1_ref, cw_ref,
                    m_ref, e_ref, o_ref, k_ref, v_ref, *, past_len, n_s, n_c):
    n_pages = len(pages)
    tq = pq_ref.shape[1]
    rows = C_GROUP * tq
    W = 4 * HEAD_DIM
    ng = PAGE_SIZE // CMP_STRIDE
    ngrp = m_ref.shape[0]
    pw0 = pw0_ref[...]
    pw1 = pw1_ref[...]

    def tail_t(x):
        return jnp.transpose(jnp.concatenate([x, jnp.zeros((PAGE_SIZE - tq, x.shape[1]), F32)], axis=0))

    xs = [pages[p][0:W, :] for p in range(n_pages)] + [tail_t(pcmp_ref[s])]
    a_acc = jnp.zeros((2 * W, ngrp), F32)
    for p0 in range(0, n_pages + 1, 2):
        grp = xs[p0:p0 + 2]
        xw = jnp.concatenate([jnp.concatenate([x * pw0, x * pw1], axis=0) for x in grp], axis=1)
        kdim = xw.shape[1]
        sel = (jnp.right_shift(_iota((kdim, ngrp), 0), 4) + p0 * ng == _iota((kdim, ngrp), 1)).astype(BF16)
        hi = xw.astype(BF16)
        lo = (xw - hi.astype(F32)).astype(BF16)
        a_acc = a_acc + (_dot(hi, sel) + _dot(lo, sel))
    pooled_t = a_acc[0:W, :] + pltpu.roll(a_acc[W:2 * W, :], ngrp - 1, axis=1)
    kc_t = _mm(cw_ref[...], pooled_t, 3)
    for p in range(n_pages):
        k_ref[s, :, p * PAGE_SIZE:(p + 1) * PAGE_SIZE] = pages[p][W:W + LANES, :].astype(BF16)
        v_ref[s, :, p * PAGE_SIZE:(p + 1) * PAGE_SIZE] = pages[p][W + LANES:W + 2 * LANES, :].astype(BF16)
    pslc_t = tail_t(pslc_ref[s])
    k_ref[s, :, past_len:past_len + PAGE_SIZE] = pslc_t[0:LANES].astype(BF16)
    v_ref[s, :, past_len:past_len + PAGE_SIZE] = pslc_t[LANES:2 * LANES].astype(BF16)

    nk = k_ref.shape[2]
    t_row = past_len + (_iota((rows, 1), 0) & (tq - 1))
    t_col = past_len + _iota((tq, 1), 0)
    pq = pq_ref[s]
    gates = _sigmoid(pg_ref[s] + gb_ref[...])
    n_idx = _iota((rows, ngrp), 1)
    cmask = (n_idx * CMP_STRIDE + (CMP_BLOCK - 1) <= t_row) & (n_idx < n_c)
    kpos = _iota((rows, nk), 1)
    win_t = win_ref[s]
    pwin_t = tail_t(pwin_ref[s])
    nwb = win_t.shape[1]
    kw_t = jnp.concatenate([win_t[0:LANES], pwin_t[0:LANES]], axis=1).astype(BF16)
    vw_t = jnp.concatenate([win_t[LANES:2 * LANES], pwin_t[LANES:2 * LANES]], axis=1).astype(BF16)
    jj = _iota((rows, nwb + PAGE_SIZE), 1)
    tl = _iota((rows, 1), 0) & (tq - 1)
    wmask = (jj > tl + (nwb - WINDOW)) & (jj <= tl + nwb) & (jj < nwb + tq)
    pieces = []
    for g in range(C_KV_HEADS):
        gs = slice(g * HEAD_DIM, (g + 1) * HEAD_DIM)
        q = _stack_heads(pq, g)
        q2 = _to_group_lanes(q, g).astype(BF16)
        p_c = _masked_softmax(_mm(q, kc_t[g * HEAD_DIM:(g + 1) * HEAD_DIM, :], 3), cmask)
        o_c = _mm(p_c, kc_t[2 * HEAD_DIM + g * HEAD_DIM: 2 * HEAD_DIM + (g + 1) * HEAD_DIM, :], 1, NT)
        psum = p_c[0:tq]
        for r in range(1, C_GROUP):
            psum = psum + p_c[r * tq:(r + 1) * tq]
        sel = _select_blocks(psum, m_ref[...], t_col, n_s)
        sel_r = jnp.concatenate([sel] * C_GROUP, axis=0).astype(BF16)
        smask = (_dot(sel_r, e_ref[...]) > 0.5) & (kpos <= t_row)
        p_s = _masked_softmax(_dot(q2, k_ref[s]), smask)
        o_s = _dot(p_s.astype(BF16), v_ref[s], NT)[:, gs]
        p_w = _masked_softmax(_dot(q2, kw_t), wmask)
        o_w = _dot(p_w.astype(BF16), vw_t, NT)[:, gs]
        pieces += _combine_branches(gates, g, o_c, o_s, o_w, tq)
    o_ref[s] = jnp.concatenate(pieces, axis=1)


def _nsa_sample(layer, page_table, cache_t, pq, pg, gate_b, pcmp, pslc, pwin, win_t, pwt0, pwt1, cmpw_t, m_mat, e_mat):
    B, T, _ = pq.shape
    nseq = 2 if B % 2 == 0 else 1
    n_pages = page_table.shape[1]
    past_len = n_pages * PAGE_SIZE
    lp = -(-(past_len + T) // SEL_BLOCK) * SEL_BLOCK
    n_s = lp // SEL_BLOCK
    n_c = lp // CMP_STRIDE - CMP_BLOCK // CMP_STRIDE + 1
    ngrp = m_mat.shape[0]
    nk = past_len + PAGE_SIZE
    nwb = win_t.shape[3]
    kern = functools.partial(_nsa_sample_kernel, n_pages=n_pages, nseq=nseq, past_len=past_len, n_s=n_s, n_c=n_c)
    page_specs = [
        pl.BlockSpec((None, None, 4 * LANES, PAGE_SIZE),
                     functools.partial(lambda b, pt, s, p: (layer, pt[b * nseq + s, p], 0, 0), s=s, p=p))
        for s in range(nseq) for p in range(n_pages)
    ]
    row = lambda w: pl.BlockSpec((nseq, T, w), lambda b, pt: (b, 0, 0))
    full = lambda a: pl.BlockSpec(a.shape, lambda b, pt: (0,) * a.ndim)
    grid_spec = pltpu.PrefetchScalarGridSpec(
        num_scalar_prefetch=1,
        grid=(B // nseq,),
        in_specs=page_specs + [
            row(C_WIDTH), row(LANES), full(gate_b), row(4 * HEAD_DIM), row(4 * HEAD_DIM), row(4 * HEAD_DIM),
            pl.BlockSpec((None, nseq, 4 * HEAD_DIM, nwb), lambda b, pt: (layer, b, 0, 0)),
            full(pwt0), full(pwt1), full(cmpw_t), full(m_mat), full(e_mat),
        ],
        out_specs=pl.BlockSpec((nseq, T, C_WIDTH), lambda b, pt: (b, 0, 0)),
        scratch_shapes=[
            pltpu.VMEM((nseq, LANES, nk), BF16),
            pltpu.VMEM((nseq, LANES, nk), BF16),
        ],
    )
    return pl.pallas_call(
        kern,
        grid_spec=grid_spec,
        out_shape=jax.ShapeDtypeStruct((B, T, C_WIDTH), F32),
        compiler_params=_cparams(("arbitrary",)),
        name="nsa_sample",
    )(page_table, *([cache_t] * (nseq * n_pages)), pq, pg, gate_b, pcmp, pslc, pwin, win_t, pwt0, pwt1, cmpw_t, m_mat, e_mat)


def _nsa_consts(lp, n_keys):
    n_str = lp // CMP_STRIDE
    n_c = n_str - CMP_BLOCK // CMP_STRIDE + 1
    n_s = lp // SEL_BLOCK
    c0 = np.arange(n_str)[:, None] * CMP_STRIDE
    s0 = np.arange(LANES)[None, :] * SEL_BLOCK
    m = (c0 < s0 + SEL_BLOCK) & (c0 + CMP_BLOCK > s0) & (np.arange(n_str)[:, None] < n_c) & (np.arange(LANES)[None, :] < n_s)
    e = (np.arange(n_keys)[None, :] // SEL_BLOCK) == np.arange(LANES)[:, None]
    return m.astype(np.float32), e.astype(np.float32)


def _prep_mlstm(conv_w, conv_b, i_b, f_b, norm_g):
    cw8 = jnp.pad(conv_w, ((0, SUBLANES - MLSTM_CONV), (0, 0)))
    bias_if = jnp.pad(jnp.concatenate([i_b, f_b]), (0, LANES - 2 * B_HEADS)).reshape(1, LANES)
    return cw8, conv_b.reshape(1, 2 * B_WIDTH), bias_if, norm_g.reshape(1, B_WIDTH)


_A0, _B0, _Q0, _CMP0, _SLC0, _WIN0, _G0, _IF0, _PEND = 0, 1024, 2048, 2560, 2816, 3072, 3328, 3456, 3584
_PROJ_SPLITS = ((_A0, _B0), (_B0, _Q0), (_Q0, _CMP0), (_CMP0, _SLC0), (_SLC0, _WIN0), (_WIN0, _G0), (_G0, _IF0),
                (_IF0, _PEND))


def _pad_lanes(a, width):
    return jnp.pad(a, [(0, 0)] * (a.ndim - 1) + [(0, width - a.shape[-1])])


def _prep_layer(P, l):
    w_in = P["w_in"][l]
    wa, wb, wc = w_in[:, 0:1024], w_in[:, 1024:2056], w_in[:, 2056:3360]
    w_in_p = jnp.concatenate([
        wa, wb[:, 0:768], wb[:, 776:1032], wc[:, 0:512], wc[:, 512:1280],
        _pad_lanes(wc[:, 1280:1304], LANES), _pad_lanes(wb[:, 768:776], LANES)], axis=1).astype(BF16)
    up = P["ffn_up"][l]
    ffn_up_p = jnp.concatenate([_pad_lanes(up[:, 0:D_FF], D_FF_PAD), _pad_lanes(up[:, D_FF:], D_FF_PAD)], axis=1).astype(BF16)
    cw = P["nsa_cmp_w"][l]
    cmpw_bd = jnp.zeros((4 * HEAD_DIM, 4 * HEAD_DIM), F32)
    for kv in range(2):
        for g in range(C_KV_HEADS):
            o = (kv * C_KV_HEADS + g) * HEAD_DIM
            cmpw_bd = cmpw_bd.at[o:o + HEAD_DIM, o:o + HEAD_DIM].set(cw[kv, g])
    pool = P["nsa_pool_w"][l].reshape(CMP_BLOCK, 4 * HEAD_DIM)
    return dict(
        w_in=w_in_p,
        w_out=P["w_out"][l].astype(BF16),
        ffn_up=ffn_up_p,
        ffn_down=jnp.pad(P["ffn_down"][l], ((0, D_FF_PAD - D_FF), (0, 0))).astype(BF16),
        ffn_cw=jnp.pad(P["ffn_conv_w"][l], ((0, SUBLANES - FFN_CONV), (0, D_FF_PAD - D_FF))),
        ffn_cb=_pad_lanes(P["ffn_conv_b"][l].reshape(1, D_FF), D_FF_PAD),
        ln1=(P["ln_g"][l, 0].reshape(1, D_MODEL), P["ln_b"][l, 0].reshape(1, D_MODEL)),
        ln2=(P["ln_g"][l, 1].reshape(1, D_MODEL), P["ln_b"][l, 1].reshape(1, D_MODEL)),
        rwkv=_prep_rwkv(P["rwkv_mu"][l], P["rwkv_w0"][l], P["rwkv_w2"][l], P["rwkv_a0"][l], P["rwkv_a2"][l],
                        P["rwkv_g2"][l], P["rwkv_k_k"][l], P["rwkv_k_a"][l], P["rwkv_r_k"][l], P["rwkv_ln_g"][l],
                        P["rwkv_ln_b"][l]),
        mlstm=_prep_mlstm(P["mlstm_conv_w"][l], P["mlstm_conv_b"][l], P["mlstm_i_b"][l], P["mlstm_f_b"][l],
                          P["mlstm_norm_g"][l]),
        pool0=pool[0:CMP_STRIDE], pool1=pool[CMP_STRIDE:CMP_BLOCK], cmpw=cmpw_bd,
        gate_b=_pad_lanes(P["nsa_gate_b"][l].reshape(1, 3 * C_HEADS), LANES),
    )


def _rows8(state):
    return jnp.pad(state, ((0, 0), (SUBLANES - state.shape[1], 0), (0, 0)))


def _last_rows(prev, cur, k):
    if cur.shape[1] >= k:
        return cur[:, cur.shape[1] - k:]
    return jnp.concatenate([prev, cur], axis=1)[:, -k:]


def _trunk(x, mod, st, layers, nsa_fn, dense_tile, rec_tile):
    B, T, _ = x.shape
    dbb, dL = dense_tile
    rbb, rL = rec_tile
    new = {k: [] for k in ("nsa_kv", "win_kv", "rwkv", "rwkv_shift", "mlstm_C", "mlstm_n", "mlstm_m", "mlstm_conv", "ffn_conv")}
    for l, Lw in enumerate(layers):
        m = mod[l]
        pa, pb, pq, pcmp, pslc, pwin, pg, pif, kvb = _modmm(
            x, m, 0, 1, Lw["w_in"], _PROJ_SPLITS + ((_SLC0, _G0),), (False,) * 8 + (True,), dbb, dL, "in_proj")
        ya, rw_st = _rwkv(pa, _rows8(st["rwkv_shift"][l][:, None, :]), _state_to_lanes(st["rwkv"][l]), *Lw["rwkv"], rbb, rL)
        yb, c_st, n_st, m_st = _mlstm(pb, _rows8(st["mlstm_conv"][l]), pif, _state_to_lanes(st["mlstm_C"][l]),
                                      st["mlstm_n"][l], st["mlstm_m"][l], *Lw["mlstm"], rbb, rL)
        yc, win_new = nsa_fn(l, Lw, pq, pg, pcmp, pslc, pwin, kvb)
        x = _outproj(ya, yb, yc, x, m, Lw["w_out"], *Lw["ln1"], dbb, dL)
        (u,) = _modmm(x, m, 3, 4, Lw["ffn_up"], ((0, 2 * D_FF_PAD),), (False,), dbb, dL, "ffn_up")
        st8 = _rows8(_pad_lanes(st["ffn_conv"][l], D_FF_PAD))
        x = _ffn_down(u, st8, x, m, Lw["ffn_cw"], Lw["ffn_cb"], Lw["ffn_down"], *Lw["ln2"], dbb, dL)
        new["nsa_kv"].append(jnp.concatenate([pcmp, pslc], axis=-1).reshape(B, T, 4, C_KV_HEADS, HEAD_DIM))
        new["win_kv"].append(win_new)
        new["rwkv"].append(_state_from_lanes(rw_st, B))
        new["rwkv_shift"].append(pa[:, -1])
        new["mlstm_C"].append(_state_from_lanes(c_st, B))
        new["mlstm_n"].append(n_st)
        new["mlstm_m"].append(m_st)
        new["mlstm_conv"].append(_last_rows(st["mlstm_conv"][l], pb[:, :, 0:2 * B_WIDTH], MLSTM_CONV - 1))
        new["ffn_conv"].append(_last_rows(st["ffn_conv"][l], u[:, :, 0:D_FF], FFN_CONV - 1))
    return x, {k: jnp.stack(v) for k, v in new.items()}


def kernel(x_prompt, x_sample, c_prompt, c_sample, cache_nsa_kv, cache_win_kv, state_rwkv, state_rwkv_shift,
           state_mlstm_C, state_mlstm_n, state_mlstm_m, state_mlstm_conv, state_ffn_conv, page_table,
           w_in, w_out, ada_w, ada_b, ln_g, ln_b, rwkv_mu, rwkv_w0, rwkv_w2, rwkv_a0, rwkv_a2, rwkv_g2,
           rwkv_k_k, rwkv_k_a, rwkv_r_k, rwkv_ln_g, rwkv_ln_b, mlstm_conv_w, mlstm_conv_b, mlstm_i_b,
           mlstm_f_b, mlstm_norm_g, nsa_pool_w, nsa_cmp_w, nsa_gate_b, ffn_up, ffn_conv_w, ffn_conv_b, ffn_down):
    P = dict(w_in=w_in, w_out=w_out, ln_g=ln_g, ln_b=ln_b, rwkv_mu=rwkv_mu, rwkv_w0=rwkv_w0, rwkv_w2=rwkv_w2,
             rwkv_a0=rwkv_a0, rwkv_a2=rwkv_a2, rwkv_g2=rwkv_g2, rwkv_k_k=rwkv_k_k, rwkv_k_a=rwkv_k_a,
             rwkv_r_k=rwkv_r_k, rwkv_ln_g=rwkv_ln_g, rwkv_ln_b=rwkv_ln_b, mlstm_conv_w=mlstm_conv_w,
             mlstm_conv_b=mlstm_conv_b, mlstm_i_b=mlstm_i_b, mlstm_f_b=mlstm_f_b, mlstm_norm_g=mlstm_norm_g,
             nsa_pool_w=nsa_pool_w, nsa_cmp_w=nsa_cmp_w, nsa_gate_b=nsa_gate_b, ffn_up=ffn_up,
             ffn_conv_w=ffn_conv_w, ffn_conv_b=ffn_conv_b, ffn_down=ffn_down)
    Bp, Tp, _ = x_prompt.shape
    Bs, Ts, _ = x_sample.shape
    G, dh = C_KV_HEADS, HEAD_DIM
    layers = [_prep_layer(P, l) for l in range(DEPTH)]

    nb = -(-(Bp + Bs) // SUBLANES) * SUBLANES
    c_all = jnp.pad(jnp.concatenate([c_prompt, c_sample], axis=0), ((0, nb - Bp - Bs), (0, 0)))
    mod = _ada(c_all, ada_w, ada_b)
    mod_p = mod[:, 0:Bp].reshape(DEPTH, Bp, 1, 6 * D_MODEL)
    mod_s = mod[:, Bp:Bp + Bs].reshape(DEPTH, Bs, 1, 6 * D_MODEL)

    st_p = dict(
        rwkv=jnp.zeros((DEPTH, Bp, A_HEADS, dh, dh), F32), rwkv_shift=jnp.zeros((DEPTH, Bp, 4 * A_WIDTH), F32),
        mlstm_C=jnp.zeros((DEPTH, Bp, B_HEADS, dh, dh), F32), mlstm_n=jnp.zeros((DEPTH, Bp, B_HEADS, dh), F32),
        mlstm_m=jnp.zeros((DEPTH, Bp, B_HEADS), F32), mlstm_conv=jnp.zeros((DEPTH, Bp, MLSTM_CONV - 1, 2 * B_WIDTH), F32),
        ffn_conv=jnp.zeros((DEPTH, Bp, FFN_CONV - 1, D_FF), F32))
    m_p, e_p = _nsa_consts(Tp, Tp)
    e_p = jnp.asarray(e_p, BF16)
    tq = 128
    rs = 4 * tq

    def nsa_prompt(l, Lw, pq, pg, pcmp, pslc, pwin, kvb):
        kc = _compress(pcmp, jnp.tile(Lw["pool0"], (rs // CMP_STRIDE, 1)), jnp.tile(Lw["pool1"], (rs // CMP_STRIDE, 1)), Lw["cmpw"])
        yc = _nsa_prompt(pq, pg, Lw["gate_b"], kc, kvb, jnp.asarray(m_p), e_p, tq)
        return yc, pwin[:, -min(WINDOW, Tp):].reshape(Bp, min(WINDOW, Tp), 2, G, dh)

    y_prompt, new_p = _trunk(x_prompt, mod_p, st_p, layers, nsa_prompt, (1, 256), (Bp, ROWS // Bp))

    st_s = dict(rwkv=state_rwkv, rwkv_shift=state_rwkv_shift, mlstm_C=state_mlstm_C, mlstm_n=state_mlstm_n,
                mlstm_m=state_mlstm_m, mlstm_conv=state_mlstm_conv, ffn_conv=state_ffn_conv)
    n_pages = page_table.shape[1]
    past_len = n_pages * PAGE_SIZE
    lp = -(-(past_len + Ts) // SEL_BLOCK) * SEL_BLOCK
    ngrp = 2 * LANES
    assert (n_pages + 1) * (PAGE_SIZE // CMP_STRIDE) <= ngrp
    m_s, e_s = _nsa_consts(lp, past_len + PAGE_SIZE)
    m_s = jnp.asarray(np.pad(m_s, ((0, ngrp - m_s.shape[0]), (0, 0))))
    e_s = jnp.asarray(e_s, BF16)
    cache_t = jnp.transpose(cache_nsa_kv, (0, 1, 3, 4, 5, 2)).reshape(DEPTH, cache_nsa_kv.shape[1], 4 * G * dh, PAGE_SIZE)
    nwb = cache_win_kv.shape[2]
    win_t_all = jnp.transpose(cache_win_kv, (0, 1, 3, 4, 5, 2)).reshape(DEPTH, Bs, 2 * G * dh, nwb)
    reps = PAGE_SIZE // CMP_STRIDE

    def nsa_sample(l, Lw, pq, pg, pcmp, pslc, pwin, kvb):
        yc = _nsa_sample(l, page_table, cache_t, pq, pg, Lw["gate_b"], pcmp, pslc, pwin, win_t_all,
                         jnp.tile(Lw["pool0"].T, (1, reps)), jnp.tile(Lw["pool1"].T, (1, reps)),
                         Lw["cmpw"].T, m_s, e_s)
        win_all = jnp.concatenate([cache_win_kv[l], pwin.reshape(Bs, Ts, 2, G, dh)], axis=1)
        keep = min(WINDOW, nwb + Ts)
        return yc, win_all[:, -keep:]

    y_sample, new_s = _trunk(x_sample, mod_s, st_s, layers, nsa_sample, (ROWS // Ts, Ts), (ROWS // Ts, Ts))

    return (y_prompt, y_sample,
            new_p["nsa_kv"], new_s["nsa_kv"], new_p["win_kv"], new_s["win_kv"],
            new_p["rwkv"], new_s["rwkv"], new_p["rwkv_shift"], new_s["rwkv_shift"],
            new_p["mlstm_C"], new_s["mlstm_C"], new_p["mlstm_n"], new_s["mlstm_n"],
            new_p["mlstm_m"], new_s["mlstm_m"], new_p["mlstm_conv"], new_s["mlstm_conv"],
            new_p["ffn_conv"], new_s["ffn_conv"])
```

```python
import functools
import math

import numpy as np
import jax
import jax.numpy as jnp
from jax import lax
from jax.experimental import pallas as pl
from jax.experimental.pallas import tpu as pltpu

F32 = jnp.float32
BF16 = jnp.bfloat16

D_MODEL = 1024
DEPTH = 4
HEAD_DIM = 64
A_WIDTH = 256
B_WIDTH = 256
C_WIDTH = 512
A_HEADS = 4
B_HEADS = 4
C_HEADS = 8
C_KV_HEADS = 2
C_GROUP = 4
PAGE_SIZE = 128
MLSTM_CONV = 4
CMP_BLOCK = 32
CMP_STRIDE = 16
SEL_BLOCK = 64
SEL_TOP = 16
WINDOW = 512
D_FF = 2752
D_FF_PAD = 2816
FFN_CONV = 3
ALPHA = (2 * DEPTH) ** 0.25
FORCE_BONUS = 1e4
NEG_INF = -1e30
LANES = 128
SUBLANES = 8
ROWS = 256
VMEM_LIMIT = 56 * 1024 * 1024

NN = (((1,), (0,)), ((), ()))
NT = (((1,), (1,)), ((), ()))


def _dot(a, b, dn=NN):
    return lax.dot_general(a, b, dn, preferred_element_type=F32)


def _split2(a):
    hi = a.astype(BF16)
    lo = (a - hi.astype(F32)).astype(BF16)
    return hi, lo


def _mm(a, b, passes=1, dn=NN):
    if passes == 1:
        return _dot(a.astype(BF16), b.astype(BF16), dn)
    ah, al = _split2(a)
    bh, bl = _split2(b)
    return _dot(ah, bh, dn) + (_dot(al, bh, dn) + _dot(ah, bl, dn))


def _mm_sel(sel, x, dn=NN):
    s = sel.astype(BF16)
    x1 = x.astype(BF16)
    r1 = x - x1.astype(F32)
    x2 = r1.astype(BF16)
    x3 = (r1 - x2.astype(F32)).astype(BF16)
    return _dot(s, x1, dn) + (_dot(s, x2, dn) + _dot(s, x3, dn))


def _mm_xsel(x, sel, dn=NN):
    s = sel.astype(BF16)
    x1 = x.astype(BF16)
    r1 = x - x1.astype(F32)
    x2 = r1.astype(BF16)
    x3 = (r1 - x2.astype(F32)).astype(BF16)
    return _dot(x1, s, dn) + (_dot(x2, s, dn) + _dot(x3, s, dn))


def _sigmoid(x):
    return 1.0 / (1.0 + jnp.exp(-x))


def _silu(x):
    return x * _sigmoid(x)


def _softplus(x):
    return jnp.maximum(x, 0.0) + jnp.log(1.0 + jnp.exp(-jnp.abs(x)))


def _log_sigmoid(x):
    return -_softplus(-x)


def _iota(shape, axis):
    return lax.broadcasted_iota(jnp.int32, shape, axis)


def _block_masks(rows, chunk):
    sh = int(math.log2(chunk))
    r = _iota((rows, rows), 0)
    s = _iota((rows, rows), 1)
    same = jnp.right_shift(r, sh) == jnp.right_shift(s, sh)
    return same, same & (s <= r), same & (s < r)


def _head_ones(width):
    r = _iota((width, width), 0)
    s = _iota((width, width), 1)
    return (jnp.right_shift(r, 6) == jnp.right_shift(s, 6)).astype(F32)


def _expand_mat(rows, chunk, nseq):
    sh = int(math.log2(chunk))
    r = _iota((rows, nseq * HEAD_DIM), 0)
    c = _iota((rows, nseq * HEAD_DIM), 1)
    return ((jnp.right_shift(c, 6) == jnp.right_shift(r, sh)) & ((r & (chunk - 1)) == 0)).astype(F32)


def _seq_lane_mask(rows, chunk, nseq):
    sh = int(math.log2(chunk))
    r = _iota((rows, nseq * HEAD_DIM), 0)
    c = _iota((rows, nseq * HEAD_DIM), 1)
    return jnp.right_shift(c, 6) == jnp.right_shift(r, sh)


def _fold_mat(nseq):
    r = _iota((nseq * HEAD_DIM, HEAD_DIM), 0)
    c = _iota((nseq * HEAD_DIM, HEAD_DIM), 1)
    return ((r & (HEAD_DIM - 1)) == c).astype(F32)


def _shifted_rows(pre, cur, nshift):
    bb, L, C = cur.shape
    full = jnp.concatenate([pre, cur], axis=1).reshape(bb * (L + SUBLANES), C)
    out = []
    for k in range(1, nshift + 1):
        sh = pltpu.roll(full, k, axis=0).reshape(bb, L + SUBLANES, C)
        out.append(sh[:, SUBLANES:, :])
    return out


def _layernorm(z, g, b):
    mu = jnp.mean(z, axis=-1, keepdims=True)
    zc = z - mu
    var = jnp.mean(zc * zc, axis=-1, keepdims=True)
    return zc * lax.rsqrt(var + 1e-5) * g + b


def _cparams(sem):
    return pltpu.CompilerParams(dimension_semantics=sem, vmem_limit_bytes=VMEM_LIMIT)


def _ada_kernel(c_ref, w_ref, b_ref, o_ref):
    c = c_ref[...]
    o_ref[...] = _mm(_silu(c), w_ref[...], 3) + b_ref[...]


def _ada(c_all, ada_w, ada_b):
    nb = c_all.shape[0]
    tn = 1536
    return pl.pallas_call(
        _ada_kernel,
        grid=(DEPTH, 6 * D_MODEL // tn),
        in_specs=[
            pl.BlockSpec((nb, D_MODEL), lambda l, n: (0, 0)),
            pl.BlockSpec((None, D_MODEL, tn), lambda l, n: (l, 0, n)),
            pl.BlockSpec((None, 1, tn), lambda l, n: (l, 0, n)),
        ],
        out_specs=pl.BlockSpec((None, nb, tn), lambda l, n: (l, 0, n)),
        out_shape=jax.ShapeDtypeStruct((DEPTH, nb, 6 * D_MODEL), F32),
        compiler_params=_cparams(("arbitrary", "arbitrary")),
        name="ada_mod",
    )(c_all, ada_w, ada_b.reshape(DEPTH, 1, 6 * D_MODEL))


def _modmm_kernel(x_ref, sh_ref, sc_ref, w_ref, *o_refs, splits, bf16_outs):
    x = x_ref[...]
    bb, L, D = x.shape
    h = (x * (1.0 + sc_ref[...]) + sh_ref[...]).reshape(bb * L, D).astype(BF16)
    o = jnp.dot(h, w_ref[...], preferred_element_type=F32)
    for (a, b), o_ref, as_bf16 in zip(splits, o_refs, bf16_outs):
        piece = o[:, a:b].reshape(bb, L, b - a)
        o_ref[...] = piece.astype(BF16) if as_bf16 else piece


def _modmm(x, mod, sh_col, sc_col, w, splits, bf16_outs, bb, L, name):
    B, T, D = x.shape
    N = w.shape[1]
    kern = functools.partial(_modmm_kernel, splits=splits, bf16_outs=bf16_outs)
    return pl.pallas_call(
        kern,
        grid=(B // bb, T // L),
        in_specs=[
            pl.BlockSpec((bb, L, D), lambda i, j: (i, j, 0)),
            pl.BlockSpec((bb, 1, D), lambda i, j: (i, 0, sh_col)),
            pl.BlockSpec((bb, 1, D), lambda i, j: (i, 0, sc_col)),
            pl.BlockSpec((D, N), lambda i, j: (0, 0)),
        ],
        out_specs=[pl.BlockSpec((bb, L, b - a), lambda i, j: (i, j, 0)) for a, b in splits],
        out_shape=[jax.ShapeDtypeStruct((B, T, b - a), BF16 if q else F32) for (a, b), q in zip(splits, bf16_outs)],
        compiler_params=_cparams(("arbitrary", "arbitrary")),
        name=name,
    )(x, mod, mod, w)


def _outproj_kernel(ya_ref, yb_ref, yc_ref, x_ref, g_ref, w_ref, lg_ref, lb_ref, o_ref):
    x = x_ref[...]
    bb, L, D = x.shape
    rows = bb * L
    ya = ya_ref[...].reshape(rows, A_WIDTH).astype(BF16)
    yb = yb_ref[...].reshape(rows, B_WIDTH).astype(BF16)
    yc = yc_ref[...].reshape(rows, C_WIDTH).astype(BF16)
    y = (jnp.dot(ya, w_ref[0:A_WIDTH, :], preferred_element_type=F32)
         + jnp.dot(yb, w_ref[A_WIDTH:A_WIDTH + B_WIDTH, :], preferred_element_type=F32)
         + jnp.dot(yc, w_ref[A_WIDTH + B_WIDTH:, :], preferred_element_type=F32))
    z = ALPHA * x + (1.0 + g_ref[...]) * y.reshape(bb, L, D)
    o_ref[...] = _layernorm(z, lg_ref[...], lb_ref[...])


def _outproj(ya, yb, yc, x, mod, w_out, ln_g, ln_b, bb, L):
    B, T, D = x.shape
    blk = lambda w: pl.BlockSpec((bb, L, w), lambda i, j: (i, j, 0))
    return pl.pallas_call(
        _outproj_kernel,
        grid=(B // bb, T // L),
        in_specs=[
            blk(A_WIDTH), blk(B_WIDTH), blk(C_WIDTH), blk(D),
            pl.BlockSpec((bb, 1, D), lambda i, j: (i, 0, 2)),
            pl.BlockSpec((D, D), lambda i, j: (0, 0)),
            pl.BlockSpec((1, D), lambda i, j: (0, 0)),
            pl.BlockSpec((1, D), lambda i, j: (0, 0)),
        ],
        out_specs=blk(D),
        out_shape=jax.ShapeDtypeStruct((B, T, D), F32),
        compiler_params=_cparams(("arbitrary", "arbitrary")),
        name="outproj_ln",
    )(ya, yb, yc, x, mod, w_out, ln_g, ln_b)


def _ffn_down_kernel(ug_ref, uv_ref, halo_ref, st_ref, x_ref, g_ref, cw_ref, cb_ref, w_ref, lg_ref, lb_ref, o_ref):
    ug = ug_ref[...]
    bb, L, N = ug.shape
    first = pl.program_id(1) == 0
    pre = jnp.where(first, st_ref[...], halo_ref[...])
    u1, u2 = _shifted_rows(pre, ug, FFN_CONV - 1)
    cw = cw_ref[...]
    conv = cb_ref[...] + ug * cw[2:3, :] + u1 * cw[1:2, :] + u2 * cw[0:1, :]
    a = (_silu(conv) * uv_ref[...]).reshape(bb * L, N).astype(BF16)
    y = jnp.dot(a, w_ref[...], preferred_element_type=F32)
    x = x_ref[...]
    z = ALPHA * x + (1.0 + g_ref[...]) * y.reshape(x.shape)
    o_ref[...] = _layernorm(z, lg_ref[...], lb_ref[...])


def _ffn_down(u, st8, x, mod, conv_w, conv_b, w_down, ln_g, ln_b, bb, L):
    B, T, D = x.shape
    N = D_FF_PAD
    lb8 = L // SUBLANES
    return pl.pallas_call(
        _ffn_down_kernel,
        grid=(B // bb, T // L),
        in_specs=[
            pl.BlockSpec((bb, L, N), lambda i, j: (i, j, 0)),
            pl.BlockSpec((bb, L, N), lambda i, j: (i, j, 1)),
            pl.BlockSpec((bb, SUBLANES, N), lambda i, j: (i, jnp.maximum(j * lb8 - 1, 0), 0)),
            pl.BlockSpec((bb, SUBLANES, N), lambda i, j: (i, 0, 0)),
            pl.BlockSpec((bb, L, D), lambda i, j: (i, j, 0)),
            pl.BlockSpec((bb, 1, D), lambda i, j: (i, 0, 5)),
            pl.BlockSpec((SUBLANES, N), lambda i, j: (0, 0)),
            pl.BlockSpec((1, N), lambda i, j: (0, 0)),
            pl.BlockSpec((N, D), lambda i, j: (0, 0)),
            pl.BlockSpec((1, D), lambda i, j: (0, 0)),
            pl.BlockSpec((1, D), lambda i, j: (0, 0)),
        ],
        out_specs=pl.BlockSpec((bb, L, D), lambda i, j: (i, j, 0)),
        out_shape=jax.ShapeDtypeStruct((B, T, D), F32),
        compiler_params=_cparams(("arbitrary", "arbitrary")),
        name="ffn_down_ln",
    )(u, u, u, st8, x, mod, conv_w, conv_b, w_down, ln_g, ln_b)


def _unit_lower_inverse(n_mat, chunk):
    rows = n_mat.shape[0]
    eye = (_iota((rows, rows), 0) == _iota((rows, rows), 1)).astype(F32)
    p = n_mat.astype(BF16)
    t = eye + n_mat
    for _ in range(int(math.log2(chunk)) - 1):
        p32 = _dot(p, p)
        p = p32.astype(BF16)
        t = t + _dot(p, t.astype(BF16))
    resid = (eye - t) + _mm(n_mat, t, 3)
    return t + _dot(t.astype(BF16), resid.astype(BF16))


def _rwkv_kernel(pa_ref, halo_ref, sh_ref, st0_ref, mu_ref, pv_ref, lw_ref, ya_ref, sto_ref, st_ref, *, chunk):
    c = pl.program_id(1)
    bb, L, _ = pa_ref.shape
    R = bb * L
    W = A_WIDTH

    @pl.when(c == 0)
    def _():
        st_ref[...] = st0_ref[...]

    pa = pa_ref[...]
    pre = jnp.where(c == 0, sh_ref[...], halo_ref[...])
    (prev,) = _shifted_rows(pre, pa, 1)
    x = pa.reshape(R, 4 * W)
    pm = x + (prev.reshape(R, 4 * W) - x) * mu_ref[...]
    r = pm[:, 0:W]
    k = pm[:, W:2 * W]
    v = pm[:, 2 * W:3 * W]
    lo = pm[:, 3 * W:4 * W]
    lane = _iota((R, W), 1)
    z = jnp.where(lane < 64, jnp.tanh(lo), jnp.where(lane < 128, lo, _sigmoid(lo)))
    lora = _mm(z, lw_ref[...], 3)
    pv = pv_ref[...]
    w0, a0, k_k, k_a, r_k, ln_g, ln_b = (pv[i:i + 1, :] for i in range(7))
    w = -_softplus(-(w0 + lora[:, 0:W])) - 0.5
    a = _sigmoid(a0 + lora[:, W:2 * W])
    g = lora[:, 2 * W:3 * W]
    ones_h = _head_ones(W)
    kk = k * k_k
    kk = kk / jnp.maximum(jnp.sqrt(_mm_xsel(kk * kk, ones_h)), 1e-12)
    k2 = k * (1.0 + (a - 1.0) * k_a)
    lw = -jnp.exp(w)
    same, incl, strict = _block_masks(R, L)
    cum = _mm_sel(incl.astype(F32), lw)
    tot = _mm_sel(same.astype(F32), lw)
    e_neg = jnp.exp(-cum)
    e_rem = jnp.exp(tot - cum)
    kb = kk * a
    a_t = -kk * jnp.exp(cum - lw)
    b_t = kb * e_neg
    k_t = k2 * e_neg
    r_t = r * jnp.exp(cum)
    bh_t = jnp.transpose(kb * e_rem)
    kh_t = jnp.transpose(k2 * e_rem)
    gam_t = jnp.transpose(jnp.exp(tot))
    per_seq = bb <= 4
    if not per_seq:
        expand = _expand_mat(R, L, bb)
        lmask = _seq_lane_mask(R, L, bb)
        lmask2 = jnp.concatenate([lmask, lmask], axis=0)
        fold = _fold_mat(bb)
        tile = jnp.transpose(fold)
    outs, new_states = [], []
    states = [st_ref[h] for h in range(A_HEADS)]
    for h in range(A_HEADS):
        hs = slice(h * HEAD_DIM, (h + 1) * HEAD_DIM)
        st_h = states[h]
        A, Bt, Kt, Rt, V = a_t[:, hs], b_t[:, hs], k_t[:, hs], r_t[:, hs], v[:, hs]
        m_ab = jnp.where(strict, _mm(A, Bt, 3, NT), 0.0)
        m_ak = jnp.where(strict, _mm(A, Kt, 1, NT), 0.0)
        g_r = _mm(Rt, jnp.concatenate([Bt, Kt], axis=0), 1, NT)
        m_rb = jnp.where(incl, g_r[:, 0:R], 0.0)
        m_rk = jnp.where(incl, g_r[:, R:2 * R], 0.0)
        if per_seq:
            ps = [_mm(jnp.concatenate([A[b * L:(b + 1) * L], Rt[b * L:(b + 1) * L]], axis=0),
                      st_h[:, b * HEAD_DIM:(b + 1) * HEAD_DIM], 1) for b in range(bb)]
            ps_a = jnp.concatenate([p[0:L] for p in ps], axis=0)
            ps_r = jnp.concatenate([p[L:2 * L] for p in ps], axis=0)
        else:
            X = jnp.concatenate([A, Rt], axis=0)
            PS = _mm_xsel(jnp.where(lmask2, _mm(X, st_h, 1), 0.0), fold)
            ps_a, ps_r = PS[0:R], PS[R:2 * R]
        Y = ps_a + _mm(m_ak, V, 1)
        U = _mm(_unit_lower_inverse(m_ab, L), Y, 3)
        UV = jnp.concatenate([U, V], axis=0)
        outs.append(ps_r + _mm(jnp.concatenate([m_rb, m_rk], axis=1), UV, 1))
        if per_seq:
            new = []
            for b in range(bb):
                ts = slice(b * L, (b + 1) * L)
                lhs = jnp.concatenate([bh_t[hs, ts], kh_t[hs, ts]], axis=1)
                uv_b = jnp.concatenate([U[ts], V[ts]], axis=0)
                new.append(gam_t[hs, b * L:b * L + 1] * st_h[:, b * HEAD_DIM:(b + 1) * HEAD_DIM] + _mm(lhs, uv_b, 1))
            new_states.append(jnp.concatenate(new, axis=1))
        else:
            UVb = jnp.where(lmask2, _mm_xsel(UV, tile), 0.0)
            lhs = jnp.concatenate([bh_t[hs, :], kh_t[hs, :]], axis=1)
            gam = _mm_xsel(gam_t[hs, :], expand)
            new_states.append(gam * st_h + _mm(lhs, UVb, 1))
    for h in range(A_HEADS):
        st_ref[h] = new_states[h]
    o = jnp.concatenate(outs, axis=1)
    inv = 1.0 / HEAD_DIM
    mu = _mm_xsel(o, ones_h) * inv
    oc = o - mu
    var = _mm_xsel(oc * oc, ones_h) * inv
    y = oc * lax.rsqrt(var + 64e-5) * ln_g + ln_b
    y = y + _mm_xsel(r * k2 * r_k, ones_h) * v
    ya_ref[...] = (y * g).reshape(bb, L, W)

    @pl.when(c == pl.num_programs(1) - 1)
    def _():
        sto_ref[...] = st_ref[...]


def _rwkv(pa, shift8, st0, mu, pvec, lora_w, bb, L):
    B, T, _ = pa.shape
    lb8 = L // SUBLANES
    kern = functools.partial(_rwkv_kernel, chunk=L)
    return pl.pallas_call(
        kern,
        grid=(B // bb, T // L),
        in_specs=[
            pl.BlockSpec((bb, L, 4 * A_WIDTH), lambda i, j: (i, j, 0)),
            pl.BlockSpec((bb, SUBLANES, 4 * A_WIDTH), lambda i, j: (i, jnp.maximum(j * lb8 - 1, 0), 0)),
            pl.BlockSpec((bb, SUBLANES, 4 * A_WIDTH), lambda i, j: (i, 0, 0)),
            pl.BlockSpec((A_HEADS, HEAD_DIM, bb * HEAD_DIM), lambda i, j: (0, 0, i)),
            pl.BlockSpec((1, 4 * A_WIDTH), lambda i, j: (0, 0)),
            pl.BlockSpec((SUBLANES, A_WIDTH), lambda i, j: (0, 0)),
            pl.BlockSpec((A_WIDTH, 3 * A_WIDTH), lambda i, j: (0, 0)),
        ],
        out_specs=[
            pl.BlockSpec((bb, L, A_WIDTH), lambda i, j: (i, j, 0)),
            pl.BlockSpec((A_HEADS, HEAD_DIM, bb * HEAD_DIM), lambda i, j: (0, 0, i)),
        ],
        out_shape=[
            jax.ShapeDtypeStruct((B, T, A_WIDTH), F32),
            jax.ShapeDtypeStruct((A_HEADS, HEAD_DIM, B * HEAD_DIM), F32),
        ],
        scratch_shapes=[pltpu.VMEM((A_HEADS, HEAD_DIM, bb * HEAD_DIM), F32)],
        compiler_params=_cparams(("arbitrary", "arbitrary")),
        name="rwkv7",
    )(pa, pa, shift8, st0, mu, pvec, lora_w)


def _prep_rwkv(mu, w0, w2, a0, a2, g2, k_k, k_a, r_k, ln_g, ln_b):
    zero = jnp.zeros((A_WIDTH,), F32)
    pvec = jnp.stack([w0, a0, k_k, k_a, r_k.reshape(A_WIDTH), ln_g, ln_b, zero])
    lora = jnp.zeros((A_WIDTH, 3 * A_WIDTH), F32)
    lora = lora.at[0:64, 0:A_WIDTH].set(w2)
    lora = lora.at[64:128, A_WIDTH:2 * A_WIDTH].set(a2)
    lora = lora.at[128:256, 2 * A_WIDTH:].set(g2)
    return mu.reshape(1, 4 * A_WIDTH), pvec, lora


def _state_to_lanes(s):
    B, H = s.shape[:2]
    return jnp.transpose(s, (1, 3, 0, 2)).reshape(H, HEAD_DIM, B * HEAD_DIM)


def _state_from_lanes(st, B):
    H = st.shape[0]
    return jnp.transpose(st.reshape(H, HEAD_DIM, B, HEAD_DIM), (2, 0, 3, 1))


def _mlstm_kernel(pb_ref, halo_ref, cv_ref, pif_ref, gt_ref, ct0_ref, n0_ref, m0_ref, cw_ref, cb_ref, bif_ref,
                  brow_ref, ng_ref, yb_ref, cto_ref, no_ref, mo_ref, ct_ref, nt_ref, m_ref):
    c = pl.program_id(1)
    bb, L, _ = pb_ref.shape
    R = bb * L
    W = B_WIDTH
    neg = -jnp.inf

    @pl.when(c == 0)
    def _():
        ct_ref[...] = ct0_ref[...]
        nt_ref[...] = n0_ref[...]
        m_ref[...] = jnp.broadcast_to(m0_ref[...], m_ref.shape)

    pb = pb_ref[...]
    qk_in = pb[:, :, 0:2 * W]
    pre = jnp.where(c == 0, cv_ref[...], halo_ref[...])
    s1, s2, s3 = _shifted_rows(pre, qk_in, MLSTM_CONV - 1)
    cw = cw_ref[...]
    conv = cb_ref[...] + qk_in * cw[3:4, :] + s1 * cw[2:3, :] + s2 * cw[1:2, :] + s3 * cw[0:1, :]
    qk = _silu(conv).reshape(R, 2 * W)
    q = qk[:, 0:W]
    k = qk[:, W:2 * W] * (HEAD_DIM ** -0.5)
    v = pb[:, :, 2 * W:3 * W].reshape(R, W)
    og = pb[:, :, 3 * W:4 * W].reshape(R, W)

    same, incl, _ = _block_masks(R, L)
    same_f = same.astype(F32)
    incl_f = incl.astype(F32)
    gc = pif_ref[...].reshape(R, LANES) + bif_ref[...]
    lane = _iota((R, LANES), 1)
    lfc = jnp.where((lane >= B_HEADS) & (lane < 2 * B_HEADS), _log_sigmoid(gc), 0.0)
    bcum_c = _mm_sel(incl_f, lfc)
    btot_c = _mm_sel(same_f, lfc)
    gr = gt_ref[...] + brow_ref[...]
    row = _iota((SUBLANES, R), 0)
    lfr = jnp.where(row >= B_HEADS, _log_sigmoid(gr), 0.0)
    bcum_r = _mm_xsel(lfr, incl_f, NT)
    btot_r = _mm_xsel(lfr, same_f)
    m_col = jnp.broadcast_to(m_ref[:, 0:1, :], (bb, L, LANES)).reshape(R, LANES)

    per_seq = bb <= 4
    if not per_seq:
        lmask = _seq_lane_mask(R, L, bb)
        fold = _fold_mat(bb)
        tile = jnp.transpose(fold)
        expand = _expand_mat(R, L, bb)
    sh = int(math.log2(L))
    rl = _iota((R, LANES), 0)
    blockind = (jnp.right_shift(rl, sh) == lane).astype(F32)
    firstind = ((jnp.right_shift(rl, sh) == lane) & ((rl & (L - 1)) == 0)).astype(F32)

    houts, kws, wcs = [], [], []
    cts = [ct_ref[h] for h in range(B_HEADS)]
    nts = [nt_ref[h] for h in range(B_HEADS)]
    m_new_all = jnp.zeros((R, LANES), F32)
    for h in range(B_HEADS):
        hs = slice(h * HEAD_DIM, (h + 1) * HEAD_DIM)
        Q, K, V = q[:, hs], k[:, hs], v[:, hs]
        b_c = bcum_c[:, B_HEADS + h:B_HEADS + h + 1]
        b_r = bcum_r[B_HEADS + h:B_HEADS + h + 1, :]
        i_r = gr[h:h + 1, :]
        i_c = gc[:, h:h + 1]
        m_c = m_col[:, h:h + 1]
        D = jnp.where(incl, b_c - b_r + i_r, neg)
        inter = b_c + m_c
        mt = jnp.maximum(inter, jnp.max(D, axis=1, keepdims=True))
        S = _mm(Q, K, 1, NT) * jnp.exp(D - mt)
        iw = jnp.exp(inter - mt)
        ct_h = cts[h]
        if per_seq:
            QC = jnp.concatenate([_mm(Q[b * L:(b + 1) * L], ct_h[:, b * HEAD_DIM:(b + 1) * HEAD_DIM], 1)
                                  for b in range(bb)], axis=0)
        else:
            QC = _mm_xsel(jnp.where(lmask, _mm(Q, ct_h, 1), 0.0), fold)
        num = _mm(S, V, 1) + iw * QC
        qn = jnp.sum(_mm(Q, nts[h], 3) * blockind, axis=1, keepdims=True)
        den = jnp.sum(S, axis=1, keepdims=True) + iw * qn
        houts.append(num / jnp.maximum(jnp.abs(den), jnp.exp(-mt)))
        bl_c = btot_c[:, B_HEADS + h:B_HEADS + h + 1]
        bl_r = btot_r[B_HEADS + h:B_HEADS + h + 1, :]
        gs_c = bl_c - b_c + i_c
        gmax = jnp.max(jnp.where(same, bl_r - b_r + i_r, neg), axis=1, keepdims=True)
        m_new = jnp.maximum(bl_c + m_c, gmax)
        kws.append(K * jnp.exp(gs_c - m_new))
        wcs.append(jnp.exp(bl_c + m_c - m_new))
        m_new_all = jnp.where(lane == h, m_new, m_new_all)

    kw_t = jnp.transpose(jnp.concatenate(kws, axis=1))
    new_c, new_n = [], []
    for h in range(B_HEADS):
        hs = slice(h * HEAD_DIM, (h + 1) * HEAD_DIM)
        ct_h = cts[h]
        if per_seq:
            new_c.append(jnp.concatenate(
                [wcs[h][b * L:b * L + 1, :] * ct_h[:, b * HEAD_DIM:(b + 1) * HEAD_DIM]
                 + _mm(kw_t[hs, b * L:(b + 1) * L], v[b * L:(b + 1) * L, hs], 1) for b in range(bb)], axis=1))
        else:
            vb = jnp.where(lmask, _mm_xsel(v[:, hs], tile), 0.0)
            wc_row = jnp.sum(wcs[h] * expand, axis=0, keepdims=True)
            new_c.append(wc_row * ct_h + _mm(kw_t[hs, :], vb, 1))
        wc_lane = jnp.sum(wcs[h] * firstind, axis=0, keepdims=True)
        new_n.append(wc_lane * nts[h] + _mm_xsel(kw_t[hs, :], blockind))
    for h in range(B_HEADS):
        ct_ref[h] = new_c[h]
        nt_ref[h] = new_n[h]
    m_ref[...] = m_new_all.reshape(bb, L, LANES)[:, 0:SUBLANES, :]

    hcat = jnp.concatenate(houts, axis=1)
    ones_h = _head_ones(W)
    inv = 1.0 / HEAD_DIM
    mu = _mm_xsel(hcat, ones_h) * inv
    hc = hcat - mu
    var = _mm_xsel(hc * hc, ones_h) * inv
    hn = hc * lax.rsqrt(var + 1e-5) * ng_ref[...]
    yb_ref[...] = (_sigmoid(og) * hn).reshape(bb, L, W)

    @pl.when(c == pl.num_programs(1) - 1)
    def _():
        cto_ref[...] = ct_ref[...]
        no_ref[...] = nt_ref[...]
        mo_ref[...] = m_ref[...]


def _mlstm(pb, conv8, pif, ct0, n0, m0, conv_w8, conv_b, bias_if, norm_g, bb, L):
    B, T, _ = pb.shape
    R = bb * L
    nbi, nch = B // bb, T // L
    lb8 = L // SUBLANES
    g_t = pif[:, :, 0:SUBLANES].reshape(nbi, bb, nch, L, SUBLANES).transpose(0, 2, 4, 1, 3).reshape(nbi, nch, SUBLANES, R)
    bias_row = jnp.broadcast_to(bias_if[0, 0:SUBLANES].reshape(SUBLANES, 1), (SUBLANES, R))
    n_in = jnp.pad(n0.reshape(nbi, bb, B_HEADS, HEAD_DIM).transpose(0, 2, 3, 1), ((0, 0), (0, 0), (0, 0), (0, LANES - bb)))
    m_in = jnp.pad(m0, ((0, 0), (0, LANES - B_HEADS))).reshape(B, 1, LANES)
    yb, ct, nt, mo = pl.pallas_call(
        _mlstm_kernel,
        grid=(nbi, nch),
        in_specs=[
            pl.BlockSpec((bb, L, 4 * B_WIDTH), lambda i, j: (i, j, 0)),
            pl.BlockSpec((bb, SUBLANES, 2 * B_WIDTH), lambda i, j: (i, jnp.maximum(j * lb8 - 1, 0), 0)),
            pl.BlockSpec((bb, SUBLANES, 2 * B_WIDTH), lambda i, j: (i, 0, 0)),
            pl.BlockSpec((bb, L, LANES), lambda i, j: (i, j, 0)),
            pl.BlockSpec((None, None, SUBLANES, R), lambda i, j: (i, j, 0, 0)),
            pl.BlockSpec((B_HEADS, HEAD_DIM, bb * HEAD_DIM), lambda i, j: (0, 0, i)),
            pl.BlockSpec((None, B_HEADS, HEAD_DIM, LANES), lambda i, j: (i, 0, 0, 0)),
            pl.BlockSpec((bb, 1, LANES), lambda i, j: (i, 0, 0)),
            pl.BlockSpec((SUBLANES, 2 * B_WIDTH), lambda i, j: (0, 0)),
            pl.BlockSpec((1, 2 * B_WIDTH), lambda i, j: (0, 0)),
            pl.BlockSpec((1, LANES), lambda i, j: (0, 0)),
            pl.BlockSpec((SUBLANES, R), lambda i, j: (0, 0)),
            pl.BlockSpec((1, B_WIDTH), lambda i, j: (0, 0)),
        ],
        out_specs=[
            pl.BlockSpec((bb, L, B_WIDTH), lambda i, j: (i, j, 0)),
            pl.BlockSpec((B_HEADS, HEAD_DIM, bb * HEAD_DIM), lambda i, j: (0, 0, i)),
            pl.BlockSpec((None, B_HEADS, HEAD_DIM, LANES), lambda i, j: (i, 0, 0, 0)),
            pl.BlockSpec((bb, SUBLANES, LANES), lambda i, j: (i, 0, 0)),
        ],
        out_shape=[
            jax.ShapeDtypeStruct((B, T, B_WIDTH), F32),
            jax.ShapeDtypeStruct((B_HEADS, HEAD_DIM, B * HEAD_DIM), F32),
            jax.ShapeDtypeStruct((nbi, B_HEADS, HEAD_DIM, LANES), F32),
            jax.ShapeDtypeStruct((B, SUBLANES, LANES), F32),
        ],
        scratch_shapes=[
            pltpu.VMEM((B_HEADS, HEAD_DIM, bb * HEAD_DIM), F32),
            pltpu.VMEM((B_HEADS, HEAD_DIM, LANES), F32),
            pltpu.VMEM((bb, SUBLANES, LANES), F32),
        ],
        compiler_params=_cparams(("arbitrary", "arbitrary")),
        name="mlstm",
    )(pb, pb, conv8, pif, g_t, ct0, n_in, m_in, conv_w8, conv_b, bias_if, bias_row, norm_g)
    n_new = nt[:, :, :, 0:bb].transpose(0, 3, 1, 2).reshape(B, B_HEADS, HEAD_DIM)
    return yb, ct, n_new, mo[:, 0, 0:B_HEADS]


def _masked_softmax(s, mask):
    mx = jnp.max(jnp.where(mask, s, NEG_INF), axis=1, keepdims=True)
    e = jnp.where(mask, jnp.exp(s - mx), 0.0)
    return e / jnp.maximum(jnp.sum(e, axis=1, keepdims=True), 1e-30)


def _select_blocks(psum, m_mat, t_col, n_s, blocks_major=False):
    tq = psum.shape[0]
    imp = _mm_xsel(psum, m_mat)
    blk = _iota((tq, LANES), 1)
    cur = jnp.right_shift(t_col, 6)
    valid = (blk * SEL_BLOCK <= t_col)
    forced = (blk == 0) | (blk == cur) | (blk == cur - 1)
    score = jnp.where(valid, imp + jnp.where(forced, FORCE_BONUS, 0.0), NEG_INF)
    score = jnp.where(blk < n_s, score, -jnp.inf)
    n_sel = min(SEL_TOP, n_s)
    if tq == LANES and n_s % SUBLANES == 0:
        sc = jnp.transpose(score)[0:n_s, :]
        idx = _iota((n_s, tq), 0)
        rank = jnp.zeros((n_s, tq), F32)
        for s in range(n_s):
            row = sc[s:s + 1, :]
            ahead = (row > sc) | ((row == sc) & (idx > s))
            rank = rank + jnp.where(ahead, 1.0, 0.0)
        sel_t = jnp.where(rank < n_sel, 1.0, 0.0)
        if n_s < LANES:
            sel_t = jnp.concatenate([sel_t, jnp.zeros((LANES - n_s, tq), F32)], axis=0)
        return sel_t if blocks_major else jnp.transpose(sel_t)
    rank = jnp.zeros((tq, LANES), F32)
    for s in range(n_s):
        col = score[:, s:s + 1]
        ahead = (col > score) | ((col == score) & (blk > s))
        rank = rank + jnp.where(ahead, 1.0, 0.0)
    sel = jnp.where(rank < n_sel, 1.0, 0.0)
    return jnp.transpose(sel) if blocks_major else sel


def _stack_heads(pq, g):
    base = g * C_GROUP * HEAD_DIM
    parts = [pq[:, base + r * HEAD_DIM: base + (r + 1) * HEAD_DIM] for r in range(C_GROUP)]
    return jnp.concatenate(parts, axis=0) * (HEAD_DIM ** -0.5)


def _to_group_lanes(q, g):
    z = jnp.zeros_like(q)
    return jnp.concatenate([q, z] if g == 0 else [z, q], axis=1)


def _compress_kernel(x_ref, pw0_ref, pw1_ref, cw_ref, kc_ref, a_ref, *, rows_per_step):
    T = x_ref.shape[0]
    rs = rows_per_step
    ng = rs // CMP_STRIDE
    pool = (jnp.right_shift(_iota((ng, rs), 1), 4) == _iota((ng, rs), 0)).astype(F32)
    for c in range(T // rs):
        x = x_ref[c * rs:(c + 1) * rs, :]
        xw = jnp.concatenate([x * pw0_ref[...], x * pw1_ref[...]], axis=1)
        a_ref[c * ng:(c + 1) * ng, :] = _mm_sel(pool, xw)
    ngrp = T // CMP_STRIDE
    W = x_ref.shape[1]
    pooled = a_ref[:, 0:W] + pltpu.roll(a_ref[:, W:2 * W], ngrp - 1, axis=0)
    kc_ref[...] = _mm(pooled, cw_ref[...], 3)


def _compress(pcmp, pwt0, pwt1, cmpw_bd):
    B, T, W = pcmp.shape
    rs = pwt0.shape[0]
    ngrp = T // CMP_STRIDE
    kern = functools.partial(_compress_kernel, rows_per_step=rs)
    return pl.pallas_call(
        kern,
        grid=(B,),
        in_specs=[
            pl.BlockSpec((None, T, W), lambda b: (b, 0, 0)),
            pl.BlockSpec((rs, W), lambda b: (0, 0)),
            pl.BlockSpec((rs, W), lambda b: (0, 0)),
            pl.BlockSpec((W, W), lambda b: (0, 0)),
        ],
        out_specs=pl.BlockSpec((None, ngrp, W), lambda b: (b, 0, 0)),
        out_shape=jax.ShapeDtypeStruct((B, ngrp, W), F32),
        scratch_shapes=[pltpu.VMEM((ngrp, 2 * W), F32)],
        compiler_params=_cparams(("arbitrary",)),
        name="nsa_compress",
    )(pcmp, pwt0, pwt1, cmpw_bd)


def _combine_branches(gates, g, o_c, o_s, o_w, tq):
    outs = []
    for r in range(C_GROUP):
        rs = slice(r * tq, (r + 1) * tq)
        j = (g * C_GROUP + r) * 3
        outs.append(gates[:, j:j + 1] * o_c[rs] + gates[:, j + 1:j + 2] * o_s[rs] + gates[:, j + 2:j + 3] * o_w[rs])
    return outs


def _nsa_prompt_kernel(pq_ref, pg_ref, gb_ref, kc_ref, kv_ref, vt_ref, m_ref, et_ref, wb_ref, o_ref, *, n_s):
    i = pl.program_id(1)
    tq = pq_ref.shape[0]
    ngrp = kc_ref.shape[0]
    kt = 4 * tq
    t0 = i * tq
    rows = C_GROUP * tq
    tl = _iota((rows, 1), 0) & (tq - 1)
    t_row = t0 + tl
    t_col = t0 + _iota((tq, 1), 0)
    pq = pq_ref[...]
    gates = _sigmoid(pg_ref[...] + gb_ref[...])
    kc = kc_ref[...]
    n_end = _iota((rows, ngrp), 1) * CMP_STRIDE + (CMP_BLOCK - 1)
    cmask = n_end <= t_row
    n_full = t0 // kt
    tl_lane = _iota((1, rows), 1) & (tq - 1)
    diag_bias = jnp.where(_iota((kt, rows), 0) <= (t0 - n_full * kt) + tl_lane, 0.0, NEG_INF)
    n_tiles = WINDOW // tq + 1
    groups = range(C_KV_HEADS)
    q2s, q2ts, sel_ts, o_cs = [], [], [], []
    for g in groups:
        q = _stack_heads(pq, g)
        q2s.append(_to_group_lanes(q, g).astype(BF16))
        p_c = _masked_softmax(_mm(q, kc[:, g * HEAD_DIM:(g + 1) * HEAD_DIM], 3, NT), cmask)
        o_cs.append(_mm(p_c, kc[:, 2 * HEAD_DIM + g * HEAD_DIM: 2 * HEAD_DIM + (g + 1) * HEAD_DIM], 1))
        psum = p_c[0:tq]
        for r in range(1, C_GROUP):
            psum = psum + p_c[r * tq:(r + 1) * tq]
        sel_ts.append(_select_blocks(psum, m_ref[...], t_col, n_s, blocks_major=True).astype(BF16))
        q2ts.append(jnp.transpose(_to_group_lanes(q, g)).astype(BF16))

    def step(j, carries, extra):
        off = pl.multiple_of(j * kt, kt)
        kk = kv_ref[pl.ds(off, kt), 0:LANES]
        vv_t = vt_ref[:, pl.ds(off, kt)]
        e_t = et_ref[pl.ds(off, kt), :]
        out = []
        for g in groups:
            m, l, acc = carries[g]
            sel_bias = (_dot(e_t, sel_ts[g]) - 1.0) * (-NEG_INF)
            s = _dot(kk, q2ts[g]) + jnp.concatenate([sel_bias] * C_GROUP, axis=1)
            if extra is not None:
                s = s + extra
            m_new = jnp.maximum(m, jnp.max(s, axis=0, keepdims=True))
            p = jnp.exp(s - m_new)
            alpha = jnp.exp(m - m_new)
            l = alpha * l + jnp.sum(p, axis=0, keepdims=True)
            acc = alpha * acc + _dot(vv_t, p.astype(BF16))
            out.append((m_new, l, acc))
        return tuple(out)

    init1 = (jnp.full((1, rows), NEG_INF, F32), jnp.zeros((1, rows), F32), jnp.zeros((LANES, rows), F32))
    carries = lax.fori_loop(0, n_full, lambda j, c: step(j, c, None), tuple(init1 for _ in groups))
    carries = step(n_full, carries, diag_bias)

    pieces = []
    for g in groups:
        gs = slice(g * HEAD_DIM, (g + 1) * HEAD_DIM)
        q2, o_c = q2s[g], o_cs[g]
        _, l, acc = carries[g]
        o_s = jnp.transpose(acc / l)[:, gs]

        s_tiles, vws = [], []
        for cidx in range(n_tiles):
            tile = i - (n_tiles - 1) + cidx
            off = pl.multiple_of(jnp.maximum(tile, 0) * tq, tq)
            kw = kv_ref[pl.ds(off, tq), 2 * LANES:3 * LANES]
            vws.append(kv_ref[pl.ds(off, tq), 3 * LANES:4 * LANES])
            tile_bias = jnp.where(tile >= 0, 0.0, NEG_INF)
            s_tiles.append(_dot(q2, kw, NT) + (wb_ref[:, cidx * tq:(cidx + 1) * tq] + tile_bias))
        s_w = jnp.concatenate(s_tiles, axis=1)
        e_w = jnp.exp(s_w - jnp.max(s_w, axis=1, keepdims=True))
        p_w = e_w / jnp.sum(e_w, axis=1, keepdims=True)
        o_w = _dot(p_w.astype(BF16), jnp.concatenate(vws, axis=0))[:, gs]
        pieces += _combine_branches(gates, g, o_c, o_s, o_w, tq)
    o_ref[...] = jnp.concatenate(pieces, axis=1)


def _nsa_prompt(pq, pg, gate_b, kc, kvb, m_mat, e_mat, tq):
    B, T, _ = pq.shape
    ngrp = kc.shape[1]
    n_s = T // SEL_BLOCK
    n_tiles = WINDOW // tq + 1
    kp = np.arange(n_tiles * tq)[None, :] - (n_tiles - 1) * tq
    tloc = (np.arange(C_GROUP * tq) % tq)[:, None]
    wbias = jnp.asarray(np.where((kp <= tloc) & (kp > tloc - WINDOW), 0.0, NEG_INF).astype(np.float32))
    kern = functools.partial(_nsa_prompt_kernel, n_s=n_s)
    return pl.pallas_call(
        kern,
        grid=(B, T // tq),
        in_specs=[
            pl.BlockSpec((None, tq, C_WIDTH), lambda b, i: (b, i, 0)),
            pl.BlockSpec((None, tq, LANES), lambda b, i: (b, i, 0)),
            pl.BlockSpec((1, LANES), lambda b, i: (0, 0)),
            pl.BlockSpec((None, ngrp, 4 * HEAD_DIM), lambda b, i: (b, 0, 0)),
            pl.BlockSpec((None, T, 4 * LANES), lambda b, i: (b, 0, 0)),
            pl.BlockSpec((None, LANES, T), lambda b, i: (b, 0, 0)),
            pl.BlockSpec((ngrp, LANES), lambda b, i: (0, 0)),
            pl.BlockSpec((T, LANES), lambda b, i: (0, 0)),
            pl.BlockSpec((C_GROUP * tq, n_tiles * tq), lambda b, i: (0, 0)),
        ],
        out_specs=pl.BlockSpec((None, tq, C_WIDTH), lambda b, i: (b, i, 0)),
        out_shape=jax.ShapeDtypeStruct((B, T, C_WIDTH), F32),
        compiler_params=_cparams(("arbitrary", "arbitrary")),
        name="nsa_prompt",
    )(pq, pg, gate_b, kc, kvb, jnp.swapaxes(kvb[:, :, LANES:2 * LANES], 1, 2), m_mat, jnp.transpose(e_mat), wbias)


def _nsa_sample_kernel(pt_ref, *refs, n_pages, nseq, past_len, n_s, n_c):
    for s in range(nseq):
        _nsa_sample_seq(s, refs[s * n_pages:(s + 1) * n_pages], *refs[nseq * n_pages:], past_len=past_len, n_s=n_s, n_c=n_c)


def _nsa_sample_seq(s, pages, pq_ref, pg_ref, gb_ref, pcmp_ref, pslc_ref, pwin_ref, win_ref, pw0_ref, pw1_ref, cw_ref,
                    m_ref, e_ref, o_ref, *, past_len, n_s, n_c):
    n_pages = len(pages)
    tq = pq_ref.shape[1]
    rows = C_GROUP * tq
    W = 4 * HEAD_DIM
    ng = PAGE_SIZE // CMP_STRIDE
    ngrp = m_ref.shape[0]
    pw0 = pw0_ref[...]
    pw1 = pw1_ref[...]

    def tail_t(x):
        return jnp.transpose(jnp.concatenate([x, jnp.zeros((PAGE_SIZE - tq, x.shape[1]), F32)], axis=0))

    xs = [pages[p][0:W, :] for p in range(n_pages)] + [tail_t(pcmp_ref[s])]
    a_acc = jnp.zeros((2 * W, ngrp), F32)
    for p0 in range(0, n_pages + 1, 2):
        grp = xs[p0:p0 + 2]
        xw = jnp.concatenate([jnp.concatenate([x * pw0, x * pw1], axis=0) for x in grp], axis=1)
        kdim = xw.shape[1]
        sel = (jnp.right_shift(_iota((kdim, ngrp), 0), 4) + p0 * ng == _iota((kdim, ngrp), 1)).astype(BF16)
        hi = xw.astype(BF16)
        lo = (xw - hi.astype(F32)).astype(BF16)
        a_acc = a_acc + (_dot(hi, sel) + _dot(lo, sel))
    pooled_t = a_acc[0:W, :] + pltpu.roll(a_acc[W:2 * W, :], ngrp - 1, axis=1)
    kc_t = _mm(cw_ref[...], pooled_t, 3)
    pslc_t = tail_t(pslc_ref[s])
    k_t = jnp.concatenate([pages[p][W:W + LANES, :].astype(BF16) for p in range(n_pages)]
                          + [pslc_t[0:LANES].astype(BF16)], axis=1)
    v_t = jnp.concatenate([pages[p][W + LANES:W + 2 * LANES, :].astype(BF16) for p in range(n_pages)]
                          + [pslc_t[LANES:2 * LANES].astype(BF16)], axis=1)
    nk = k_t.shape[1]
    t_row = past_len + (_iota((rows, 1), 0) & (tq - 1))
    t_col = past_len + _iota((tq, 1), 0)
    pq = pq_ref[s]
    gates = _sigmoid(pg_ref[s] + gb_ref[...])
    n_idx = _iota((rows, ngrp), 1)
    cmask = (n_idx * CMP_STRIDE + (CMP_BLOCK - 1) <= t_row) & (n_idx < n_c)
    kpos = _iota((rows, nk), 1)
    win_t = win_ref[s]
    pwin_t = tail_t(pwin_ref[s])
    nwb = win_t.shape[1]
    kw_t = jnp.concatenate([win_t[0:LANES], pwin_t[0:LANES]], axis=1).astype(BF16)
    vw_t = jnp.concatenate([win_t[LANES:2 * LANES], pwin_t[LANES:2 * LANES]], axis=1).astype(BF16)
    jj = _iota((rows, nwb + PAGE_SIZE), 1)
    tl = _iota((rows, 1), 0) & (tq - 1)
    wmask = (jj > tl + (nwb - WINDOW)) & (jj <= tl + nwb) & (jj < nwb + tq)
    pieces = []
    for g in range(C_KV_HEADS):
        gs = slice(g * HEAD_DIM, (g + 1) * HEAD_DIM)
        q = _stack_heads(pq, g)
        q2 = _to_group_lanes(q, g).astype(BF16)
        p_c = _masked_softmax(_mm(q, kc_t[g * HEAD_DIM:(g + 1) * HEAD_DIM, :], 3), cmask)
        o_c = _mm(p_c, kc_t[2 * HEAD_DIM + g * HEAD_DIM: 2 * HEAD_DIM + (g + 1) * HEAD_DIM, :], 1, NT)
        psum = p_c[0:tq]
        for r in range(1, C_GROUP):
            psum = psum + p_c[r * tq:(r + 1) * tq]
        sel = _select_blocks(psum, m_ref[...], t_col, n_s)
        sel_r = jnp.concatenate([sel] * C_GROUP, axis=0).astype(BF16)
        smask = (_dot(sel_r, e_ref[...]) > 0.5) & (kpos <= t_row)
        p_s = _masked_softmax(_dot(q2, k_t), smask)
        o_s = _dot(p_s.astype(BF16), v_t, NT)[:, gs]
        p_w = _masked_softmax(_dot(q2, kw_t), wmask)
        o_w = _dot(p_w.astype(BF16), vw_t, NT)[:, gs]
        pieces += _combine_branches(gates, g, o_c, o_s, o_w, tq)
    o_ref[s] = jnp.concatenate(pieces, axis=1)


def _nsa_sample(layer, page_table, cache_t, pq, pg, gate_b, pcmp, pslc, pwin, win_t, pwt0, pwt1, cmpw_t, m_mat, e_mat):
    B, T, _ = pq.shape
    nseq = 2 if B % 2 == 0 else 1
    n_pages = page_table.shape[1]
    past_len = n_pages * PAGE_SIZE
    lp = -(-(past_len + T) // SEL_BLOCK) * SEL_BLOCK
    n_s = lp // SEL_BLOCK
    n_c = lp // CMP_STRIDE - CMP_BLOCK // CMP_STRIDE + 1
    ngrp = m_mat.shape[0]
    nk = past_len + PAGE_SIZE
    nwb = win_t.shape[3]
    kern = functools.partial(_nsa_sample_kernel, n_pages=n_pages, nseq=nseq, past_len=past_len, n_s=n_s, n_c=n_c)
    page_specs = [
        pl.BlockSpec((None, None, 4 * LANES, PAGE_SIZE),
                     functools.partial(lambda b, pt, s, p: (layer, pt[b * nseq + s, p], 0, 0), s=s, p=p))
        for s in range(nseq) for p in range(n_pages)
    ]
    row = lambda w: pl.BlockSpec((nseq, T, w), lambda b, pt: (b, 0, 0))
    full = lambda a: pl.BlockSpec(a.shape, lambda b, pt: (0,) * a.ndim)
    grid_spec = pltpu.PrefetchScalarGridSpec(
        num_scalar_prefetch=1,
        grid=(B // nseq,),
        in_specs=page_specs + [
            row(C_WIDTH), row(LANES), full(gate_b), row(4 * HEAD_DIM), row(4 * HEAD_DIM), row(4 * HEAD_DIM),
            pl.BlockSpec((None, nseq, 4 * HEAD_DIM, nwb), lambda b, pt: (layer, b, 0, 0)),
            full(pwt0), full(pwt1), full(cmpw_t), full(m_mat), full(e_mat),
        ],
        out_specs=pl.BlockSpec((nseq, T, C_WIDTH), lambda b, pt: (b, 0, 0)),
    )
    return pl.pallas_call(
        kern,
        grid_spec=grid_spec,
        out_shape=jax.ShapeDtypeStruct((B, T, C_WIDTH), F32),
        compiler_params=_cparams(("arbitrary",)),
        name="nsa_sample",
    )(page_table, *([cache_t] * (nseq * n_pages)), pq, pg, gate_b, pcmp, pslc, pwin, win_t, pwt0, pwt1, cmpw_t, m_mat, e_mat)


def _nsa_consts(lp, n_keys):
    n_str = lp // CMP_STRIDE
    n_c = n_str - CMP_BLOCK // CMP_STRIDE + 1
    n_s = lp // SEL_BLOCK
    c0 = np.arange(n_str)[:, None] * CMP_STRIDE
    s0 = np.arange(LANES)[None, :] * SEL_BLOCK
    m = (c0 < s0 + SEL_BLOCK) & (c0 + CMP_BLOCK > s0) & (np.arange(n_str)[:, None] < n_c) & (np.arange(LANES)[None, :] < n_s)
    e = (np.arange(n_keys)[None, :] // SEL_BLOCK) == np.arange(LANES)[:, None]
    return m.astype(np.float32), e.astype(np.float32)


def _prep_mlstm(conv_w, conv_b, i_b, f_b, norm_g):
    cw8 = jnp.pad(conv_w, ((0, SUBLANES - MLSTM_CONV), (0, 0)))
    bias_if = jnp.pad(jnp.concatenate([i_b, f_b]), (0, LANES - 2 * B_HEADS)).reshape(1, LANES)
    return cw8, conv_b.reshape(1, 2 * B_WIDTH), bias_if, norm_g.reshape(1, B_WIDTH)


_A0, _B0, _Q0, _CMP0, _SLC0, _WIN0, _G0, _IF0, _PEND = 0, 1024, 2048, 2560, 2816, 3072, 3328, 3456, 3584
_PROJ_SPLITS = ((_A0, _B0), (_B0, _Q0), (_Q0, _CMP0), (_CMP0, _SLC0), (_SLC0, _WIN0), (_WIN0, _G0), (_G0, _IF0),
                (_IF0, _PEND))


def _pad_lanes(a, width):
    return jnp.pad(a, [(0, 0)] * (a.ndim - 1) + [(0, width - a.shape[-1])])


def _prep_layer(P, l):
    w_in = P["w_in"][l]
    wa, wb, wc = w_in[:, 0:1024], w_in[:, 1024:2056], w_in[:, 2056:3360]
    w_in_p = jnp.concatenate([
        wa, wb[:, 0:768], wb[:, 776:1032], wc[:, 0:512], wc[:, 512:1280],
        _pad_lanes(wc[:, 1280:1304], LANES), _pad_lanes(wb[:, 768:776], LANES)], axis=1).astype(BF16)
    up = P["ffn_up"][l]
    ffn_up_p = jnp.concatenate([_pad_lanes(up[:, 0:D_FF], D_FF_PAD), _pad_lanes(up[:, D_FF:], D_FF_PAD)], axis=1).astype(BF16)
    cw = P["nsa_cmp_w"][l]
    cmpw_bd = jnp.zeros((4 * HEAD_DIM, 4 * HEAD_DIM), F32)
    for kv in range(2):
        for g in range(C_KV_HEADS):
            o = (kv * C_KV_HEADS + g) * HEAD_DIM
            cmpw_bd = cmpw_bd.at[o:o + HEAD_DIM, o:o + HEAD_DIM].set(cw[kv, g])
    pool = P["nsa_pool_w"][l].reshape(CMP_BLOCK, 4 * HEAD_DIM)
    return dict(
        w_in=w_in_p,
        w_out=P["w_out"][l].astype(BF16),
        ffn_up=ffn_up_p,
        ffn_down=jnp.pad(P["ffn_down"][l], ((0, D_FF_PAD - D_FF), (0, 0))).astype(BF16),
        ffn_cw=jnp.pad(P["ffn_conv_w"][l], ((0, SUBLANES - FFN_CONV), (0, D_FF_PAD - D_FF))),
        ffn_cb=_pad_lanes(P["ffn_conv_b"][l].reshape(1, D_FF), D_FF_PAD),
        ln1=(P["ln_g"][l, 0].reshape(1, D_MODEL), P["ln_b"][l, 0].reshape(1, D_MODEL)),
        ln2=(P["ln_g"][l, 1].reshape(1, D_MODEL), P["ln_b"][l, 1].reshape(1, D_MODEL)),
        rwkv=_prep_rwkv(P["rwkv_mu"][l], P["rwkv_w0"][l], P["rwkv_w2"][l], P["rwkv_a0"][l], P["rwkv_a2"][l],
                        P["rwkv_g2"][l], P["rwkv_k_k"][l], P["rwkv_k_a"][l], P["rwkv_r_k"][l], P["rwkv_ln_g"][l],
                        P["rwkv_ln_b"][l]),
        mlstm=_prep_mlstm(P["mlstm_conv_w"][l], P["mlstm_conv_b"][l], P["mlstm_i_b"][l], P["mlstm_f_b"][l],
                          P["mlstm_norm_g"][l]),
        pool0=pool[0:CMP_STRIDE], pool1=pool[CMP_STRIDE:CMP_BLOCK], cmpw=cmpw_bd,
        gate_b=_pad_lanes(P["nsa_gate_b"][l].reshape(1, 3 * C_HEADS), LANES),
    )


def _rows8(state):
    return jnp.pad(state, ((0, 0), (SUBLANES - state.shape[1], 0), (0, 0)))


def _last_rows(prev, cur, k):
    if cur.shape[1] >= k:
        return cur[:, cur.shape[1] - k:]
    return jnp.concatenate([prev, cur], axis=1)[:, -k:]


def _trunk(x, mod, st, layers, nsa_fn, dense_tile, rec_tile):
    B, T, _ = x.shape
    dbb, dL = dense_tile
    rbb, rL = rec_tile
    new = {k: [] for k in ("nsa_kv", "win_kv", "rwkv", "rwkv_shift", "mlstm_C", "mlstm_n", "mlstm_m", "mlstm_conv", "ffn_conv")}
    for l, Lw in enumerate(layers):
        m = mod[l]
        pa, pb, pq, pcmp, pslc, pwin, pg, pif, kvb = _modmm(
            x, m, 0, 1, Lw["w_in"], _PROJ_SPLITS + ((_SLC0, _G0),), (False,) * 8 + (True,), dbb, dL, "in_proj")
        ya, rw_st = _rwkv(pa, _rows8(st["rwkv_shift"][l][:, None, :]), _state_to_lanes(st["rwkv"][l]), *Lw["rwkv"], rbb, rL)
        yb, c_st, n_st, m_st = _mlstm(pb, _rows8(st["mlstm_conv"][l]), pif, _state_to_lanes(st["mlstm_C"][l]),
                                      st["mlstm_n"][l], st["mlstm_m"][l], *Lw["mlstm"], rbb, rL)
        yc, win_new = nsa_fn(l, Lw, pq, pg, pcmp, pslc, pwin, kvb)
        x = _outproj(ya, yb, yc, x, m, Lw["w_out"], *Lw["ln1"], dbb, dL)
        (u,) = _modmm(x, m, 3, 4, Lw["ffn_up"], ((0, 2 * D_FF_PAD),), (False,), dbb, dL, "ffn_up")
        st8 = _rows8(_pad_lanes(st["ffn_conv"][l], D_FF_PAD))
        x = _ffn_down(u, st8, x, m, Lw["ffn_cw"], Lw["ffn_cb"], Lw["ffn_down"], *Lw["ln2"], dbb, dL)
        new["nsa_kv"].append(jnp.concatenate([pcmp, pslc], axis=-1).reshape(B, T, 4, C_KV_HEADS, HEAD_DIM))
        new["win_kv"].append(win_new)
        new["rwkv"].append(_state_from_lanes(rw_st, B))
        new["rwkv_shift"].append(pa[:, -1])
        new["mlstm_C"].append(_state_from_lanes(c_st, B))
        new["mlstm_n"].append(n_st)
        new["mlstm_m"].append(m_st)
        new["mlstm_conv"].append(_last_rows(st["mlstm_conv"][l], pb[:, :, 0:2 * B_WIDTH], MLSTM_CONV - 1))
        new["ffn_conv"].append(_last_rows(st["ffn_conv"][l], u[:, :, 0:D_FF], FFN_CONV - 1))
    return x, {k: jnp.stack(v) for k, v in new.items()}


def kernel(x_prompt, x_sample, c_prompt, c_sample, cache_nsa_kv, cache_win_kv, state_rwkv, state_rwkv_shift,
           state_mlstm_C, state_mlstm_n, state_mlstm_m, state_mlstm_conv, state_ffn_conv, page_table,
           w_in, w_out, ada_w, ada_b, ln_g, ln_b, rwkv_mu, rwkv_w0, rwkv_w2, rwkv_a0, rwkv_a2, rwkv_g2,
           rwkv_k_k, rwkv_k_a, rwkv_r_k, rwkv_ln_g, rwkv_ln_b, mlstm_conv_w, mlstm_conv_b, mlstm_i_b,
           mlstm_f_b, mlstm_norm_g, nsa_pool_w, nsa_cmp_w, nsa_gate_b, ffn_up, ffn_conv_w, ffn_conv_b, ffn_down):
    P = dict(w_in=w_in, w_out=w_out, ln_g=ln_g, ln_b=ln_b, rwkv_mu=rwkv_mu, rwkv_w0=rwkv_w0, rwkv_w2=rwkv_w2,
             rwkv_a0=rwkv_a0, rwkv_a2=rwkv_a2, rwkv_g2=rwkv_g2, rwkv_k_k=rwkv_k_k, rwkv_k_a=rwkv_k_a,
             rwkv_r_k=rwkv_r_k, rwkv_ln_g=rwkv_ln_g, rwkv_ln_b=rwkv_ln_b, mlstm_conv_w=mlstm_conv_w,
             mlstm_conv_b=mlstm_conv_b, mlstm_i_b=mlstm_i_b, mlstm_f_b=mlstm_f_b, mlstm_norm_g=mlstm_norm_g,
             nsa_pool_w=nsa_pool_w, nsa_cmp_w=nsa_cmp_w, nsa_gate_b=nsa_gate_b, ffn_up=ffn_up,
             ffn_conv_w=ffn_conv_w, ffn_conv_b=ffn_conv_b, ffn_down=ffn_down)
    Bp, Tp, _ = x_prompt.shape
    Bs, Ts, _ = x_sample.shape
    G, dh = C_KV_HEADS, HEAD_DIM
    layers = [_prep_layer(P, l) for l in range(DEPTH)]

    nb = -(-(Bp + Bs) // SUBLANES) * SUBLANES
    c_all = jnp.pad(jnp.concatenate([c_prompt, c_sample], axis=0), ((0, nb - Bp - Bs), (0, 0)))
    mod = _ada(c_all, ada_w, ada_b)
    mod_p = mod[:, 0:Bp].reshape(DEPTH, Bp, 1, 6 * D_MODEL)
    mod_s = mod[:, Bp:Bp + Bs].reshape(DEPTH, Bs, 1, 6 * D_MODEL)

    st_p = dict(
        rwkv=jnp.zeros((DEPTH, Bp, A_HEADS, dh, dh), F32), rwkv_shift=jnp.zeros((DEPTH, Bp, 4 * A_WIDTH), F32),
        mlstm_C=jnp.zeros((DEPTH, Bp, B_HEADS, dh, dh), F32), mlstm_n=jnp.zeros((DEPTH, Bp, B_HEADS, dh), F32),
        mlstm_m=jnp.zeros((DEPTH, Bp, B_HEADS), F32), mlstm_conv=jnp.zeros((DEPTH, Bp, MLSTM_CONV - 1, 2 * B_WIDTH), F32),
        ffn_conv=jnp.zeros((DEPTH, Bp, FFN_CONV - 1, D_FF), F32))
    m_p, e_p = _nsa_consts(Tp, Tp)
    e_p = jnp.asarray(e_p, BF16)
    tq = 128
    rs = 4 * tq

    def nsa_prompt(l, Lw, pq, pg, pcmp, pslc, pwin, kvb):
        kc = _compress(pcmp, jnp.tile(Lw["pool0"], (rs // CMP_STRIDE, 1)), jnp.tile(Lw["pool1"], (rs // CMP_STRIDE, 1)), Lw["cmpw"])
        yc = _nsa_prompt(pq, pg, Lw["gate_b"], kc, kvb, jnp.asarray(m_p), e_p, tq)
        return yc, pwin[:, -min(WINDOW, Tp):].reshape(Bp, min(WINDOW, Tp), 2, G, dh)

    y_prompt, new_p = _trunk(x_prompt, mod_p, st_p, layers, nsa_prompt, (1, 256), (Bp, ROWS // Bp))

    st_s = dict(rwkv=state_rwkv, rwkv_shift=state_rwkv_shift, mlstm_C=state_mlstm_C, mlstm_n=state_mlstm_n,
                mlstm_m=state_mlstm_m, mlstm_conv=state_mlstm_conv, ffn_conv=state_ffn_conv)
    n_pages = page_table.shape[1]
    past_len = n_pages * PAGE_SIZE
    lp = -(-(past_len + Ts) // SEL_BLOCK) * SEL_BLOCK
    ngrp = 2 * LANES
    assert (n_pages + 1) * (PAGE_SIZE // CMP_STRIDE) <= ngrp
    m_s, e_s = _nsa_consts(lp, past_len + PAGE_SIZE)
    m_s = jnp.asarray(np.pad(m_s, ((0, ngrp - m_s.shape[0]), (0, 0))))
    e_s = jnp.asarray(e_s, BF16)
    cache_t = jnp.transpose(cache_nsa_kv, (0, 1, 3, 4, 5, 2)).reshape(DEPTH, cache_nsa_kv.shape[1], 4 * G * dh, PAGE_SIZE)
    nwb = cache_win_kv.shape[2]
    win_t_all = jnp.transpose(cache_win_kv, (0, 1, 3, 4, 5, 2)).reshape(DEPTH, Bs, 2 * G * dh, nwb)
    reps = PAGE_SIZE // CMP_STRIDE

    def nsa_sample(l, Lw, pq, pg, pcmp, pslc, pwin, kvb):
        yc = _nsa_sample(l, page_table, cache_t, pq, pg, Lw["gate_b"], pcmp, pslc, pwin, win_t_all,
                         jnp.tile(Lw["pool0"].T, (1, reps)), jnp.tile(Lw["pool1"].T, (1, reps)),
                         Lw["cmpw"].T, m_s, e_s)
        win_all = jnp.concatenate([cache_win_kv[l], pwin.reshape(Bs, Ts, 2, G, dh)], axis=1)
        keep = min(WINDOW, nwb + Ts)
        return yc, win_all[:, -keep:]

    y_sample, new_s = _trunk(x_sample, mod_s, st_s, layers, nsa_sample, (ROWS // Ts, Ts), (ROWS // Ts, Ts))

    return (y_prompt, y_sample,
            new_p["nsa_kv"], new_s["nsa_kv"], new_p["win_kv"], new_s["win_kv"],
            new_p["rwkv"], new_s["rwkv"], new_p["rwkv_shift"], new_s["rwkv_shift"],
            new_p["mlstm_C"], new_s["mlstm_C"], new_p["mlstm_n"], new_s["mlstm_n"],
            new_p["mlstm_m"], new_s["mlstm_m"], new_p["mlstm_conv"], new_s["mlstm_conv"],
            new_p["ffn_conv"], new_s["ffn_conv"])
```

```python
import functools
import math

import numpy as np
import jax
import jax.numpy as jnp
from jax import lax
from jax.experimental import pallas as pl
from jax.experimental.pallas import tpu as pltpu

F32 = jnp.float32
BF16 = jnp.bfloat16

D_MODEL = 1024
DEPTH = 4
HEAD_DIM = 64
A_WIDTH = 256
B_WIDTH = 256
C_WIDTH = 512
A_HEADS = 4
B_HEADS = 4
C_HEADS = 8
C_KV_HEADS = 2
C_GROUP = 4
PAGE_SIZE = 128
MLSTM_CONV = 4
CMP_BLOCK = 32
CMP_STRIDE = 16
SEL_BLOCK = 64
SEL_TOP = 16
WINDOW = 512
D_FF = 2752
D_FF_PAD = 2816
FFN_CONV = 3
ALPHA = (2 * DEPTH) ** 0.25
FORCE_BONUS = 1e4
NEG_INF = -1e30
LANES = 128
SUBLANES = 8
ROWS = 256
VMEM_LIMIT = 56 * 1024 * 1024

NN = (((1,), (0,)), ((), ()))
NT = (((1,), (1,)), ((), ()))


def _dot(a, b, dn=NN):
    return lax.dot_general(a, b, dn, preferred_element_type=F32)


def _split2(a):
    hi = a.astype(BF16)
    lo = (a - hi.astype(F32)).astype(BF16)
    return hi, lo


def _mm(a, b, passes=1, dn=NN):
    if passes == 1:
        return _dot(a.astype(BF16), b.astype(BF16), dn)
    ah, al = _split2(a)
    bh, bl = _split2(b)
    return _dot(ah, bh, dn) + (_dot(al, bh, dn) + _dot(ah, bl, dn))


def _mm_sel(sel, x, dn=NN):
    s = sel.astype(BF16)
    x1 = x.astype(BF16)
    r1 = x - x1.astype(F32)
    x2 = r1.astype(BF16)
    x3 = (r1 - x2.astype(F32)).astype(BF16)
    return _dot(s, x1, dn) + (_dot(s, x2, dn) + _dot(s, x3, dn))


def _mm_xsel(x, sel, dn=NN):
    s = sel.astype(BF16)
    x1 = x.astype(BF16)
    r1 = x - x1.astype(F32)
    x2 = r1.astype(BF16)
    x3 = (r1 - x2.astype(F32)).astype(BF16)
    return _dot(x1, s, dn) + (_dot(x2, s, dn) + _dot(x3, s, dn))


def _sigmoid(x):
    return 1.0 / (1.0 + jnp.exp(-x))


def _silu(x):
    return x * _sigmoid(x)


def _softplus(x):
    return jnp.maximum(x, 0.0) + jnp.log(1.0 + jnp.exp(-jnp.abs(x)))


def _log_sigmoid(x):
    return -_softplus(-x)


def _iota(shape, axis):
    return lax.broadcasted_iota(jnp.int32, shape, axis)


def _block_masks(rows, chunk):
    sh = int(math.log2(chunk))
    r = _iota((rows, rows), 0)
    s = _iota((rows, rows), 1)
    same = jnp.right_shift(r, sh) == jnp.right_shift(s, sh)
    return same, same & (s <= r), same & (s < r)


def _head_ones(width):
    r = _iota((width, width), 0)
    s = _iota((width, width), 1)
    return (jnp.right_shift(r, 6) == jnp.right_shift(s, 6)).astype(F32)


def _expand_mat(rows, chunk, nseq):
    sh = int(math.log2(chunk))
    r = _iota((rows, nseq * HEAD_DIM), 0)
    c = _iota((rows, nseq * HEAD_DIM), 1)
    return ((jnp.right_shift(c, 6) == jnp.right_shift(r, sh)) & ((r & (chunk - 1)) == 0)).astype(F32)


def _seq_lane_mask(rows, chunk, nseq):
    sh = int(math.log2(chunk))
    r = _iota((rows, nseq * HEAD_DIM), 0)
    c = _iota((rows, nseq * HEAD_DIM), 1)
    return jnp.right_shift(c, 6) == jnp.right_shift(r, sh)


def _fold_mat(nseq):
    r = _iota((nseq * HEAD_DIM, HEAD_DIM), 0)
    c = _iota((nseq * HEAD_DIM, HEAD_DIM), 1)
    return ((r & (HEAD_DIM - 1)) == c).astype(F32)


def _shifted_rows(pre, cur, nshift):
    bb, L, C = cur.shape
    full = jnp.concatenate([pre, cur], axis=1).reshape(bb * (L + SUBLANES), C)
    out = []
    for k in range(1, nshift + 1):
        sh = pltpu.roll(full, k, axis=0).reshape(bb, L + SUBLANES, C)
        out.append(sh[:, SUBLANES:, :])
    return out


def _layernorm(z, g, b):
    mu = jnp.mean(z, axis=-1, keepdims=True)
    zc = z - mu
    var = jnp.mean(zc * zc, axis=-1, keepdims=True)
    return zc * lax.rsqrt(var + 1e-5) * g + b


def _cparams(sem):
    return pltpu.CompilerParams(dimension_semantics=sem, vmem_limit_bytes=VMEM_LIMIT)


def _ada_kernel(c_ref, w_ref, b_ref, o_ref):
    c = c_ref[...]
    o_ref[...] = _mm(_silu(c), w_ref[...], 3) + b_ref[...]


def _ada(c_all, ada_w, ada_b):
    nb = c_all.shape[0]
    tn = 1536
    return pl.pallas_call(
        _ada_kernel,
        grid=(DEPTH, 6 * D_MODEL // tn),
        in_specs=[
            pl.BlockSpec((nb, D_MODEL), lambda l, n: (0, 0)),
            pl.BlockSpec((None, D_MODEL, tn), lambda l, n: (l, 0, n)),
            pl.BlockSpec((None, 1, tn), lambda l, n: (l, 0, n)),
        ],
        out_specs=pl.BlockSpec((None, nb, tn), lambda l, n: (l, 0, n)),
        out_shape=jax.ShapeDtypeStruct((DEPTH, nb, 6 * D_MODEL), F32),
        compiler_params=_cparams(("arbitrary", "arbitrary")),
        name="ada_mod",
    )(c_all, ada_w, ada_b.reshape(DEPTH, 1, 6 * D_MODEL))


def _modmm_kernel(x_ref, sh_ref, sc_ref, w_ref, *o_refs, splits, bf16_outs):
    x = x_ref[...]
    bb, L, D = x.shape
    h = (x * (1.0 + sc_ref[...]) + sh_ref[...]).reshape(bb * L, D).astype(BF16)
    o = jnp.dot(h, w_ref[...], preferred_element_type=F32)
    for (a, b), o_ref, as_bf16 in zip(splits, o_refs, bf16_outs):
        piece = o[:, a:b].reshape(bb, L, b - a)
        o_ref[...] = piece.astype(BF16) if as_bf16 else piece


def _modmm(x, mod, sh_col, sc_col, w, splits, bf16_outs, bb, L, name):
    B, T, D = x.shape
    N = w.shape[1]
    kern = functools.partial(_modmm_kernel, splits=splits, bf16_outs=bf16_outs)
    return pl.pallas_call(
        kern,
        grid=(B // bb, T // L),
        in_specs=[
            pl.BlockSpec((bb, L, D), lambda i, j: (i, j, 0)),
            pl.BlockSpec((bb, 1, D), lambda i, j: (i, 0, sh_col)),
            pl.BlockSpec((bb, 1, D), lambda i, j: (i, 0, sc_col)),
            pl.BlockSpec((D, N), lambda i, j: (0, 0)),
        ],
        out_specs=[pl.BlockSpec((bb, L, b - a), lambda i, j: (i, j, 0)) for a, b in splits],
        out_shape=[jax.ShapeDtypeStruct((B, T, b - a), BF16 if q else F32) for (a, b), q in zip(splits, bf16_outs)],
        compiler_params=_cparams(("arbitrary", "arbitrary")),
        name=name,
    )(x, mod, mod, w)


def _outproj_kernel(ya_ref, yb_ref, yc_ref, x_ref, g_ref, w_ref, lg_ref, lb_ref, o_ref):
    x = x_ref[...]
    bb, L, D = x.shape
    rows = bb * L
    ya = ya_ref[...].reshape(rows, A_WIDTH).astype(BF16)
    yb = yb_ref[...].reshape(rows, B_WIDTH).astype(BF16)
    yc = yc_ref[...].reshape(rows, C_WIDTH).astype(BF16)
    y = (jnp.dot(ya, w_ref[0:A_WIDTH, :], preferred_element_type=F32)
         + jnp.dot(yb, w_ref[A_WIDTH:A_WIDTH + B_WIDTH, :], preferred_element_type=F32)
         + jnp.dot(yc, w_ref[A_WIDTH + B_WIDTH:, :], preferred_element_type=F32))
    z = ALPHA * x + (1.0 + g_ref[...]) * y.reshape(bb, L, D)
    o_ref[...] = _layernorm(z, lg_ref[...], lb_ref[...])


def _outproj(ya, yb, yc, x, mod, w_out, ln_g, ln_b, bb, L):
    B, T, D = x.shape
    blk = lambda w: pl.BlockSpec((bb, L, w), lambda i, j: (i, j, 0))
    return pl.pallas_call(
        _outproj_kernel,
        grid=(B // bb, T // L),
        in_specs=[
            blk(A_WIDTH), blk(B_WIDTH), blk(C_WIDTH), blk(D),
            pl.BlockSpec((bb, 1, D), lambda i, j: (i, 0, 2)),
            pl.BlockSpec((D, D), lambda i, j: (0, 0)),
            pl.BlockSpec((1, D), lambda i, j: (0, 0)),
            pl.BlockSpec((1, D), lambda i, j: (0, 0)),
        ],
        out_specs=blk(D),
        out_shape=jax.ShapeDtypeStruct((B, T, D), F32),
        compiler_params=_cparams(("arbitrary", "arbitrary")),
        name="outproj_ln",
    )(ya, yb, yc, x, mod, w_out, ln_g, ln_b)


def _ffn_down_kernel(ug_ref, uv_ref, halo_ref, st_ref, x_ref, g_ref, cw_ref, cb_ref, w_ref, lg_ref, lb_ref, o_ref):
    ug = ug_ref[...]
    bb, L, N = ug.shape
    first = pl.program_id(1) == 0
    pre = jnp.where(first, st_ref[...], halo_ref[...])
    u1, u2 = _shifted_rows(pre, ug, FFN_CONV - 1)
    cw = cw_ref[...]
    conv = cb_ref[...] + ug * cw[2:3, :] + u1 * cw[1:2, :] + u2 * cw[0:1, :]
    a = (_silu(conv) * uv_ref[...]).reshape(bb * L, N).astype(BF16)
    y = jnp.dot(a, w_ref[...], preferred_element_type=F32)
    x = x_ref[...]
    z = ALPHA * x + (1.0 + g_ref[...]) * y.reshape(x.shape)
    o_ref[...] = _layernorm(z, lg_ref[...], lb_ref[...])


def _ffn_down(u, st8, x, mod, conv_w, conv_b, w_down, ln_g, ln_b, bb, L):
    B, T, D = x.shape
    N = D_FF_PAD
    lb8 = L // SUBLANES
    return pl.pallas_call(
        _ffn_down_kernel,
        grid=(B // bb, T // L),
        in_specs=[
            pl.BlockSpec((bb, L, N), lambda i, j: (i, j, 0)),
            pl.BlockSpec((bb, L, N), lambda i, j: (i, j, 1)),
            pl.BlockSpec((bb, SUBLANES, N), lambda i, j: (i, jnp.maximum(j * lb8 - 1, 0), 0)),
            pl.BlockSpec((bb, SUBLANES, N), lambda i, j: (i, 0, 0)),
            pl.BlockSpec((bb, L, D), lambda i, j: (i, j, 0)),
            pl.BlockSpec((bb, 1, D), lambda i, j: (i, 0, 5)),
            pl.BlockSpec((SUBLANES, N), lambda i, j: (0, 0)),
            pl.BlockSpec((1, N), lambda i, j: (0, 0)),
            pl.BlockSpec((N, D), lambda i, j: (0, 0)),
            pl.BlockSpec((1, D), lambda i, j: (0, 0)),
            pl.BlockSpec((1, D), lambda i, j: (0, 0)),
        ],
        out_specs=pl.BlockSpec((bb, L, D), lambda i, j: (i, j, 0)),
        out_shape=jax.ShapeDtypeStruct((B, T, D), F32),
        compiler_params=_cparams(("arbitrary", "arbitrary")),
        name="ffn_down_ln",
    )(u, u, u, st8, x, mod, conv_w, conv_b, w_down, ln_g, ln_b)


def _unit_lower_inverses(n_mats, chunk):
    rows = n_mats[0].shape[0]
    eye = (_iota((rows, rows), 0) == _iota((rows, rows), 1)).astype(F32)
    ps = [n.astype(BF16) for n in n_mats]
    ts = [eye + n for n in n_mats]
    for _ in range(int(math.log2(chunk)) - 1):
        ps = [_dot(p, p).astype(BF16) for p in ps]
        ts = [t + _dot(p, t.astype(BF16)) for p, t in zip(ps, ts)]
    resids = [(eye - t) + _mm(n, t, 3) for n, t in zip(n_mats, ts)]
    return [t + _dot(t.astype(BF16), r.astype(BF16)) for t, r in zip(ts, resids)]


def _rwkv_kernel(pa_ref, halo_ref, sh_ref, st0_ref, mu_ref, pv_ref, lw_ref, ya_ref, sto_ref, st_ref, *, chunk):
    c = pl.program_id(1)
    bb, L, _ = pa_ref.shape
    R = bb * L
    W = A_WIDTH

    @pl.when(c == 0)
    def _():
        st_ref[...] = st0_ref[...]

    pa = pa_ref[...]
    pre = jnp.where(c == 0, sh_ref[...], halo_ref[...])
    (prev,) = _shifted_rows(pre, pa, 1)
    x = pa.reshape(R, 4 * W)
    pm = x + (prev.reshape(R, 4 * W) - x) * mu_ref[...]
    r = pm[:, 0:W]
    k = pm[:, W:2 * W]
    v = pm[:, 2 * W:3 * W]
    lo = pm[:, 3 * W:4 * W]
    lane = _iota((R, W), 1)
    z = jnp.where(lane < 64, jnp.tanh(lo), jnp.where(lane < 128, lo, _sigmoid(lo)))
    lora = _mm(z, lw_ref[...], 3)
    pv = pv_ref[...]
    w0, a0, k_k, k_a, r_k, ln_g, ln_b = (pv[i:i + 1, :] for i in range(7))
    w = -_softplus(-(w0 + lora[:, 0:W])) - 0.5
    a = _sigmoid(a0 + lora[:, W:2 * W])
    g = lora[:, 2 * W:3 * W]
    ones_h = _head_ones(W)
    kk = k * k_k
    kk = kk / jnp.maximum(jnp.sqrt(_mm_xsel(kk * kk, ones_h)), 1e-12)
    k2 = k * (1.0 + (a - 1.0) * k_a)
    lw = -jnp.exp(w)
    same, incl, strict = _block_masks(R, L)
    cum = _mm_sel(incl.astype(F32), lw)
    tot = _mm_sel(same.astype(F32), lw)
    e_neg = jnp.exp(-cum)
    e_rem = jnp.exp(tot - cum)
    kb = kk * a
    a_t = -kk * jnp.exp(cum - lw)
    b_t = kb * e_neg
    k_t = k2 * e_neg
    r_t = r * jnp.exp(cum)
    bh_t = jnp.transpose(kb * e_rem)
    kh_t = jnp.transpose(k2 * e_rem)
    gam_t = jnp.transpose(jnp.exp(tot))
    per_seq = bb <= 4
    if not per_seq:
        expand = _expand_mat(R, L, bb)
        lmask = _seq_lane_mask(R, L, bb)
        lmask2 = jnp.concatenate([lmask, lmask], axis=0)
        fold = _fold_mat(bb)
        tile = jnp.transpose(fold)
    heads = range(A_HEADS)
    hsl = [slice(h * HEAD_DIM, (h + 1) * HEAD_DIM) for h in heads]
    states = [st_ref[h] for h in heads]
    As, Bs, Ks, Rs, Vs = ([x[:, hs] for hs in hsl] for x in (a_t, b_t, k_t, r_t, v))
    m_abs = [jnp.where(strict, _mm(As[h], Bs[h], 3, NT), 0.0) for h in heads]
    t_invs = _unit_lower_inverses(m_abs, L)
    m_aks = [jnp.where(strict, _mm(As[h], Ks[h], 1, NT), 0.0) for h in heads]
    g_rs = [_mm(Rs[h], jnp.concatenate([Bs[h], Ks[h]], axis=0), 1, NT) for h in heads]
    m_rbks = [jnp.concatenate([jnp.where(incl, g[:, 0:R], 0.0), jnp.where(incl, g[:, R:2 * R], 0.0)], axis=1) for g in g_rs]
    if per_seq:
        pss = [[_mm(jnp.concatenate([As[h][b * L:(b + 1) * L], Rs[h][b * L:(b + 1) * L]], axis=0),
                    states[h][:, b * HEAD_DIM:(b + 1) * HEAD_DIM], 1) for b in range(bb)] for h in heads]
        ps_as = [jnp.concatenate([p[0:L] for p in ps], axis=0) for ps in pss]
        ps_rs = [jnp.concatenate([p[L:2 * L] for p in ps], axis=0) for ps in pss]
    else:
        pss = [_mm_xsel(jnp.where(lmask2, _mm(jnp.concatenate([As[h], Rs[h]], axis=0), states[h], 1), 0.0), fold) for h in heads]
        ps_as = [p[0:R] for p in pss]
        ps_rs = [p[R:2 * R] for p in pss]
    Ys = [ps_as[h] + _mm(m_aks[h], Vs[h], 1) for h in heads]
    Us = [_mm(t_invs[h], Ys[h], 3) for h in heads]
    UVs = [jnp.concatenate([Us[h], Vs[h]], axis=0) for h in heads]
    outs = [ps_rs[h] + _mm(m_rbks[h], UVs[h], 1) for h in heads]
    new_states = []
    for h in heads:
        hs = hsl[h]
        if per_seq:
            new = []
            for b in range(bb):
                ts = slice(b * L, (b + 1) * L)
                lhs = jnp.concatenate([bh_t[hs, ts], kh_t[hs, ts]], axis=1)
                uv_b = jnp.concatenate([Us[h][ts], Vs[h][ts]], axis=0)
                new.append(gam_t[hs, b * L:b * L + 1] * states[h][:, b * HEAD_DIM:(b + 1) * HEAD_DIM] + _mm(lhs, uv_b, 1))
            new_states.append(jnp.concatenate(new, axis=1))
        else:
            UVb = jnp.where(lmask2, _mm_xsel(UVs[h], tile), 0.0)
            lhs = jnp.concatenate([bh_t[hs, :], kh_t[hs, :]], axis=1)
            gam = _mm_xsel(gam_t[hs, :], expand)
            new_states.append(gam * states[h] + _mm(lhs, UVb, 1))
    for h in range(A_HEADS):
        st_ref[h] = new_states[h]
    o = jnp.concatenate(outs, axis=1)
    inv = 1.0 / HEAD_DIM
    mu = _mm_xsel(o, ones_h) * inv
    oc = o - mu
    var = _mm_xsel(oc * oc, ones_h) * inv
    y = oc * lax.rsqrt(var + 64e-5) * ln_g + ln_b
    y = y + _mm_xsel(r * k2 * r_k, ones_h) * v
    ya_ref[...] = (y * g).reshape(bb, L, W)

    @pl.when(c == pl.num_programs(1) - 1)
    def _():
        sto_ref[...] = st_ref[...]


def _rwkv(pa, shift8, st0, mu, pvec, lora_w, bb, L):
    B, T, _ = pa.shape
    lb8 = L // SUBLANES
    kern = functools.partial(_rwkv_kernel, chunk=L)
    return pl.pallas_call(
        kern,
        grid=(B // bb, T // L),
        in_specs=[
            pl.BlockSpec((bb, L, 4 * A_WIDTH), lambda i, j: (i, j, 0)),
            pl.BlockSpec((bb, SUBLANES, 4 * A_WIDTH), lambda i, j: (i, jnp.maximum(j * lb8 - 1, 0), 0)),
            pl.BlockSpec((bb, SUBLANES, 4 * A_WIDTH), lambda i, j: (i, 0, 0)),
            pl.BlockSpec((A_HEADS, HEAD_DIM, bb * HEAD_DIM), lambda i, j: (0, 0, i)),
            pl.BlockSpec((1, 4 * A_WIDTH), lambda i, j: (0, 0)),
            pl.BlockSpec((SUBLANES, A_WIDTH), lambda i, j: (0, 0)),
            pl.BlockSpec((A_WIDTH, 3 * A_WIDTH), lambda i, j: (0, 0)),
        ],
        out_specs=[
            pl.BlockSpec((bb, L, A_WIDTH), lambda i, j: (i, j, 0)),
            pl.BlockSpec((A_HEADS, HEAD_DIM, bb * HEAD_DIM), lambda i, j: (0, 0, i)),
        ],
        out_shape=[
            jax.ShapeDtypeStruct((B, T, A_WIDTH), F32),
            jax.ShapeDtypeStruct((A_HEADS, HEAD_DIM, B * HEAD_DIM), F32),
        ],
        scratch_shapes=[pltpu.VMEM((A_HEADS, HEAD_DIM, bb * HEAD_DIM), F32)],
        compiler_params=_cparams(("arbitrary", "arbitrary")),
        name="rwkv7",
    )(pa, pa, shift8, st0, mu, pvec, lora_w)


def _prep_rwkv(mu, w0, w2, a0, a2, g2, k_k, k_a, r_k, ln_g, ln_b):
    zero = jnp.zeros((A_WIDTH,), F32)
    pvec = jnp.stack([w0, a0, k_k, k_a, r_k.reshape(A_WIDTH), ln_g, ln_b, zero])
    lora = jnp.zeros((A_WIDTH, 3 * A_WIDTH), F32)
    lora = lora.at[0:64, 0:A_WIDTH].set(w2)
    lora = lora.at[64:128, A_WIDTH:2 * A_WIDTH].set(a2)
    lora = lora.at[128:256, 2 * A_WIDTH:].set(g2)
    return mu.reshape(1, 4 * A_WIDTH), pvec, lora


def _state_to_lanes(s):
    B, H = s.shape[:2]
    return jnp.transpose(s, (1, 3, 0, 2)).reshape(H, HEAD_DIM, B * HEAD_DIM)


def _state_from_lanes(st, B):
    H = st.shape[0]
    return jnp.transpose(st.reshape(H, HEAD_DIM, B, HEAD_DIM), (2, 0, 3, 1))


def _mlstm_kernel(pb_ref, halo_ref, cv_ref, pif_ref, gt_ref, ct0_ref, n0_ref, m0_ref, cw_ref, cb_ref, bif_ref,
                  brow_ref, ng_ref, yb_ref, cto_ref, no_ref, mo_ref, ct_ref, nt_ref, m_ref):
    c = pl.program_id(1)
    bb, L, _ = pb_ref.shape
    R = bb * L
    W = B_WIDTH
    neg = -jnp.inf

    @pl.when(c == 0)
    def _():
        ct_ref[...] = ct0_ref[...]
        nt_ref[...] = n0_ref[...]
        m_ref[...] = jnp.broadcast_to(m0_ref[...], m_ref.shape)

    pb = pb_ref[...]
    qk_in = pb[:, :, 0:2 * W]
    pre = jnp.where(c == 0, cv_ref[...], halo_ref[...])
    s1, s2, s3 = _shifted_rows(pre, qk_in, MLSTM_CONV - 1)
    cw = cw_ref[...]
    conv = cb_ref[...] + qk_in * cw[3:4, :] + s1 * cw[2:3, :] + s2 * cw[1:2, :] + s3 * cw[0:1, :]
    qk = _silu(conv).reshape(R, 2 * W)
    q = qk[:, 0:W]
    k = qk[:, W:2 * W] * (HEAD_DIM ** -0.5)
    v = pb[:, :, 2 * W:3 * W].reshape(R, W)
    og = pb[:, :, 3 * W:4 * W].reshape(R, W)

    same, incl, _ = _block_masks(R, L)
    same_f = same.astype(F32)
    incl_f = incl.astype(F32)
    gc = pif_ref[...].reshape(R, LANES) + bif_ref[...]
    lane = _iota((R, LANES), 1)
    lfc = jnp.where((lane >= B_HEADS) & (lane < 2 * B_HEADS), _log_sigmoid(gc), 0.0)
    bcum_c = _mm_sel(incl_f, lfc)
    btot_c = _mm_sel(same_f, lfc)
    gr = gt_ref[...] + brow_ref[...]
    row = _iota((SUBLANES, R), 0)
    lfr = jnp.where(row >= B_HEADS, _log_sigmoid(gr), 0.0)
    bcum_r = _mm_xsel(lfr, incl_f, NT)
    btot_r = _mm_xsel(lfr, same_f)
    m_col = jnp.broadcast_to(m_ref[:, 0:1, :], (bb, L, LANES)).reshape(R, LANES)

    per_seq = bb <= 4
    if not per_seq:
        lmask = _seq_lane_mask(R, L, bb)
        fold = _fold_mat(bb)
        tile = jnp.transpose(fold)
        expand = _expand_mat(R, L, bb)
    sh = int(math.log2(L))
    rl = _iota((R, LANES), 0)
    blockind = (jnp.right_shift(rl, sh) == lane).astype(F32)
    firstind = ((jnp.right_shift(rl, sh) == lane) & ((rl & (L - 1)) == 0)).astype(F32)

    houts, kws, wcs = [], [], []
    cts = [ct_ref[h] for h in range(B_HEADS)]
    nts = [nt_ref[h] for h in range(B_HEADS)]
    m_new_all = jnp.zeros((R, LANES), F32)
    for h in range(B_HEADS):
        hs = slice(h * HEAD_DIM, (h + 1) * HEAD_DIM)
        Q, K, V = q[:, hs], k[:, hs], v[:, hs]
        b_c = bcum_c[:, B_HEADS + h:B_HEADS + h + 1]
        b_r = bcum_r[B_HEADS + h:B_HEADS + h + 1, :]
        i_r = gr[h:h + 1, :]
        i_c = gc[:, h:h + 1]
        m_c = m_col[:, h:h + 1]
        D = jnp.where(incl, b_c - b_r + i_r, neg)
        inter = b_c + m_c
        mt = jnp.maximum(inter, jnp.max(D, axis=1, keepdims=True))
        S = _mm(Q, K, 1, NT) * jnp.exp(D - mt)
        iw = jnp.exp(inter - mt)
        ct_h = cts[h]
        if per_seq:
            QC = jnp.concatenate([_mm(Q[b * L:(b + 1) * L], ct_h[:, b * HEAD_DIM:(b + 1) * HEAD_DIM], 1)
                                  for b in range(bb)], axis=0)
        else:
            QC = _mm_xsel(jnp.where(lmask, _mm(Q, ct_h, 1), 0.0), fold)
        num = _mm(S, V, 1) + iw * QC
        qn = jnp.sum(_mm(Q, nts[h], 3) * blockind, axis=1, keepdims=True)
        den = jnp.sum(S, axis=1, keepdims=True) + iw * qn
        houts.append(num / jnp.maximum(jnp.abs(den), jnp.exp(-mt)))
        bl_c = btot_c[:, B_HEADS + h:B_HEADS + h + 1]
        bl_r = btot_r[B_HEADS + h:B_HEADS + h + 1, :]
        gs_c = bl_c - b_c + i_c
        gmax = jnp.max(jnp.where(same, bl_r - b_r + i_r, neg), axis=1, keepdims=True)
        m_new = jnp.maximum(bl_c + m_c, gmax)
        kws.append(K * jnp.exp(gs_c - m_new))
        wcs.append(jnp.exp(bl_c + m_c - m_new))
        m_new_all = jnp.where(lane == h, m_new, m_new_all)

    kw_t = jnp.transpose(jnp.concatenate(kws, axis=1))
    new_c, new_n = [], []
    for h in range(B_HEADS):
        hs = slice(h * HEAD_DIM, (h + 1) * HEAD_DIM)
        ct_h = cts[h]
        if per_seq:
            new_c.append(jnp.concatenate(
                [wcs[h][b * L:b * L + 1, :] * ct_h[:, b * HEAD_DIM:(b + 1) * HEAD_DIM]
                 + _mm(kw_t[hs, b * L:(b + 1) * L], v[b * L:(b + 1) * L, hs], 1) for b in range(bb)], axis=1))
        else:
            vb = jnp.where(lmask, _mm_xsel(v[:, hs], tile), 0.0)
            wc_row = jnp.sum(wcs[h] * expand, axis=0, keepdims=True)
            new_c.append(wc_row * ct_h + _mm(kw_t[hs, :], vb, 1))
        wc_lane = jnp.sum(wcs[h] * firstind, axis=0, keepdims=True)
        new_n.append(wc_lane * nts[h] + _mm_xsel(kw_t[hs, :], blockind))
    for h in range(B_HEADS):
        ct_ref[h] = new_c[h]
        nt_ref[h] = new_n[h]
    m_ref[...] = m_new_all.reshape(bb, L, LANES)[:, 0:SUBLANES, :]

    hcat = jnp.concatenate(houts, axis=1)
    ones_h = _head_ones(W)
    inv = 1.0 / HEAD_DIM
    mu = _mm_xsel(hcat, ones_h) * inv
    hc = hcat - mu
    var = _mm_xsel(hc * hc, ones_h) * inv
    hn = hc * lax.rsqrt(var + 1e-5) * ng_ref[...]
    yb_ref[...] = (_sigmoid(og) * hn).reshape(bb, L, W)

    @pl.when(c == pl.num_programs(1) - 1)
    def _():
        cto_ref[...] = ct_ref[...]
        no_ref[...] = nt_ref[...]
        mo_ref[...] = m_ref[...]


def _mlstm(pb, conv8, pif, ct0, n0, m0, conv_w8, conv_b, bias_if, norm_g, bb, L):
    B, T, _ = pb.shape
    R = bb * L
    nbi, nch = B // bb, T // L
    lb8 = L // SUBLANES
    g_t = pif[:, :, 0:SUBLANES].reshape(nbi, bb, nch, L, SUBLANES).transpose(0, 2, 4, 1, 3).reshape(nbi, nch, SUBLANES, R)
    bias_row = jnp.broadcast_to(bias_if[0, 0:SUBLANES].reshape(SUBLANES, 1), (SUBLANES, R))
    n_in = jnp.pad(n0.reshape(nbi, bb, B_HEADS, HEAD_DIM).transpose(0, 2, 3, 1), ((0, 0), (0, 0), (0, 0), (0, LANES - bb)))
    m_in = jnp.pad(m0, ((0, 0), (0, LANES - B_HEADS))).reshape(B, 1, LANES)
    yb, ct, nt, mo = pl.pallas_call(
        _mlstm_kernel,
        grid=(nbi, nch),
        in_specs=[
            pl.BlockSpec((bb, L, 4 * B_WIDTH), lambda i, j: (i, j, 0)),
            pl.BlockSpec((bb, SUBLANES, 2 * B_WIDTH), lambda i, j: (i, jnp.maximum(j * lb8 - 1, 0), 0)),
            pl.BlockSpec((bb, SUBLANES, 2 * B_WIDTH), lambda i, j: (i, 0, 0)),
            pl.BlockSpec((bb, L, LANES), lambda i, j: (i, j, 0)),
            pl.BlockSpec((None, None, SUBLANES, R), lambda i, j: (i, j, 0, 0)),
            pl.BlockSpec((B_HEADS, HEAD_DIM, bb * HEAD_DIM), lambda i, j: (0, 0, i)),
            pl.BlockSpec((None, B_HEADS, HEAD_DIM, LANES), lambda i, j: (i, 0, 0, 0)),
            pl.BlockSpec((bb, 1, LANES), lambda i, j: (i, 0, 0)),
            pl.BlockSpec((SUBLANES, 2 * B_WIDTH), lambda i, j: (0, 0)),
            pl.BlockSpec((1, 2 * B_WIDTH), lambda i, j: (0, 0)),
            pl.BlockSpec((1, LANES), lambda i, j: (0, 0)),
            pl.BlockSpec((SUBLANES, R), lambda i, j: (0, 0)),
            pl.BlockSpec((1, B_WIDTH), lambda i, j: (0, 0)),
        ],
        out_specs=[
            pl.BlockSpec((bb, L, B_WIDTH), lambda i, j: (i, j, 0)),
            pl.BlockSpec((B_HEADS, HEAD_DIM, bb * HEAD_DIM), lambda i, j: (0, 0, i)),
            pl.BlockSpec((None, B_HEADS, HEAD_DIM, LANES), lambda i, j: (i, 0, 0, 0)),
            pl.BlockSpec((bb, SUBLANES, LANES), lambda i, j: (i, 0, 0)),
        ],
        out_shape=[
            jax.ShapeDtypeStruct((B, T, B_WIDTH), F32),
            jax.ShapeDtypeStruct((B_HEADS, HEAD_DIM, B * HEAD_DIM), F32),
            jax.ShapeDtypeStruct((nbi, B_HEADS, HEAD_DIM, LANES), F32),
            jax.ShapeDtypeStruct((B, SUBLANES, LANES), F32),
        ],
        scratch_shapes=[
            pltpu.VMEM((B_HEADS, HEAD_DIM, bb * HEAD_DIM), F32),
            pltpu.VMEM((B_HEADS, HEAD_DIM, LANES), F32),
            pltpu.VMEM((bb, SUBLANES, LANES), F32),
        ],
        compiler_params=_cparams(("arbitrary", "arbitrary")),
        name="mlstm",
    )(pb, pb, conv8, pif, g_t, ct0, n_in, m_in, conv_w8, conv_b, bias_if, bias_row, norm_g)
    n_new = nt[:, :, :, 0:bb].transpose(0, 3, 1, 2).reshape(B, B_HEADS, HEAD_DIM)
    return yb, ct, n_new, mo[:, 0, 0:B_HEADS]


def _masked_softmax(s, mask):
    mx = jnp.max(jnp.where(mask, s, NEG_INF), axis=1, keepdims=True)
    e = jnp.where(mask, jnp.exp(s - mx), 0.0)
    return e / jnp.maximum(jnp.sum(e, axis=1, keepdims=True), 1e-30)


def _select_blocks(psums, m_mat, t_col, n_s, blocks_major=False):
    tq = psums[0].shape[0]
    imps = [_mm_xsel(p, m_mat) for p in psums]
    blk = _iota((tq, LANES), 1)
    cur = jnp.right_shift(t_col, 6)
    valid = (blk * SEL_BLOCK <= t_col)
    bonus = jnp.where((blk == 0) | (blk == cur) | (blk == cur - 1), FORCE_BONUS, 0.0)
    scores = [jnp.where(blk < n_s, jnp.where(valid, imp + bonus, NEG_INF), -jnp.inf) for imp in imps]
    n_sel = min(SEL_TOP, n_s)
    if tq == LANES and n_s % SUBLANES == 0:
        scs = [jnp.transpose(sc)[0:n_s, :] for sc in scores]
        idx = _iota((n_s, tq), 0)
        ranks = [jnp.zeros((n_s, tq), F32) for _ in scs]
        for s in range(n_s):
            for c, sc in enumerate(scs):
                row = sc[s:s + 1, :]
                ranks[c] = ranks[c] + jnp.where((row > sc) | ((row == sc) & (idx > s)), 1.0, 0.0)
        sel_ts = [jnp.where(r < n_sel, 1.0, 0.0) for r in ranks]
        if n_s < LANES:
            sel_ts = [jnp.concatenate([st, jnp.zeros((LANES - n_s, tq), F32)], axis=0) for st in sel_ts]
        return sel_ts if blocks_major else [jnp.transpose(st) for st in sel_ts]
    ranks = [jnp.zeros((tq, LANES), F32) for _ in scores]
    for s in range(n_s):
        for c, sc in enumerate(scores):
            col = sc[:, s:s + 1]
            ranks[c] = ranks[c] + jnp.where((col > sc) | ((col == sc) & (blk > s)), 1.0, 0.0)
    sels = [jnp.where(r < n_sel, 1.0, 0.0) for r in ranks]
    return [jnp.transpose(x) for x in sels] if blocks_major else sels


def _stack_heads(pq, g):
    base = g * C_GROUP * HEAD_DIM
    parts = [pq[:, base + r * HEAD_DIM: base + (r + 1) * HEAD_DIM] for r in range(C_GROUP)]
    return jnp.concatenate(parts, axis=0) * (HEAD_DIM ** -0.5)


def _to_group_lanes(q, g):
    z = jnp.zeros_like(q)
    return jnp.concatenate([q, z] if g == 0 else [z, q], axis=1)


def _compress_kernel(x_ref, pw0_ref, pw1_ref, cw_ref, kc_ref, a_ref, *, rows_per_step):
    T = x_ref.shape[0]
    rs = rows_per_step
    ng = rs // CMP_STRIDE
    pool = (jnp.right_shift(_iota((ng, rs), 1), 4) == _iota((ng, rs), 0)).astype(F32)
    for c in range(T // rs):
        x = x_ref[c * rs:(c + 1) * rs, :]
        xw = jnp.concatenate([x * pw0_ref[...], x * pw1_ref[...]], axis=1)
        a_ref[c * ng:(c + 1) * ng, :] = _mm_sel(pool, xw)
    ngrp = T // CMP_STRIDE
    W = x_ref.shape[1]
    pooled = a_ref[:, 0:W] + pltpu.roll(a_ref[:, W:2 * W], ngrp - 1, axis=0)
    kc_ref[...] = _mm(pooled, cw_ref[...], 3)


def _compress(pcmp, pwt0, pwt1, cmpw_bd):
    B, T, W = pcmp.shape
    rs = pwt0.shape[0]
    ngrp = T // CMP_STRIDE
    kern = functools.partial(_compress_kernel, rows_per_step=rs)
    return pl.pallas_call(
        kern,
        grid=(B,),
        in_specs=[
            pl.BlockSpec((None, T, W), lambda b: (b, 0, 0)),
            pl.BlockSpec((rs, W), lambda b: (0, 0)),
            pl.BlockSpec((rs, W), lambda b: (0, 0)),
            pl.BlockSpec((W, W), lambda b: (0, 0)),
        ],
        out_specs=pl.BlockSpec((None, ngrp, W), lambda b: (b, 0, 0)),
        out_shape=jax.ShapeDtypeStruct((B, ngrp, W), F32),
        scratch_shapes=[pltpu.VMEM((ngrp, 2 * W), F32)],
        compiler_params=_cparams(("arbitrary",)),
        name="nsa_compress",
    )(pcmp, pwt0, pwt1, cmpw_bd)


def _combine_branches(gates, g, o_c, o_s, o_w, tq):
    outs = []
    for r in range(C_GROUP):
        rs = slice(r * tq, (r + 1) * tq)
        j = (g * C_GROUP + r) * 3
        outs.append(gates[:, j:j + 1] * o_c[rs] + gates[:, j + 1:j + 2] * o_s[rs] + gates[:, j + 2:j + 3] * o_w[rs])
    return outs


def _nsa_prompt_kernel(pq_ref, pg_ref, gb_ref, kc_ref, kv_ref, vt_ref, m_ref, et_ref, wb_ref, o_ref, *, n_s):
    i = pl.program_id(1)
    tq = pq_ref.shape[0]
    ngrp = kc_ref.shape[0]
    kt = 4 * tq
    t0 = i * tq
    rows = C_GROUP * tq
    tl = _iota((rows, 1), 0) & (tq - 1)
    t_row = t0 + tl
    t_col = t0 + _iota((tq, 1), 0)
    pq = pq_ref[...]
    gates = _sigmoid(pg_ref[...] + gb_ref[...])
    kc = kc_ref[...]
    n_end = _iota((rows, ngrp), 1) * CMP_STRIDE + (CMP_BLOCK - 1)
    cmask = n_end <= t_row
    n_full = t0 // kt
    tl_lane = _iota((1, rows), 1) & (tq - 1)
    diag_bias = jnp.where(_iota((kt, rows), 0) <= (t0 - n_full * kt) + tl_lane, 0.0, NEG_INF)
    n_tiles = WINDOW // tq + 1
    groups = range(C_KV_HEADS)
    qs = [_stack_heads(pq, g) for g in groups]
    q2fs = [_to_group_lanes(qs[g], g) for g in groups]
    q2s = [x.astype(BF16) for x in q2fs]
    q2ts = [jnp.transpose(x).astype(BF16) for x in q2fs]
    p_cs = [_masked_softmax(_mm(qs[g], kc[:, g * HEAD_DIM:(g + 1) * HEAD_DIM], 3, NT), cmask) for g in groups]
    o_cs = [_mm(p_cs[g], kc[:, 2 * HEAD_DIM + g * HEAD_DIM: 2 * HEAD_DIM + (g + 1) * HEAD_DIM], 1) for g in groups]
    psums = [functools.reduce(lambda a, b: a + b, [p[r * tq:(r + 1) * tq] for r in range(C_GROUP)]) for p in p_cs]
    sel_ts = [x.astype(BF16) for x in _select_blocks(psums, m_ref[...], t_col, n_s, blocks_major=True)]

    def step(j, carries, extra):
        off = pl.multiple_of(j * kt, kt)
        kk = kv_ref[pl.ds(off, kt), 0:LANES]
        vv_t = vt_ref[:, pl.ds(off, kt)]
        e_t = et_ref[pl.ds(off, kt), :]
        biases = [(_dot(e_t, sel_ts[g]) - 1.0) * (-NEG_INF) for g in groups]
        ss = [_dot(kk, q2ts[g]) + jnp.concatenate([biases[g]] * C_GROUP, axis=1) for g in groups]
        if extra is not None:
            ss = [s + extra for s in ss]
        m_news = [jnp.maximum(carries[g][0], jnp.max(ss[g], axis=0, keepdims=True)) for g in groups]
        ps = [jnp.exp(ss[g] - m_news[g]) for g in groups]
        alphas = [jnp.exp(carries[g][0] - m_news[g]) for g in groups]
        ls = [alphas[g] * carries[g][1] + jnp.sum(ps[g], axis=0, keepdims=True) for g in groups]
        accs = [alphas[g] * carries[g][2] + _dot(vv_t, ps[g].astype(BF16)) for g in groups]
        return tuple((m_news[g], ls[g], accs[g]) for g in groups)

    init1 = (jnp.full((1, rows), NEG_INF, F32), jnp.zeros((1, rows), F32), jnp.zeros((LANES, rows), F32))
    carries = lax.fori_loop(0, n_full, lambda j, c: step(j, c, None), tuple(init1 for _ in groups))
    carries = step(n_full, carries, diag_bias)

    gsl = [slice(g * HEAD_DIM, (g + 1) * HEAD_DIM) for g in groups]
    o_ss = [jnp.transpose(carries[g][2] / carries[g][1])[:, gsl[g]] for g in groups]

    kws, vws, wbs = [], [], []
    for cidx in range(n_tiles):
        tile = i - (n_tiles - 1) + cidx
        off = pl.multiple_of(jnp.maximum(tile, 0) * tq, tq)
        kws.append(kv_ref[pl.ds(off, tq), 2 * LANES:3 * LANES])
        vws.append(kv_ref[pl.ds(off, tq), 3 * LANES:4 * LANES])
        wbs.append(wb_ref[:, cidx * tq:(cidx + 1) * tq] + jnp.where(tile >= 0, 0.0, NEG_INF))
    vw = jnp.concatenate(vws, axis=0)
    s_ws = [jnp.concatenate([_dot(q2s[g], kws[c], NT) + wbs[c] for c in range(n_tiles)], axis=1) for g in groups]
    e_ws = [jnp.exp(s - jnp.max(s, axis=1, keepdims=True)) for s in s_ws]
    p_ws = [e / jnp.sum(e, axis=1, keepdims=True) for e in e_ws]
    o_ws = [_dot(p_ws[g].astype(BF16), vw)[:, gsl[g]] for g in groups]
    pieces = []
    for g in groups:
        pieces += _combine_branches(gates, g, o_cs[g], o_ss[g], o_ws[g], tq)
    o_ref[...] = jnp.concatenate(pieces, axis=1)


def _nsa_prompt(pq, pg, gate_b, kc, kvb, m_mat, e_mat, tq):
    B, T, _ = pq.shape
    ngrp = kc.shape[1]
    n_s = T // SEL_BLOCK
    n_tiles = WINDOW // tq + 1
    kp = np.arange(n_tiles * tq)[None, :] - (n_tiles - 1) * tq
    tloc = (np.arange(C_GROUP * tq) % tq)[:, None]
    wbias = jnp.asarray(np.where((kp <= tloc) & (kp > tloc - WINDOW), 0.0, NEG_INF).astype(np.float32))
    kern = functools.partial(_nsa_prompt_kernel, n_s=n_s)
    return pl.pallas_call(
        kern,
        grid=(B, T // tq),
        in_specs=[
            pl.BlockSpec((None, tq, C_WIDTH), lambda b, i: (b, i, 0)),
            pl.BlockSpec((None, tq, LANES), lambda b, i: (b, i, 0)),
            pl.BlockSpec((1, LANES), lambda b, i: (0, 0)),
            pl.BlockSpec((None, ngrp, 4 * HEAD_DIM), lambda b, i: (b, 0, 0)),
            pl.BlockSpec((None, T, 4 * LANES), lambda b, i: (b, 0, 0)),
            pl.BlockSpec((None, LANES, T), lambda b, i: (b, 0, 0)),
            pl.BlockSpec((ngrp, LANES), lambda b, i: (0, 0)),
            pl.BlockSpec((T, LANES), lambda b, i: (0, 0)),
            pl.BlockSpec((C_GROUP * tq, n_tiles * tq), lambda b, i: (0, 0)),
        ],
        out_specs=pl.BlockSpec((None, tq, C_WIDTH), lambda b, i: (b, i, 0)),
        out_shape=jax.ShapeDtypeStruct((B, T, C_WIDTH), F32),
        compiler_params=_cparams(("arbitrary", "arbitrary")),
        name="nsa_prompt",
    )(pq, pg, gate_b, kc, kvb, jnp.swapaxes(kvb[:, :, LANES:2 * LANES], 1, 2), m_mat, jnp.transpose(e_mat), wbias)


def _nsa_sample_kernel(pt_ref, *refs, n_pages, nseq, past_len, n_s, n_c):
    (pq_ref, pg_ref, gb_ref, pcmp_ref, pslc_ref, pwin_ref, win_ref, pw0_ref, pw1_ref, cw_ref, m_ref, e_ref,
     o_ref) = refs[nseq * n_pages:]
    seqs = range(nseq)
    pages = [refs[s * n_pages:(s + 1) * n_pages] for s in seqs]
    tq = pq_ref.shape[1]
    rows = C_GROUP * tq
    W = 4 * HEAD_DIM
    ng = PAGE_SIZE // CMP_STRIDE
    ngrp = m_ref.shape[0]
    pw0 = pw0_ref[...]
    pw1 = pw1_ref[...]

    def tail_t(x):
        return jnp.transpose(jnp.concatenate([x, jnp.zeros((PAGE_SIZE - tq, x.shape[1]), F32)], axis=0))

    xs = [[pages[s][p][0:W, :] for p in range(n_pages)] + [tail_t(pcmp_ref[s])] for s in seqs]
    a_acc = jnp.zeros((nseq * 2 * W, ngrp), F32)
    for p0 in range(0, n_pages + 1, 2):
        xw = jnp.concatenate(
            [jnp.concatenate([jnp.concatenate([x * pw0, x * pw1], axis=0) for x in xs[s][p0:p0 + 2]], axis=1)
             for s in seqs], axis=0)
        kdim = xw.shape[1]
        sel = (jnp.right_shift(_iota((kdim, ngrp), 0), 4) + p0 * ng == _iota((kdim, ngrp), 1)).astype(BF16)
        hi = xw.astype(BF16)
        lo = (xw - hi.astype(F32)).astype(BF16)
        a_acc = a_acc + (_dot(hi, sel) + _dot(lo, sel))
    pooled_ts = [a_acc[s * 2 * W:s * 2 * W + W, :] + pltpu.roll(a_acc[s * 2 * W + W:(s + 1) * 2 * W, :], ngrp - 1, axis=1)
                 for s in seqs]
    kc_all = _mm(cw_ref[...], jnp.concatenate(pooled_ts, axis=1), 3)
    kc_ts = [kc_all[:, s * ngrp:(s + 1) * ngrp] for s in seqs]
    pslc_ts = [tail_t(pslc_ref[s]) for s in seqs]
    k_ts = [jnp.concatenate([pages[s][p][W:W + LANES, :].astype(BF16) for p in range(n_pages)]
                            + [pslc_ts[s][0:LANES].astype(BF16)], axis=1) for s in seqs]
    v_ts = [jnp.concatenate([pages[s][p][W + LANES:W + 2 * LANES, :].astype(BF16) for p in range(n_pages)]
                            + [pslc_ts[s][LANES:2 * LANES].astype(BF16)], axis=1) for s in seqs]
    nk = k_ts[0].shape[1]
    t_row = past_len + (_iota((rows, 1), 0) & (tq - 1))
    t_col = past_len + _iota((tq, 1), 0)
    n_idx = _iota((rows, ngrp), 1)
    cmask = (n_idx * CMP_STRIDE + (CMP_BLOCK - 1) <= t_row) & (n_idx < n_c)
    kpos = _iota((rows, nk), 1)
    nwb = win_ref.shape[2]
    pwin_ts = [tail_t(pwin_ref[s]) for s in seqs]
    kw_ts = [jnp.concatenate([win_ref[s, 0:LANES, :], pwin_ts[s][0:LANES]], axis=1).astype(BF16) for s in seqs]
    vw_ts = [jnp.concatenate([win_ref[s, LANES:2 * LANES, :], pwin_ts[s][LANES:2 * LANES]], axis=1).astype(BF16)
             for s in seqs]
    jj = _iota((rows, nwb + PAGE_SIZE), 1)
    tl = _iota((rows, 1), 0) & (tq - 1)
    wmask = (jj > tl + (nwb - WINDOW)) & (jj <= tl + nwb) & (jj < nwb + tq)

    chains = [(s, g) for s in seqs for g in range(C_KV_HEADS)]
    gsl = [slice(g * HEAD_DIM, (g + 1) * HEAD_DIM) for g in range(C_KV_HEADS)]
    qs = [_stack_heads(pq_ref[s], g) for s, g in chains]
    q2s = [_to_group_lanes(q, g).astype(BF16) for q, (s, g) in zip(qs, chains)]
    p_cs = [_masked_softmax(_mm(q, kc_ts[s][gsl[g], :], 3), cmask) for q, (s, g) in zip(qs, chains)]
    o_cs = [_mm(p, kc_ts[s][2 * HEAD_DIM + g * HEAD_DIM: 2 * HEAD_DIM + (g + 1) * HEAD_DIM, :], 1, NT)
            for p, (s, g) in zip(p_cs, chains)]
    psums = [functools.reduce(lambda a, b: a + b, [p[r * tq:(r + 1) * tq] for r in range(C_GROUP)]) for p in p_cs]
    sels = _select_blocks(psums, m_ref[...], t_col, n_s)
    e_mat = e_ref[...]
    smasks = [(_dot(jnp.concatenate([sel] * C_GROUP, axis=0).astype(BF16), e_mat) > 0.5) & (kpos <= t_row)
              for sel in sels]
    p_ss = [_masked_softmax(_dot(q2, k_ts[s]), m) for q2, m, (s, g) in zip(q2s, smasks, chains)]
    o_ss = [_dot(p.astype(BF16), v_ts[s], NT)[:, gsl[g]] for p, (s, g) in zip(p_ss, chains)]
    p_ws = [_masked_softmax(_dot(q2, kw_ts[s]), wmask) for q2, (s, g) in zip(q2s, chains)]
    o_ws = [_dot(p.astype(BF16), vw_ts[s], NT)[:, gsl[g]] for p, (s, g) in zip(p_ws, chains)]
    for s in seqs:
        gates = _sigmoid(pg_ref[s] + gb_ref[...])
        pieces = []
        for c, (cs, g) in enumerate(chains):
            if cs == s:
                pieces += _combine_branches(gates, g, o_cs[c], o_ss[c], o_ws[c], tq)
        o_ref[s] = jnp.concatenate(pieces, axis=1)


def _nsa_sample(layer, page_table, cache_t, pq, pg, gate_b, pcmp, pslc, pwin, win_t, pwt0, pwt1, cmpw_t, m_mat, e_mat):
    B, T, _ = pq.shape
    nseq = 2 if B % 2 == 0 else 1
    n_pages = page_table.shape[1]
    past_len = n_pages * PAGE_SIZE
    lp = -(-(past_len + T) // SEL_BLOCK) * SEL_BLOCK
    n_s = lp // SEL_BLOCK
    n_c = lp // CMP_STRIDE - CMP_BLOCK // CMP_STRIDE + 1
    ngrp = m_mat.shape[0]
    nk = past_len + PAGE_SIZE
    nwb = win_t.shape[3]
    kern = functools.partial(_nsa_sample_kernel, n_pages=n_pages, nseq=nseq, past_len=past_len, n_s=n_s, n_c=n_c)
    page_specs = [
        pl.BlockSpec((None, None, 4 * LANES, PAGE_SIZE),
                     functools.partial(lambda b, pt, s, p: (layer, pt[b * nseq + s, p], 0, 0), s=s, p=p))
        for s in range(nseq) for p in range(n_pages)
    ]
    row = lambda w: pl.BlockSpec((nseq, T, w), lambda b, pt: (b, 0, 0))
    full = lambda a: pl.BlockSpec(a.shape, lambda b, pt: (0,) * a.ndim)
    grid_spec = pltpu.PrefetchScalarGridSpec(
        num_scalar_prefetch=1,
        grid=(B // nseq,),
        in_specs=page_specs + [
            row(C_WIDTH), row(LANES), full(gate_b), row(4 * HEAD_DIM), row(4 * HEAD_DIM), row(4 * HEAD_DIM),
            pl.BlockSpec((None, nseq, 4 * HEAD_DIM, nwb), lambda b, pt: (layer, b, 0, 0)),
            full(pwt0), full(pwt1), full(cmpw_t), full(m_mat), full(e_mat),
        ],
        out_specs=pl.BlockSpec((nseq, T, C_WIDTH), lambda b, pt: (b, 0, 0)),
    )
    return pl.pallas_call(
        kern,
        grid_spec=grid_spec,
        out_shape=jax.ShapeDtypeStruct((B, T, C_WIDTH), F32),
        compiler_params=_cparams(("arbitrary",)),
        name="nsa_sample",
    )(page_table, *([cache_t] * (nseq * n_pages)), pq, pg, gate_b, pcmp, pslc, pwin, win_t, pwt0, pwt1, cmpw_t, m_mat, e_mat)


def _nsa_consts(lp, n_keys):
    n_str = lp // CMP_STRIDE
    n_c = n_str - CMP_BLOCK // CMP_STRIDE + 1
    n_s = lp // SEL_BLOCK
    c0 = np.arange(n_str)[:, None] * CMP_STRIDE
    s0 = np.arange(LANES)[None, :] * SEL_BLOCK
    m = (c0 < s0 + SEL_BLOCK) & (c0 + CMP_BLOCK > s0) & (np.arange(n_str)[:, None] < n_c) & (np.arange(LANES)[None, :] < n_s)
    e = (np.arange(n_keys)[None, :] // SEL_BLOCK) == np.arange(LANES)[:, None]
    return m.astype(np.float32), e.astype(np.float32)


def _prep_mlstm(conv_w, conv_b, i_b, f_b, norm_g):
    cw8 = jnp.pad(conv_w, ((0, SUBLANES - MLSTM_CONV), (0, 0)))
    bias_if = jnp.pad(jnp.concatenate([i_b, f_b]), (0, LANES - 2 * B_HEADS)).reshape(1, LANES)
    return cw8, conv_b.reshape(1, 2 * B_WIDTH), bias_if, norm_g.reshape(1, B_WIDTH)


_A0, _B0, _Q0, _CMP0, _SLC0, _WIN0, _G0, _IF0, _PEND = 0, 1024, 2048, 2560, 2816, 3072, 3328, 3456, 3584
_PROJ_SPLITS = ((_A0, _B0), (_B0, _Q0), (_Q0, _CMP0), (_CMP0, _SLC0), (_SLC0, _WIN0), (_WIN0, _G0), (_G0, _IF0),
                (_IF0, _PEND))


def _pad_lanes(a, width):
    return jnp.pad(a, [(0, 0)] * (a.ndim - 1) + [(0, width - a.shape[-1])])


def _prep_layer(P, l):
    w_in = P["w_in"][l]
    wa, wb, wc = w_in[:, 0:1024], w_in[:, 1024:2056], w_in[:, 2056:3360]
    w_in_p = jnp.concatenate([
        wa, wb[:, 0:768], wb[:, 776:1032], wc[:, 0:512], wc[:, 512:1280],
        _pad_lanes(wc[:, 1280:1304], LANES), _pad_lanes(wb[:, 768:776], LANES)], axis=1).astype(BF16)
    up = P["ffn_up"][l]
    ffn_up_p = jnp.concatenate([_pad_lanes(up[:, 0:D_FF], D_FF_PAD), _pad_lanes(up[:, D_FF:], D_FF_PAD)], axis=1).astype(BF16)
    cw = P["nsa_cmp_w"][l]
    cmpw_bd = jnp.zeros((4 * HEAD_DIM, 4 * HEAD_DIM), F32)
    for kv in range(2):
        for g in range(C_KV_HEADS):
            o = (kv * C_KV_HEADS + g) * HEAD_DIM
            cmpw_bd = cmpw_bd.at[o:o + HEAD_DIM, o:o + HEAD_DIM].set(cw[kv, g])
    pool = P["nsa_pool_w"][l].reshape(CMP_BLOCK, 4 * HEAD_DIM)
    return dict(
        w_in=w_in_p,
        w_out=P["w_out"][l].astype(BF16),
        ffn_up=ffn_up_p,
        ffn_down=jnp.pad(P["ffn_down"][l], ((0, D_FF_PAD - D_FF), (0, 0))).astype(BF16),
        ffn_cw=jnp.pad(P["ffn_conv_w"][l], ((0, SUBLANES - FFN_CONV), (0, D_FF_PAD - D_FF))),
        ffn_cb=_pad_lanes(P["ffn_conv_b"][l].reshape(1, D_FF), D_FF_PAD),
        ln1=(P["ln_g"][l, 0].reshape(1, D_MODEL), P["ln_b"][l, 0].reshape(1, D_MODEL)),
        ln2=(P["ln_g"][l, 1].reshape(1, D_MODEL), P["ln_b"][l, 1].reshape(1, D_MODEL)),
        rwkv=_prep_rwkv(P["rwkv_mu"][l], P["rwkv_w0"][l], P["rwkv_w2"][l], P["rwkv_a0"][l], P["rwkv_a2"][l],
                        P["rwkv_g2"][l], P["rwkv_k_k"][l], P["rwkv_k_a"][l], P["rwkv_r_k"][l], P["rwkv_ln_g"][l],
                        P["rwkv_ln_b"][l]),
        mlstm=_prep_mlstm(P["mlstm_conv_w"][l], P["mlstm_conv_b"][l], P["mlstm_i_b"][l], P["mlstm_f_b"][l],
                          P["mlstm_norm_g"][l]),
        pool0=pool[0:CMP_STRIDE], pool1=pool[CMP_STRIDE:CMP_BLOCK], cmpw=cmpw_bd,
        gate_b=_pad_lanes(P["nsa_gate_b"][l].reshape(1, 3 * C_HEADS), LANES),
    )


def _rows8(state):
    return jnp.pad(state, ((0, 0), (SUBLANES - state.shape[1], 0), (0, 0)))


def _last_rows(prev, cur, k):
    if cur.shape[1] >= k:
        return cur[:, cur.shape[1] - k:]
    return jnp.concatenate([prev, cur], axis=1)[:, -k:]


def _trunk(x, mod, st, layers, nsa_fn, dense_tile, rec_tile):
    B, T, _ = x.shape
    dbb, dL = dense_tile
    rbb, rL = rec_tile
    new = {k: [] for k in ("nsa_kv", "win_kv", "rwkv", "rwkv_shift", "mlstm_C", "mlstm_n", "mlstm_m", "mlstm_conv", "ffn_conv")}
    for l, Lw in enumerate(layers):
        m = mod[l]
        pa, pb, pq, pcmp, pslc, pwin, pg, pif, kvb = _modmm(
            x, m, 0, 1, Lw["w_in"], _PROJ_SPLITS + ((_SLC0, _G0),), (False,) * 8 + (True,), dbb, dL, "in_proj")
        ya, rw_st = _rwkv(pa, _rows8(st["rwkv_shift"][l][:, None, :]), _state_to_lanes(st["rwkv"][l]), *Lw["rwkv"], rbb, rL)
        yb, c_st, n_st, m_st = _mlstm(pb, _rows8(st["mlstm_conv"][l]), pif, _state_to_lanes(st["mlstm_C"][l]),
                                      st["mlstm_n"][l], st["mlstm_m"][l], *Lw["mlstm"], rbb, rL)
        yc, win_new = nsa_fn(l, Lw, pq, pg, pcmp, pslc, pwin, kvb)
        x = _outproj(ya, yb, yc, x, m, Lw["w_out"], *Lw["ln1"], dbb, dL)
        (u,) = _modmm(x, m, 3, 4, Lw["ffn_up"], ((0, 2 * D_FF_PAD),), (False,), dbb, dL, "ffn_up")
        st8 = _rows8(_pad_lanes(st["ffn_conv"][l], D_FF_PAD))
        x = _ffn_down(u, st8, x, m, Lw["ffn_cw"], Lw["ffn_cb"], Lw["ffn_down"], *Lw["ln2"], dbb, dL)
        new["nsa_kv"].append(jnp.concatenate([pcmp, pslc], axis=-1).reshape(B, T, 4, C_KV_HEADS, HEAD_DIM))
        new["win_kv"].append(win_new)
        new["rwkv"].append(_state_from_lanes(rw_st, B))
        new["rwkv_shift"].append(pa[:, -1])
        new["mlstm_C"].append(_state_from_lanes(c_st, B))
        new["mlstm_n"].append(n_st)
        new["mlstm_m"].append(m_st)
        new["mlstm_conv"].append(_last_rows(st["mlstm_conv"][l], pb[:, :, 0:2 * B_WIDTH], MLSTM_CONV - 1))
        new["ffn_conv"].append(_last_rows(st["ffn_conv"][l], u[:, :, 0:D_FF], FFN_CONV - 1))
    return x, {k: jnp.stack(v) for k, v in new.items()}


def kernel(x_prompt, x_sample, c_prompt, c_sample, cache_nsa_kv, cache_win_kv, state_rwkv, state_rwkv_shift,
           state_mlstm_C, state_mlstm_n, state_mlstm_m, state_mlstm_conv, state_ffn_conv, page_table,
           w_in, w_out, ada_w, ada_b, ln_g, ln_b, rwkv_mu, rwkv_w0, rwkv_w2, rwkv_a0, rwkv_a2, rwkv_g2,
           rwkv_k_k, rwkv_k_a, rwkv_r_k, rwkv_ln_g, rwkv_ln_b, mlstm_conv_w, mlstm_conv_b, mlstm_i_b,
           mlstm_f_b, mlstm_norm_g, nsa_pool_w, nsa_cmp_w, nsa_gate_b, ffn_up, ffn_conv_w, ffn_conv_b, ffn_down):
    P = dict(w_in=w_in, w_out=w_out, ln_g=ln_g, ln_b=ln_b, rwkv_mu=rwkv_mu, rwkv_w0=rwkv_w0, rwkv_w2=rwkv_w2,
             rwkv_a0=rwkv_a0, rwkv_a2=rwkv_a2, rwkv_g2=rwkv_g2, rwkv_k_k=rwkv_k_k, rwkv_k_a=rwkv_k_a,
             rwkv_r_k=rwkv_r_k, rwkv_ln_g=rwkv_ln_g, rwkv_ln_b=rwkv_ln_b, mlstm_conv_w=mlstm_conv_w,
             mlstm_conv_b=mlstm_conv_b, mlstm_i_b=mlstm_i_b, mlstm_f_b=mlstm_f_b, mlstm_norm_g=mlstm_norm_g,
             nsa_pool_w=nsa_pool_w, nsa_cmp_w=nsa_cmp_w, nsa_gate_b=nsa_gate_b, ffn_up=ffn_up,
             ffn_conv_w=ffn_conv_w, ffn_conv_b=ffn_conv_b, ffn_down=ffn_down)
    Bp, Tp, _ = x_prompt.shape
    Bs, Ts, _ = x_sample.shape
    G, dh = C_KV_HEADS, HEAD_DIM
    layers = [_prep_layer(P, l) for l in range(DEPTH)]

    nb = -(-(Bp + Bs) // SUBLANES) * SUBLANES
    c_all = jnp.pad(jnp.concatenate([c_prompt, c_sample], axis=0), ((0, nb - Bp - Bs), (0, 0)))
    mod = _ada(c_all, ada_w, ada_b)
    mod_p = mod[:, 0:Bp].reshape(DEPTH, Bp, 1, 6 * D_MODEL)
    mod_s = mod[:, Bp:Bp + Bs].reshape(DEPTH, Bs, 1, 6 * D_MODEL)

    st_p = dict(
        rwkv=jnp.zeros((DEPTH, Bp, A_HEADS, dh, dh), F32), rwkv_shift=jnp.zeros((DEPTH, Bp, 4 * A_WIDTH), F32),
        mlstm_C=jnp.zeros((DEPTH, Bp, B_HEADS, dh, dh), F32), mlstm_n=jnp.zeros((DEPTH, Bp, B_HEADS, dh), F32),
        mlstm_m=jnp.zeros((DEPTH, Bp, B_HEADS), F32), mlstm_conv=jnp.zeros((DEPTH, Bp, MLSTM_CONV - 1, 2 * B_WIDTH), F32),
        ffn_conv=jnp.zeros((DEPTH, Bp, FFN_CONV - 1, D_FF), F32))
    m_p, e_p = _nsa_consts(Tp, Tp)
    e_p = jnp.asarray(e_p, BF16)
    tq = 128
    rs = 4 * tq

    def nsa_prompt(l, Lw, pq, pg, pcmp, pslc, pwin, kvb):
        kc = _compress(pcmp, jnp.tile(Lw["pool0"], (rs // CMP_STRIDE, 1)), jnp.tile(Lw["pool1"], (rs // CMP_STRIDE, 1)), Lw["cmpw"])
        yc = _nsa_prompt(pq, pg, Lw["gate_b"], kc, kvb, jnp.asarray(m_p), e_p, tq)
        return yc, pwin[:, -min(WINDOW, Tp):].reshape(Bp, min(WINDOW, Tp), 2, G, dh)

    y_prompt, new_p = _trunk(x_prompt, mod_p, st_p, layers, nsa_prompt, (1, 256), (Bp, ROWS // Bp))

    st_s = dict(rwkv=state_rwkv, rwkv_shift=state_rwkv_shift, mlstm_C=state_mlstm_C, mlstm_n=state_mlstm_n,
                mlstm_m=state_mlstm_m, mlstm_conv=state_mlstm_conv, ffn_conv=state_ffn_conv)
    n_pages = page_table.shape[1]
    past_len = n_pages * PAGE_SIZE
    lp = -(-(past_len + Ts) // SEL_BLOCK) * SEL_BLOCK
    ngrp = 2 * LANES
    assert (n_pages + 1) * (PAGE_SIZE // CMP_STRIDE) <= ngrp
    m_s, e_s = _nsa_consts(lp, past_len + PAGE_SIZE)
    m_s = jnp.asarray(np.pad(m_s, ((0, ngrp - m_s.shape[0]), (0, 0))))
    e_s = jnp.asarray(e_s, BF16)
    cache_t = jnp.transpose(cache_nsa_kv, (0, 1, 3, 4, 5, 2)).reshape(DEPTH, cache_nsa_kv.shape[1], 4 * G * dh, PAGE_SIZE)
    nwb = cache_win_kv.shape[2]
    win_t_all = jnp.transpose(cache_win_kv, (0, 1, 3, 4, 5, 2)).reshape(DEPTH, Bs, 2 * G * dh, nwb)
    reps = PAGE_SIZE // CMP_STRIDE

    def nsa_sample(l, Lw, pq, pg, pcmp, pslc, pwin, kvb):
        yc = _nsa_sample(l, page_table, cache_t, pq, pg, Lw["gate_b"], pcmp, pslc, pwin, win_t_all,
                         jnp.tile(Lw["pool0"].T, (1, reps)), jnp.tile(Lw["pool1"].T, (1, reps)),
                         Lw["cmpw"].T, m_s, e_s)
        win_all = jnp.concatenate([cache_win_kv[l], pwin.reshape(Bs, Ts, 2, G, dh)], axis=1)
        keep = min(WINDOW, nwb + Ts)
        return yc, win_all[:, -keep:]

    y_sample, new_s = _trunk(x_sample, mod_s, st_s, layers, nsa_sample, (ROWS // Ts, Ts), (ROWS // Ts, Ts))

    return (y_prompt, y_sample,
            new_p["nsa_kv"], new_s["nsa_kv"], new_p["win_kv"], new_s["win_kv"],
            new_p["rwkv"], new_s["rwkv"], new_p["rwkv_shift"], new_s["rwkv_shift"],
            new_p["mlstm_C"], new_s["mlstm_C"], new_p["mlstm_n"], new_s["mlstm_n"],
            new_p["mlstm_m"], new_s["mlstm_m"], new_p["mlstm_conv"], new_s["mlstm_conv"],
            new_p["ffn_conv"], new_s["ffn_conv"])
```

```python
import functools
import math

import numpy as np
import jax
import jax.numpy as jnp
from jax import lax
from jax.experimental import pallas as pl
from jax.experimental.pallas import tpu as pltpu

F32 = jnp.float32
BF16 = jnp.bfloat16

D_MODEL = 1024
DEPTH = 4
HEAD_DIM = 64
A_WIDTH = 256
B_WIDTH = 256
C_WIDTH = 512
A_HEADS = 4
B_HEADS = 4
C_HEADS = 8
C_KV_HEADS = 2
C_GROUP = 4
PAGE_SIZE = 128
MLSTM_CONV = 4
CMP_BLOCK = 32
CMP_STRIDE = 16
SEL_BLOCK = 64
SEL_TOP = 16
WINDOW = 512
D_FF = 2752
D_FF_PAD = 2816
FFN_CONV = 3
ALPHA = (2 * DEPTH) ** 0.25
FORCE_BONUS = 1e4
NEG_INF = -1e30
LANES = 128
SUBLANES = 8
ROWS = 256
VMEM_LIMIT = 56 * 1024 * 1024

NN = (((1,), (0,)), ((), ()))
NT = (((1,), (1,)), ((), ()))


def _dot(a, b, dn=NN):
    return lax.dot_general(a, b, dn, preferred_element_type=F32)


def _split2(a):
    hi = a.astype(BF16)
    lo = (a - hi.astype(F32)).astype(BF16)
    return hi, lo


def _mm(a, b, passes=1, dn=NN):
    if passes == 1:
        return _dot(a.astype(BF16), b.astype(BF16), dn)
    ah, al = _split2(a)
    bh, bl = _split2(b)
    return _dot(ah, bh, dn) + (_dot(al, bh, dn) + _dot(ah, bl, dn))


def _mm_sel(sel, x, dn=NN):
    s = sel.astype(BF16)
    x1 = x.astype(BF16)
    r1 = x - x1.astype(F32)
    x2 = r1.astype(BF16)
    x3 = (r1 - x2.astype(F32)).astype(BF16)
    return _dot(s, x1, dn) + (_dot(s, x2, dn) + _dot(s, x3, dn))


def _mm_xsel(x, sel, dn=NN):
    s = sel.astype(BF16)
    x1 = x.astype(BF16)
    r1 = x - x1.astype(F32)
    x2 = r1.astype(BF16)
    x3 = (r1 - x2.astype(F32)).astype(BF16)
    return _dot(x1, s, dn) + (_dot(x2, s, dn) + _dot(x3, s, dn))


def _sigmoid(x):
    return 1.0 / (1.0 + jnp.exp(-x))


def _silu(x):
    return x * _sigmoid(x)


def _softplus(x):
    return jnp.maximum(x, 0.0) + jnp.log(1.0 + jnp.exp(-jnp.abs(x)))


def _log_sigmoid(x):
    return -_softplus(-x)


def _iota(shape, axis):
    return lax.broadcasted_iota(jnp.int32, shape, axis)


def _block_masks(rows, chunk):
    sh = int(math.log2(chunk))
    r = _iota((rows, rows), 0)
    s = _iota((rows, rows), 1)
    same = jnp.right_shift(r, sh) == jnp.right_shift(s, sh)
    return same, same & (s <= r), same & (s < r)


def _head_ones(width):
    r = _iota((width, width), 0)
    s = _iota((width, width), 1)
    return (jnp.right_shift(r, 6) == jnp.right_shift(s, 6)).astype(F32)


def _expand_mat(rows, chunk, nseq):
    sh = int(math.log2(chunk))
    r = _iota((rows, nseq * HEAD_DIM), 0)
    c = _iota((rows, nseq * HEAD_DIM), 1)
    return ((jnp.right_shift(c, 6) == jnp.right_shift(r, sh)) & ((r & (chunk - 1)) == 0)).astype(F32)


def _seq_lane_mask(rows, chunk, nseq):
    sh = int(math.log2(chunk))
    r = _iota((rows, nseq * HEAD_DIM), 0)
    c = _iota((rows, nseq * HEAD_DIM), 1)
    return jnp.right_shift(c, 6) == jnp.right_shift(r, sh)


def _fold_mat(nseq):
    r = _iota((nseq * HEAD_DIM, HEAD_DIM), 0)
    c = _iota((nseq * HEAD_DIM, HEAD_DIM), 1)
    return ((r & (HEAD_DIM - 1)) == c).astype(F32)


def _shifted_rows(pre, cur, nshift):
    bb, L, C = cur.shape
    full = jnp.concatenate([pre, cur], axis=1).reshape(bb * (L + SUBLANES), C)
    out = []
    for k in range(1, nshift + 1):
        sh = pltpu.roll(full, k, axis=0).reshape(bb, L + SUBLANES, C)
        out.append(sh[:, SUBLANES:, :])
    return out


def _layernorm(z, g, b):
    mu = jnp.mean(z, axis=-1, keepdims=True)
    zc = z - mu
    var = jnp.mean(zc * zc, axis=-1, keepdims=True)
    return zc * lax.rsqrt(var + 1e-5) * g + b


def _cparams(sem):
    return pltpu.CompilerParams(dimension_semantics=sem, vmem_limit_bytes=VMEM_LIMIT)


def _ada_kernel(c_ref, w_ref, b_ref, o_ref):
    c = c_ref[...]
    o_ref[...] = _mm(_silu(c), w_ref[...], 3) + b_ref[...]


def _ada(c_all, ada_w, ada_b):
    nb = c_all.shape[0]
    tn = 1536
    return pl.pallas_call(
        _ada_kernel,
        grid=(DEPTH, 6 * D_MODEL // tn),
        in_specs=[
            pl.BlockSpec((nb, D_MODEL), lambda l, n: (0, 0)),
            pl.BlockSpec((None, D_MODEL, tn), lambda l, n: (l, 0, n)),
            pl.BlockSpec((None, 1, tn), lambda l, n: (l, 0, n)),
        ],
        out_specs=pl.BlockSpec((None, nb, tn), lambda l, n: (l, 0, n)),
        out_shape=jax.ShapeDtypeStruct((DEPTH, nb, 6 * D_MODEL), F32),
        compiler_params=_cparams(("arbitrary", "arbitrary")),
        name="ada_mod",
    )(c_all, ada_w, ada_b.reshape(DEPTH, 1, 6 * D_MODEL))


def _modmm_kernel(x_ref, sh_ref, sc_ref, w_ref, *o_refs, splits, bf16_outs):
    x = x_ref[...]
    bb, L, D = x.shape
    h = (x * (1.0 + sc_ref[...]) + sh_ref[...]).reshape(bb * L, D).astype(BF16)
    o = jnp.dot(h, w_ref[...], preferred_element_type=F32)
    for (a, b), o_ref, as_bf16 in zip(splits, o_refs, bf16_outs):
        piece = o[:, a:b].reshape(bb, L, b - a)
        o_ref[...] = piece.astype(BF16) if as_bf16 else piece


def _modmm(x, mod, sh_col, sc_col, w, splits, bf16_outs, bb, L, name):
    B, T, D = x.shape
    N = w.shape[1]
    kern = functools.partial(_modmm_kernel, splits=splits, bf16_outs=bf16_outs)
    return pl.pallas_call(
        kern,
        grid=(B // bb, T // L),
        in_specs=[
            pl.BlockSpec((bb, L, D), lambda i, j: (i, j, 0)),
            pl.BlockSpec((bb, 1, D), lambda i, j: (i, 0, sh_col)),
            pl.BlockSpec((bb, 1, D), lambda i, j: (i, 0, sc_col)),
            pl.BlockSpec((D, N), lambda i, j: (0, 0)),
        ],
        out_specs=[pl.BlockSpec((bb, L, b - a), lambda i, j: (i, j, 0)) for a, b in splits],
        out_shape=[jax.ShapeDtypeStruct((B, T, b - a), BF16 if q else F32) for (a, b), q in zip(splits, bf16_outs)],
        compiler_params=_cparams(("arbitrary", "arbitrary")),
        name=name,
    )(x, mod, mod, w)


def _outproj_kernel(ya_ref, yb_ref, yc_ref, x_ref, g_ref, w_ref, lg_ref, lb_ref, o_ref):
    x = x_ref[...]
    bb, L, D = x.shape
    rows = bb * L
    ya = ya_ref[...].reshape(rows, A_WIDTH).astype(BF16)
    yb = yb_ref[...].reshape(rows, B_WIDTH).astype(BF16)
    yc = yc_ref[...].reshape(rows, C_WIDTH).astype(BF16)
    y = (jnp.dot(ya, w_ref[0:A_WIDTH, :], preferred_element_type=F32)
         + jnp.dot(yb, w_ref[A_WIDTH:A_WIDTH + B_WIDTH, :], preferred_element_type=F32)
         + jnp.dot(yc, w_ref[A_WIDTH + B_WIDTH:, :], preferred_element_type=F32))
    z = ALPHA * x + (1.0 + g_ref[...]) * y.reshape(bb, L, D)
    o_ref[...] = _layernorm(z, lg_ref[...], lb_ref[...])


def _outproj(ya, yb, yc, x, mod, w_out, ln_g, ln_b, bb, L):
    B, T, D = x.shape
    blk = lambda w: pl.BlockSpec((bb, L, w), lambda i, j: (i, j, 0))
    return pl.pallas_call(
        _outproj_kernel,
        grid=(B // bb, T // L),
        in_specs=[
            blk(A_WIDTH), blk(B_WIDTH), blk(C_WIDTH), blk(D),
            pl.BlockSpec((bb, 1, D), lambda i, j: (i, 0, 2)),
            pl.BlockSpec((D, D), lambda i, j: (0, 0)),
            pl.BlockSpec((1, D), lambda i, j: (0, 0)),
            pl.BlockSpec((1, D), lambda i, j: (0, 0)),
        ],
        out_specs=blk(D),
        out_shape=jax.ShapeDtypeStruct((B, T, D), F32),
        compiler_params=_cparams(("arbitrary", "arbitrary")),
        name="outproj_ln",
    )(ya, yb, yc, x, mod, w_out, ln_g, ln_b)


def _ffn_down_kernel(ug_ref, uv_ref, halo_ref, st_ref, x_ref, g_ref, cw_ref, cb_ref, w_ref, lg_ref, lb_ref, o_ref):
    ug = ug_ref[...]
    bb, L, N = ug.shape
    first = pl.program_id(1) == 0
    pre = jnp.where(first, st_ref[...], halo_ref[...])
    u1, u2 = _shifted_rows(pre, ug, FFN_CONV - 1)
    cw = cw_ref[...]
    conv = cb_ref[...] + ug * cw[2:3, :] + u1 * cw[1:2, :] + u2 * cw[0:1, :]
    a = (_silu(conv) * uv_ref[...]).reshape(bb * L, N).astype(BF16)
    y = jnp.dot(a, w_ref[...], preferred_element_type=F32)
    x = x_ref[...]
    z = ALPHA * x + (1.0 + g_ref[...]) * y.reshape(x.shape)
    o_ref[...] = _layernorm(z, lg_ref[...], lb_ref[...])


def _ffn_down(u, st8, x, mod, conv_w, conv_b, w_down, ln_g, ln_b, bb, L):
    B, T, D = x.shape
    N = D_FF_PAD
    lb8 = L // SUBLANES
    return pl.pallas_call(
        _ffn_down_kernel,
        grid=(B // bb, T // L),
        in_specs=[
            pl.BlockSpec((bb, L, N), lambda i, j: (i, j, 0)),
            pl.BlockSpec((bb, L, N), lambda i, j: (i, j, 1)),
            pl.BlockSpec((bb, SUBLANES, N), lambda i, j: (i, jnp.maximum(j * lb8 - 1, 0), 0)),
            pl.BlockSpec((bb, SUBLANES, N), lambda i, j: (i, 0, 0)),
            pl.BlockSpec((bb, L, D), lambda i, j: (i, j, 0)),
            pl.BlockSpec((bb, 1, D), lambda i, j: (i, 0, 5)),
            pl.BlockSpec((SUBLANES, N), lambda i, j: (0, 0)),
            pl.BlockSpec((1, N), lambda i, j: (0, 0)),
            pl.BlockSpec((N, D), lambda i, j: (0, 0)),
            pl.BlockSpec((1, D), lambda i, j: (0, 0)),
            pl.BlockSpec((1, D), lambda i, j: (0, 0)),
        ],
        out_specs=pl.BlockSpec((bb, L, D), lambda i, j: (i, j, 0)),
        out_shape=jax.ShapeDtypeStruct((B, T, D), F32),
        compiler_params=_cparams(("arbitrary", "arbitrary")),
        name="ffn_down_ln",
    )(u, u, u, st8, x, mod, conv_w, conv_b, w_down, ln_g, ln_b)


def _unit_lower_inverses(n_mats, chunk):
    rows = n_mats[0].shape[0]
    eye = (_iota((rows, rows), 0) == _iota((rows, rows), 1)).astype(F32)
    ps = [n.astype(BF16) for n in n_mats]
    ts = [eye + n for n in n_mats]
    for _ in range(int(math.log2(chunk)) - 1):
        ps = [_dot(p, p).astype(BF16) for p in ps]
        ts = [t + _dot(p, t.astype(BF16)) for p, t in zip(ps, ts)]
    resids = [(eye - t) + _mm(n, t, 3) for n, t in zip(n_mats, ts)]
    return [t + _dot(t.astype(BF16), r.astype(BF16)) for t, r in zip(ts, resids)]


def _rwkv_kernel(pa_ref, halo_ref, sh_ref, st0_ref, mu_ref, pv_ref, lw_ref, ya_ref, sto_ref, st_ref, *, chunk):
    c = pl.program_id(1)
    bb, L, _ = pa_ref.shape
    R = bb * L
    W = A_WIDTH

    @pl.when(c == 0)
    def _():
        st_ref[...] = st0_ref[...]

    pa = pa_ref[...]
    pre = jnp.where(c == 0, sh_ref[...], halo_ref[...])
    (prev,) = _shifted_rows(pre, pa, 1)
    x = pa.reshape(R, 4 * W)
    pm = x + (prev.reshape(R, 4 * W) - x) * mu_ref[...]
    r = pm[:, 0:W]
    k = pm[:, W:2 * W]
    v = pm[:, 2 * W:3 * W]
    lo = pm[:, 3 * W:4 * W]
    lane = _iota((R, W), 1)
    z = jnp.where(lane < 64, jnp.tanh(lo), jnp.where(lane < 128, lo, _sigmoid(lo)))
    lora = _mm(z, lw_ref[...], 3)
    pv = pv_ref[...]
    w0, a0, k_k, k_a, r_k, ln_g, ln_b = (pv[i:i + 1, :] for i in range(7))
    w = -_softplus(-(w0 + lora[:, 0:W])) - 0.5
    a = _sigmoid(a0 + lora[:, W:2 * W])
    g = lora[:, 2 * W:3 * W]
    ones_h = _head_ones(W)
    kk = k * k_k
    kk = kk / jnp.maximum(jnp.sqrt(_mm_xsel(kk * kk, ones_h)), 1e-12)
    k2 = k * (1.0 + (a - 1.0) * k_a)
    lw = -jnp.exp(w)
    same, incl, strict = _block_masks(R, L)
    cum = _mm_sel(incl.astype(F32), lw)
    tot = _mm_sel(same.astype(F32), lw)
    e_neg = jnp.exp(-cum)
    e_rem = jnp.exp(tot - cum)
    kb = kk * a
    a_t = -kk * jnp.exp(cum - lw)
    b_t = kb * e_neg
    k_t = k2 * e_neg
    r_t = r * jnp.exp(cum)
    bh_t = jnp.transpose(kb * e_rem)
    kh_t = jnp.transpose(k2 * e_rem)
    gam_t = jnp.transpose(jnp.exp(tot))
    per_seq = bb <= 4
    if not per_seq:
        expand = _expand_mat(R, L, bb)
        lmask = _seq_lane_mask(R, L, bb)
        lmask2 = jnp.concatenate([lmask, lmask], axis=0)
        fold = _fold_mat(bb)
        tile = jnp.transpose(fold)
    heads = range(A_HEADS)
    hsl = [slice(h * HEAD_DIM, (h + 1) * HEAD_DIM) for h in heads]
    states = [st_ref[h] for h in heads]
    As, Bs, Ks, Rs, Vs = ([x[:, hs] for hs in hsl] for x in (a_t, b_t, k_t, r_t, v))
    m_abs = [jnp.where(strict, _mm(As[h], Bs[h], 3, NT), 0.0) for h in heads]
    t_invs = _unit_lower_inverses(m_abs, L)
    m_aks = [jnp.where(strict, _mm(As[h], Ks[h], 1, NT), 0.0) for h in heads]
    g_rs = [_mm(Rs[h], jnp.concatenate([Bs[h], Ks[h]], axis=0), 1, NT) for h in heads]
    m_rbks = [jnp.concatenate([jnp.where(incl, g[:, 0:R], 0.0), jnp.where(incl, g[:, R:2 * R], 0.0)], axis=1) for g in g_rs]
    if per_seq:
        pss = [[_mm(jnp.concatenate([As[h][b * L:(b + 1) * L], Rs[h][b * L:(b + 1) * L]], axis=0),
                    states[h][:, b * HEAD_DIM:(b + 1) * HEAD_DIM], 1) for b in range(bb)] for h in heads]
        ps_as = [jnp.concatenate([p[0:L] for p in ps], axis=0) for ps in pss]
        ps_rs = [jnp.concatenate([p[L:2 * L] for p in ps], axis=0) for ps in pss]
    else:
        pss = [_mm_xsel(jnp.where(lmask2, _mm(jnp.concatenate([As[h], Rs[h]], axis=0), states[h], 1), 0.0), fold) for h in heads]
        ps_as = [p[0:R] for p in pss]
        ps_rs = [p[R:2 * R] for p in pss]
    Ys = [ps_as[h] + _mm(m_aks[h], Vs[h], 1) for h in heads]
    Us = [_mm(t_invs[h], Ys[h], 3) for h in heads]
    UVs = [jnp.concatenate([Us[h], Vs[h]], axis=0) for h in heads]
    outs = [ps_rs[h] + _mm(m_rbks[h], UVs[h], 1) for h in heads]
    new_states = []
    for h in heads:
        hs = hsl[h]
        if per_seq:
            new = []
            for b in range(bb):
                ts = slice(b * L, (b + 1) * L)
                lhs = jnp.concatenate([bh_t[hs, ts], kh_t[hs, ts]], axis=1)
                uv_b = jnp.concatenate([Us[h][ts], Vs[h][ts]], axis=0)
                new.append(gam_t[hs, b * L:b * L + 1] * states[h][:, b * HEAD_DIM:(b + 1) * HEAD_DIM] + _mm(lhs, uv_b, 1))
            new_states.append(jnp.concatenate(new, axis=1))
        else:
            UVb = jnp.where(lmask2, _mm_xsel(UVs[h], tile), 0.0)
            lhs = jnp.concatenate([bh_t[hs, :], kh_t[hs, :]], axis=1)
            gam = _mm_xsel(gam_t[hs, :], expand)
            new_states.append(gam * states[h] + _mm(lhs, UVb, 1))
    for h in range(A_HEADS):
        st_ref[h] = new_states[h]
    o = jnp.concatenate(outs, axis=1)
    inv = 1.0 / HEAD_DIM
    mu = _mm_xsel(o, ones_h) * inv
    oc = o - mu
    var = _mm_xsel(oc * oc, ones_h) * inv
    y = oc * lax.rsqrt(var + 64e-5) * ln_g + ln_b
    y = y + _mm_xsel(r * k2 * r_k, ones_h) * v
    ya_ref[...] = (y * g).reshape(bb, L, W)

    @pl.when(c == pl.num_programs(1) - 1)
    def _():
        sto_ref[...] = st_ref[...]


def _rwkv(pa, shift8, st0, mu, pvec, lora_w, bb, L):
    B, T, _ = pa.shape
    lb8 = L // SUBLANES
    kern = functools.partial(_rwkv_kernel, chunk=L)
    return pl.pallas_call(
        kern,
        grid=(B // bb, T // L),
        in_specs=[
            pl.BlockSpec((bb, L, 4 * A_WIDTH), lambda i, j: (i, j, 0)),
            pl.BlockSpec((bb, SUBLANES, 4 * A_WIDTH), lambda i, j: (i, jnp.maximum(j * lb8 - 1, 0), 0)),
            pl.BlockSpec((bb, SUBLANES, 4 * A_WIDTH), lambda i, j: (i, 0, 0)),
            pl.BlockSpec((A_HEADS, HEAD_DIM, bb * HEAD_DIM), lambda i, j: (0, 0, i)),
            pl.BlockSpec((1, 4 * A_WIDTH), lambda i, j: (0, 0)),
            pl.BlockSpec((SUBLANES, A_WIDTH), lambda i, j: (0, 0)),
            pl.BlockSpec((A_WIDTH, 3 * A_WIDTH), lambda i, j: (0, 0)),
        ],
        out_specs=[
            pl.BlockSpec((bb, L, A_WIDTH), lambda i, j: (i, j, 0)),
            pl.BlockSpec((A_HEADS, HEAD_DIM, bb * HEAD_DIM), lambda i, j: (0, 0, i)),
        ],
        out_shape=[
            jax.ShapeDtypeStruct((B, T, A_WIDTH), F32),
            jax.ShapeDtypeStruct((A_HEADS, HEAD_DIM, B * HEAD_DIM), F32),
        ],
        scratch_shapes=[pltpu.VMEM((A_HEADS, HEAD_DIM, bb * HEAD_DIM), F32)],
        compiler_params=_cparams(("arbitrary", "arbitrary")),
        name="rwkv7",
    )(pa, pa, shift8, st0, mu, pvec, lora_w)


def _prep_rwkv(mu, w0, w2, a0, a2, g2, k_k, k_a, r_k, ln_g, ln_b):
    zero = jnp.zeros((A_WIDTH,), F32)
    pvec = jnp.stack([w0, a0, k_k, k_a, r_k.reshape(A_WIDTH), ln_g, ln_b, zero])
    lora = jnp.zeros((A_WIDTH, 3 * A_WIDTH), F32)
    lora = lora.at[0:64, 0:A_WIDTH].set(w2)
    lora = lora.at[64:128, A_WIDTH:2 * A_WIDTH].set(a2)
    lora = lora.at[128:256, 2 * A_WIDTH:].set(g2)
    return mu.reshape(1, 4 * A_WIDTH), pvec, lora


def _state_to_lanes(s):
    B, H = s.shape[:2]
    return jnp.transpose(s, (1, 3, 0, 2)).reshape(H, HEAD_DIM, B * HEAD_DIM)


def _state_from_lanes(st, B):
    H = st.shape[0]
    return jnp.transpose(st.reshape(H, HEAD_DIM, B, HEAD_DIM), (2, 0, 3, 1))


def _mlstm_kernel(pb_ref, halo_ref, cv_ref, pif_ref, gt_ref, ct0_ref, n0_ref, m0_ref, cw_ref, cb_ref, bif_ref,
                  brow_ref, ng_ref, yb_ref, cto_ref, no_ref, mo_ref, ct_ref, nt_ref, m_ref):
    c = pl.program_id(1)
    bb, L, _ = pb_ref.shape
    R = bb * L
    W = B_WIDTH
    neg = -jnp.inf

    @pl.when(c == 0)
    def _():
        ct_ref[...] = ct0_ref[...]
        nt_ref[...] = n0_ref[...]
        m_ref[...] = jnp.broadcast_to(m0_ref[...], m_ref.shape)

    pb = pb_ref[...]
    qk_in = pb[:, :, 0:2 * W]
    pre = jnp.where(c == 0, cv_ref[...], halo_ref[...])
    s1, s2, s3 = _shifted_rows(pre, qk_in, MLSTM_CONV - 1)
    cw = cw_ref[...]
    conv = cb_ref[...] + qk_in * cw[3:4, :] + s1 * cw[2:3, :] + s2 * cw[1:2, :] + s3 * cw[0:1, :]
    qk = _silu(conv).reshape(R, 2 * W)
    q = qk[:, 0:W]
    k = qk[:, W:2 * W] * (HEAD_DIM ** -0.5)
    v = pb[:, :, 2 * W:3 * W].reshape(R, W)
    og = pb[:, :, 3 * W:4 * W].reshape(R, W)

    same, incl, _ = _block_masks(R, L)
    same_f = same.astype(F32)
    incl_f = incl.astype(F32)
    gc = pif_ref[...].reshape(R, LANES) + bif_ref[...]
    lane = _iota((R, LANES), 1)
    lfc = jnp.where((lane >= B_HEADS) & (lane < 2 * B_HEADS), _log_sigmoid(gc), 0.0)
    bcum_c = _mm_sel(incl_f, lfc)
    btot_c = _mm_sel(same_f, lfc)
    gr = gt_ref[...] + brow_ref[...]
    row = _iota((SUBLANES, R), 0)
    lfr = jnp.where(row >= B_HEADS, _log_sigmoid(gr), 0.0)
    bcum_r = _mm_xsel(lfr, incl_f, NT)
    btot_r = _mm_xsel(lfr, same_f)
    m_col = jnp.broadcast_to(m_ref[:, 0:1, :], (bb, L, LANES)).reshape(R, LANES)

    per_seq = bb <= 4
    if not per_seq:
        lmask = _seq_lane_mask(R, L, bb)
        fold = _fold_mat(bb)
        tile = jnp.transpose(fold)
        expand = _expand_mat(R, L, bb)
    sh = int(math.log2(L))
    rl = _iota((R, LANES), 0)
    blockind = (jnp.right_shift(rl, sh) == lane).astype(F32)
    firstind = ((jnp.right_shift(rl, sh) == lane) & ((rl & (L - 1)) == 0)).astype(F32)

    houts, kws, wcs = [], [], []
    cts = [ct_ref[h] for h in range(B_HEADS)]
    nts = [nt_ref[h] for h in range(B_HEADS)]
    m_new_all = jnp.zeros((R, LANES), F32)
    heads = range(B_HEADS)
    hsl = [slice(h * HEAD_DIM, (h + 1) * HEAD_DIM) for h in heads]
    qks = [_mm(q[:, hs], k[:, hs], 1, NT) for hs in hsl]
    if per_seq:
        QCs = [jnp.concatenate([_mm(q[b * L:(b + 1) * L, hsl[h]], cts[h][:, b * HEAD_DIM:(b + 1) * HEAD_DIM], 1)
                                for b in range(bb)], axis=0) for h in heads]
    else:
        QCs = [_mm_xsel(jnp.where(lmask, _mm(q[:, hsl[h]], cts[h], 1), 0.0), fold) for h in heads]
    qns = [jnp.sum(_mm(q[:, hsl[h]], nts[h], 3) * blockind, axis=1, keepdims=True) for h in heads]
    Ds = [jnp.where(incl, bcum_c[:, B_HEADS + h:B_HEADS + h + 1] - bcum_r[B_HEADS + h:B_HEADS + h + 1, :] + gr[h:h + 1, :], neg)
          for h in heads]
    inters = [bcum_c[:, B_HEADS + h:B_HEADS + h + 1] + m_col[:, h:h + 1] for h in heads]
    mts = [jnp.maximum(inters[h], jnp.max(Ds[h], axis=1, keepdims=True)) for h in heads]
    Ss = [qks[h] * jnp.exp(Ds[h] - mts[h]) for h in heads]
    svs = [_mm(Ss[h], v[:, hsl[h]], 1) for h in heads]
    for h in heads:
        hs = hsl[h]
        K = k[:, hs]
        b_c = bcum_c[:, B_HEADS + h:B_HEADS + h + 1]
        b_r = bcum_r[B_HEADS + h:B_HEADS + h + 1, :]
        i_r = gr[h:h + 1, :]
        i_c = gc[:, h:h + 1]
        m_c = m_col[:, h:h + 1]
        mt = mts[h]
        iw = jnp.exp(inters[h] - mt)
        num = svs[h] + iw * QCs[h]
        den = jnp.sum(Ss[h], axis=1, keepdims=True) + iw * qns[h]
        houts.append(num / jnp.maximum(jnp.abs(den), jnp.exp(-mt)))
        bl_c = btot_c[:, B_HEADS + h:B_HEADS + h + 1]
        bl_r = btot_r[B_HEADS + h:B_HEADS + h + 1, :]
        gs_c = bl_c - b_c + i_c
        gmax = jnp.max(jnp.where(same, bl_r - b_r + i_r, neg), axis=1, keepdims=True)
        m_new = jnp.maximum(bl_c + m_c, gmax)
        kws.append(K * jnp.exp(gs_c - m_new))
        wcs.append(jnp.exp(bl_c + m_c - m_new))
        m_new_all = jnp.where(lane == h, m_new, m_new_all)

    kw_t = jnp.transpose(jnp.concatenate(kws, axis=1))
    new_c, new_n = [], []
    for h in range(B_HEADS):
        hs = slice(h * HEAD_DIM, (h + 1) * HEAD_DIM)
        ct_h = cts[h]
        if per_seq:
            new_c.append(jnp.concatenate(
                [wcs[h][b * L:b * L + 1, :] * ct_h[:, b * HEAD_DIM:(b + 1) * HEAD_DIM]
                 + _mm(kw_t[hs, b * L:(b + 1) * L], v[b * L:(b + 1) * L, hs], 1) for b in range(bb)], axis=1))
        else:
            vb = jnp.where(lmask, _mm_xsel(v[:, hs], tile), 0.0)
            wc_row = jnp.sum(wcs[h] * expand, axis=0, keepdims=True)
            new_c.append(wc_row * ct_h + _mm(kw_t[hs, :], vb, 1))
        wc_lane = jnp.sum(wcs[h] * firstind, axis=0, keepdims=True)
        new_n.append(wc_lane * nts[h] + _mm_xsel(kw_t[hs, :], blockind))
    for h in range(B_HEADS):
        ct_ref[h] = new_c[h]
        nt_ref[h] = new_n[h]
    m_ref[...] = m_new_all.reshape(bb, L, LANES)[:, 0:SUBLANES, :]

    hcat = jnp.concatenate(houts, axis=1)
    ones_h = _head_ones(W)
    inv = 1.0 / HEAD_DIM
    mu = _mm_xsel(hcat, ones_h) * inv
    hc = hcat - mu
    var = _mm_xsel(hc * hc, ones_h) * inv
    hn = hc * lax.rsqrt(var + 1e-5) * ng_ref[...]
    yb_ref[...] = (_sigmoid(og) * hn).reshape(bb, L, W)

    @pl.when(c == pl.num_programs(1) - 1)
    def _():
        cto_ref[...] = ct_ref[...]
        no_ref[...] = nt_ref[...]
        mo_ref[...] = m_ref[...]


def _mlstm(pb, conv8, pif, ct0, n0, m0, conv_w8, conv_b, bias_if, norm_g, bb, L):
    B, T, _ = pb.shape
    R = bb * L
    nbi, nch = B // bb, T // L
    lb8 = L // SUBLANES
    g_t = pif[:, :, 0:SUBLANES].reshape(nbi, bb, nch, L, SUBLANES).transpose(0, 2, 4, 1, 3).reshape(nbi, nch, SUBLANES, R)
    bias_row = jnp.broadcast_to(bias_if[0, 0:SUBLANES].reshape(SUBLANES, 1), (SUBLANES, R))
    n_in = jnp.pad(n0.reshape(nbi, bb, B_HEADS, HEAD_DIM).transpose(0, 2, 3, 1), ((0, 0), (0, 0), (0, 0), (0, LANES - bb)))
    m_in = jnp.pad(m0, ((0, 0), (0, LANES - B_HEADS))).reshape(B, 1, LANES)
    yb, ct, nt, mo = pl.pallas_call(
        _mlstm_kernel,
        grid=(nbi, nch),
        in_specs=[
            pl.BlockSpec((bb, L, 4 * B_WIDTH), lambda i, j: (i, j, 0)),
            pl.BlockSpec((bb, SUBLANES, 2 * B_WIDTH), lambda i, j: (i, jnp.maximum(j * lb8 - 1, 0), 0)),
            pl.BlockSpec((bb, SUBLANES, 2 * B_WIDTH), lambda i, j: (i, 0, 0)),
            pl.BlockSpec((bb, L, LANES), lambda i, j: (i, j, 0)),
            pl.BlockSpec((None, None, SUBLANES, R), lambda i, j: (i, j, 0, 0)),
            pl.BlockSpec((B_HEADS, HEAD_DIM, bb * HEAD_DIM), lambda i, j: (0, 0, i)),
            pl.BlockSpec((None, B_HEADS, HEAD_DIM, LANES), lambda i, j: (i, 0, 0, 0)),
            pl.BlockSpec((bb, 1, LANES), lambda i, j: (i, 0, 0)),
            pl.BlockSpec((SUBLANES, 2 * B_WIDTH), lambda i, j: (0, 0)),
            pl.BlockSpec((1, 2 * B_WIDTH), lambda i, j: (0, 0)),
            pl.BlockSpec((1, LANES), lambda i, j: (0, 0)),
            pl.BlockSpec((SUBLANES, R), lambda i, j: (0, 0)),
            pl.BlockSpec((1, B_WIDTH), lambda i, j: (0, 0)),
        ],
        out_specs=[
            pl.BlockSpec((bb, L, B_WIDTH), lambda i, j: (i, j, 0)),
            pl.BlockSpec((B_HEADS, HEAD_DIM, bb * HEAD_DIM), lambda i, j: (0, 0, i)),
            pl.BlockSpec((None, B_HEADS, HEAD_DIM, LANES), lambda i, j: (i, 0, 0, 0)),
            pl.BlockSpec((bb, SUBLANES, LANES), lambda i, j: (i, 0, 0)),
        ],
        out_shape=[
            jax.ShapeDtypeStruct((B, T, B_WIDTH), F32),
            jax.ShapeDtypeStruct((B_HEADS, HEAD_DIM, B * HEAD_DIM), F32),
            jax.ShapeDtypeStruct((nbi, B_HEADS, HEAD_DIM, LANES), F32),
            jax.ShapeDtypeStruct((B, SUBLANES, LANES), F32),
        ],
        scratch_shapes=[
            pltpu.VMEM((B_HEADS, HEAD_DIM, bb * HEAD_DIM), F32),
            pltpu.VMEM((B_HEADS, HEAD_DIM, LANES), F32),
            pltpu.VMEM((bb, SUBLANES, LANES), F32),
        ],
        compiler_params=_cparams(("arbitrary", "arbitrary")),
        name="mlstm",
    )(pb, pb, conv8, pif, g_t, ct0, n_in, m_in, conv_w8, conv_b, bias_if, bias_row, norm_g)
    n_new = nt[:, :, :, 0:bb].transpose(0, 3, 1, 2).reshape(B, B_HEADS, HEAD_DIM)
    return yb, ct, n_new, mo[:, 0, 0:B_HEADS]


def _masked_softmax(s, mask):
    mx = jnp.max(jnp.where(mask, s, NEG_INF), axis=1, keepdims=True)
    e = jnp.where(mask, jnp.exp(s - mx), 0.0)
    return e / jnp.maximum(jnp.sum(e, axis=1, keepdims=True), 1e-30)


def _select_blocks(psums, m_mat, t_col, n_s, blocks_major=False):
    tq = psums[0].shape[0]
    imps = [_mm_xsel(p, m_mat) for p in psums]
    blk = _iota((tq, LANES), 1)
    cur = jnp.right_shift(t_col, 6)
    valid = (blk * SEL_BLOCK <= t_col)
    bonus = jnp.where((blk == 0) | (blk == cur) | (blk == cur - 1), FORCE_BONUS, 0.0)
    scores = [jnp.where(blk < n_s, jnp.where(valid, imp + bonus, NEG_INF), -jnp.inf) for imp in imps]
    n_sel = min(SEL_TOP, n_s)
    if tq == LANES and n_s % SUBLANES == 0:
        scs = [jnp.transpose(sc)[0:n_s, :] for sc in scores]
        idx = _iota((n_s, tq), 0)
        ranks = [jnp.zeros((n_s, tq), F32) for _ in scs]
        for s in range(n_s):
            for c, sc in enumerate(scs):
                row = sc[s:s + 1, :]
                ranks[c] = ranks[c] + jnp.where((row > sc) | ((row == sc) & (idx > s)), 1.0, 0.0)
        sel_ts = [jnp.where(r < n_sel, 1.0, 0.0) for r in ranks]
        if n_s < LANES:
            sel_ts = [jnp.concatenate([st, jnp.zeros((LANES - n_s, tq), F32)], axis=0) for st in sel_ts]
        return sel_ts if blocks_major else [jnp.transpose(st) for st in sel_ts]
    ranks = [jnp.zeros((tq, LANES), F32) for _ in scores]
    for s in range(n_s):
        for c, sc in enumerate(scores):
            col = sc[:, s:s + 1]
            ranks[c] = ranks[c] + jnp.where((col > sc) | ((col == sc) & (blk > s)), 1.0, 0.0)
    sels = [jnp.where(r < n_sel, 1.0, 0.0) for r in ranks]
    return [jnp.transpose(x) for x in sels] if blocks_major else sels


def _stack_heads(pq, g):
    base = g * C_GROUP * HEAD_DIM
    parts = [pq[:, base + r * HEAD_DIM: base + (r + 1) * HEAD_DIM] for r in range(C_GROUP)]
    return jnp.concatenate(parts, axis=0) * (HEAD_DIM ** -0.5)


def _to_group_lanes(q, g):
    z = jnp.zeros_like(q)
    return jnp.concatenate([q, z] if g == 0 else [z, q], axis=1)


def _compress_kernel(x_ref, pw0_ref, pw1_ref, cw_ref, kc_ref, a_ref, *, rows_per_step):
    T = x_ref.shape[0]
    rs = rows_per_step
    ng = rs // CMP_STRIDE
    pool = (jnp.right_shift(_iota((ng, rs), 1), 4) == _iota((ng, rs), 0)).astype(F32)
    for c in range(T // rs):
        x = x_ref[c * rs:(c + 1) * rs, :]
        xw = jnp.concatenate([x * pw0_ref[...], x * pw1_ref[...]], axis=1)
        a_ref[c * ng:(c + 1) * ng, :] = _mm_sel(pool, xw)
    ngrp = T // CMP_STRIDE
    W = x_ref.shape[1]
    pooled = a_ref[:, 0:W] + pltpu.roll(a_ref[:, W:2 * W], ngrp - 1, axis=0)
    kc_ref[...] = _mm(pooled, cw_ref[...], 3)


def _compress(pcmp, pwt0, pwt1, cmpw_bd):
    B, T, W = pcmp.shape
    rs = pwt0.shape[0]
    ngrp = T // CMP_STRIDE
    kern = functools.partial(_compress_kernel, rows_per_step=rs)
    return pl.pallas_call(
        kern,
        grid=(B,),
        in_specs=[
            pl.BlockSpec((None, T, W), lambda b: (b, 0, 0)),
            pl.BlockSpec((rs, W), lambda b: (0, 0)),
            pl.BlockSpec((rs, W), lambda b: (0, 0)),
            pl.BlockSpec((W, W), lambda b: (0, 0)),
        ],
        out_specs=pl.BlockSpec((None, ngrp, W), lambda b: (b, 0, 0)),
        out_shape=jax.ShapeDtypeStruct((B, ngrp, W), F32),
        scratch_shapes=[pltpu.VMEM((ngrp, 2 * W), F32)],
        compiler_params=_cparams(("arbitrary",)),
        name="nsa_compress",
    )(pcmp, pwt0, pwt1, cmpw_bd)


def _combine_branches(gates, g, o_c, o_s, o_w, tq):
    outs = []
    for r in range(C_GROUP):
        rs = slice(r * tq, (r + 1) * tq)
        j = (g * C_GROUP + r) * 3
        outs.append(gates[:, j:j + 1] * o_c[rs] + gates[:, j + 1:j + 2] * o_s[rs] + gates[:, j + 2:j + 3] * o_w[rs])
    return outs


def _nsa_prompt_kernel(pq_ref, pg_ref, gb_ref, kc_ref, kv_ref, vt_ref, m_ref, et_ref, wb_ref, o_ref, *, n_s):
    i = pl.program_id(1)
    tq = pq_ref.shape[0]
    ngrp = kc_ref.shape[0]
    kt = 4 * tq
    t0 = i * tq
    rows = C_GROUP * tq
    tl = _iota((rows, 1), 0) & (tq - 1)
    t_row = t0 + tl
    t_col = t0 + _iota((tq, 1), 0)
    pq = pq_ref[...]
    gates = _sigmoid(pg_ref[...] + gb_ref[...])
    kc = kc_ref[...]
    n_end = _iota((rows, ngrp), 1) * CMP_STRIDE + (CMP_BLOCK - 1)
    cmask = n_end <= t_row
    n_full = t0 // kt
    tl_lane = _iota((1, rows), 1) & (tq - 1)
    diag_bias = jnp.where(_iota((kt, rows), 0) <= (t0 - n_full * kt) + tl_lane, 0.0, NEG_INF)
    n_tiles = WINDOW // tq + 1
    groups = range(C_KV_HEADS)
    qs = [_stack_heads(pq, g) for g in groups]
    q2fs = [_to_group_lanes(qs[g], g) for g in groups]
    q2s = [x.astype(BF16) for x in q2fs]
    q2ts = [jnp.transpose(x).astype(BF16) for x in q2fs]
    p_cs = [_masked_softmax(_mm(qs[g], kc[:, g * HEAD_DIM:(g + 1) * HEAD_DIM], 3, NT), cmask) for g in groups]
    o_cs = [_mm(p_cs[g], kc[:, 2 * HEAD_DIM + g * HEAD_DIM: 2 * HEAD_DIM + (g + 1) * HEAD_DIM], 1) for g in groups]
    psums = [functools.reduce(lambda a, b: a + b, [p[r * tq:(r + 1) * tq] for r in range(C_GROUP)]) for p in p_cs]
    sel_ts = [x.astype(BF16) for x in _select_blocks(psums, m_ref[...], t_col, n_s, blocks_major=True)]

    def step(j, carries, extra):
        off = pl.multiple_of(j * kt, kt)
        kk = kv_ref[pl.ds(off, kt), 0:LANES]
        vv_t = vt_ref[:, pl.ds(off, kt)]
        e_t = et_ref[pl.ds(off, kt), :]
        biases = [(_dot(e_t, sel_ts[g]) - 1.0) * (-NEG_INF) for g in groups]
        ss = [_dot(kk, q2ts[g]) + jnp.concatenate([biases[g]] * C_GROUP, axis=1) for g in groups]
        if extra is not None:
            ss = [s + extra for s in ss]
        m_news = [jnp.maximum(carries[g][0], jnp.max(ss[g], axis=0, keepdims=True)) for g in groups]
        ps = [jnp.exp(ss[g] - m_news[g]) for g in groups]
        alphas = [jnp.exp(carries[g][0] - m_news[g]) for g in groups]
        ls = [alphas[g] * carries[g][1] + jnp.sum(ps[g], axis=0, keepdims=True) for g in groups]
        accs = [alphas[g] * carries[g][2] + _dot(vv_t, ps[g].astype(BF16)) for g in groups]
        return tuple((m_news[g], ls[g], accs[g]) for g in groups)

    init1 = (jnp.full((1, rows), NEG_INF, F32), jnp.zeros((1, rows), F32), jnp.zeros((LANES, rows), F32))
    carries = lax.fori_loop(0, n_full, lambda j, c: step(j, c, None), tuple(init1 for _ in groups))
    carries = step(n_full, carries, diag_bias)

    gsl = [slice(g * HEAD_DIM, (g + 1) * HEAD_DIM) for g in groups]
    o_ss = [jnp.transpose(carries[g][2] / carries[g][1])[:, gsl[g]] for g in groups]

    kws, vws, wbs = [], [], []
    for cidx in range(n_tiles):
        tile = i - (n_tiles - 1) + cidx
        off = pl.multiple_of(jnp.maximum(tile, 0) * tq, tq)
        kws.append(kv_ref[pl.ds(off, tq), 2 * LANES:3 * LANES])
        vws.append(kv_ref[pl.ds(off, tq), 3 * LANES:4 * LANES])
        wbs.append(wb_ref[:, cidx * tq:(cidx + 1) * tq] + jnp.where(tile >= 0, 0.0, NEG_INF))
    vw = jnp.concatenate(vws, axis=0)
    s_ws = [jnp.concatenate([_dot(q2s[g], kws[c], NT) + wbs[c] for c in range(n_tiles)], axis=1) for g in groups]
    e_ws = [jnp.exp(s - jnp.max(s, axis=1, keepdims=True)) for s in s_ws]
    p_ws = [e / jnp.sum(e, axis=1, keepdims=True) for e in e_ws]
    o_ws = [_dot(p_ws[g].astype(BF16), vw)[:, gsl[g]] for g in groups]
    pieces = []
    for g in groups:
        pieces += _combine_branches(gates, g, o_cs[g], o_ss[g], o_ws[g], tq)
    o_ref[...] = jnp.concatenate(pieces, axis=1)


def _nsa_prompt(pq, pg, gate_b, kc, kvb, m_mat, e_mat, tq):
    B, T, _ = pq.shape
    ngrp = kc.shape[1]
    n_s = T // SEL_BLOCK
    n_tiles = WINDOW // tq + 1
    kp = np.arange(n_tiles * tq)[None, :] - (n_tiles - 1) * tq
    tloc = (np.arange(C_GROUP * tq) % tq)[:, None]
    wbias = jnp.asarray(np.where((kp <= tloc) & (kp > tloc - WINDOW), 0.0, NEG_INF).astype(np.float32))
    kern = functools.partial(_nsa_prompt_kernel, n_s=n_s)
    return pl.pallas_call(
        kern,
        grid=(B, T // tq),
        in_specs=[
            pl.BlockSpec((None, tq, C_WIDTH), lambda b, i: (b, i, 0)),
            pl.BlockSpec((None, tq, LANES), lambda b, i: (b, i, 0)),
            pl.BlockSpec((1, LANES), lambda b, i: (0, 0)),
            pl.BlockSpec((None, ngrp, 4 * HEAD_DIM), lambda b, i: (b, 0, 0)),
            pl.BlockSpec((None, T, 4 * LANES), lambda b, i: (b, 0, 0)),
            pl.BlockSpec((None, LANES, T), lambda b, i: (b, 0, 0)),
            pl.BlockSpec((ngrp, LANES), lambda b, i: (0, 0)),
            pl.BlockSpec((T, LANES), lambda b, i: (0, 0)),
            pl.BlockSpec((C_GROUP * tq, n_tiles * tq), lambda b, i: (0, 0)),
        ],
        out_specs=pl.BlockSpec((None, tq, C_WIDTH), lambda b, i: (b, i, 0)),
        out_shape=jax.ShapeDtypeStruct((B, T, C_WIDTH), F32),
        compiler_params=_cparams(("arbitrary", "arbitrary")),
        name="nsa_prompt",
    )(pq, pg, gate_b, kc, kvb, jnp.swapaxes(kvb[:, :, LANES:2 * LANES], 1, 2), m_mat, jnp.transpose(e_mat), wbias)


def _nsa_sample_kernel(pt_ref, *refs, n_pages, nseq, past_len, n_s, n_c):
    (pq_ref, pg_ref, gb_ref, pcmp_ref, pslc_ref, pwin_ref, win_ref, pw0_ref, pw1_ref, cw_ref, m_ref, e_ref,
     o_ref) = refs[nseq * n_pages:]
    seqs = range(nseq)
    pages = [refs[s * n_pages:(s + 1) * n_pages] for s in seqs]
    tq = pq_ref.shape[1]
    rows = C_GROUP * tq
    W = 4 * HEAD_DIM
    ng = PAGE_SIZE // CMP_STRIDE
    ngrp = m_ref.shape[0]
    pw0 = pw0_ref[...]
    pw1 = pw1_ref[...]

    def tail_t(x):
        return jnp.transpose(jnp.concatenate([x, jnp.zeros((PAGE_SIZE - tq, x.shape[1]), F32)], axis=0))

    xs = [[pages[s][p][0:W, :] for p in range(n_pages)] + [tail_t(pcmp_ref[s])] for s in seqs]
    a_acc = jnp.zeros((nseq * 2 * W, ngrp), F32)
    for p0 in range(0, n_pages + 1, 2):
        xw = jnp.concatenate(
            [jnp.concatenate([jnp.concatenate([x * pw0, x * pw1], axis=0) for x in xs[s][p0:p0 + 2]], axis=1)
             for s in seqs], axis=0)
        kdim = xw.shape[1]
        sel = (jnp.right_shift(_iota((kdim, ngrp), 0), 4) + p0 * ng == _iota((kdim, ngrp), 1)).astype(BF16)
        a_acc = a_acc + _dot(xw.astype(BF16), sel)
    pooled_ts = [a_acc[s * 2 * W:s * 2 * W + W, :] + pltpu.roll(a_acc[s * 2 * W + W:(s + 1) * 2 * W, :], ngrp - 1, axis=1)
                 for s in seqs]
    kc_all = _mm(cw_ref[...], jnp.concatenate(pooled_ts, axis=1), 3)
    kc_ts = [kc_all[:, s * ngrp:(s + 1) * ngrp] for s in seqs]
    pslc_ts = [tail_t(pslc_ref[s]) for s in seqs]
    k_ts = [jnp.concatenate([pages[s][p][W:W + LANES, :].astype(BF16) for p in range(n_pages)]
                            + [pslc_ts[s][0:LANES].astype(BF16)], axis=1) for s in seqs]
    v_ts = [jnp.concatenate([pages[s][p][W + LANES:W + 2 * LANES, :].astype(BF16) for p in range(n_pages)]
                            + [pslc_ts[s][LANES:2 * LANES].astype(BF16)], axis=1) for s in seqs]
    nk = k_ts[0].shape[1]
    t_row = past_len + (_iota((rows, 1), 0) & (tq - 1))
    t_col = past_len + _iota((tq, 1), 0)
    n_idx = _iota((rows, ngrp), 1)
    cmask = (n_idx * CMP_STRIDE + (CMP_BLOCK - 1) <= t_row) & (n_idx < n_c)
    kpos = _iota((rows, nk), 1)
    nwb = win_ref.shape[2]
    pwin_ts = [tail_t(pwin_ref[s]) for s in seqs]
    kw_ts = [jnp.concatenate([win_ref[s, 0:LANES, :], pwin_ts[s][0:LANES]], axis=1).astype(BF16) for s in seqs]
    vw_ts = [jnp.concatenate([win_ref[s, LANES:2 * LANES, :], pwin_ts[s][LANES:2 * LANES]], axis=1).astype(BF16)
             for s in seqs]
    jj = _iota((rows, nwb + PAGE_SIZE), 1)
    tl = _iota((rows, 1), 0) & (tq - 1)
    wmask = (jj > tl + (nwb - WINDOW)) & (jj <= tl + nwb) & (jj < nwb + tq)

    chains = [(s, g) for s in seqs for g in range(C_KV_HEADS)]
    gsl = [slice(g * HEAD_DIM, (g + 1) * HEAD_DIM) for g in range(C_KV_HEADS)]
    qs = [_stack_heads(pq_ref[s], g) for s, g in chains]
    q2s = [_to_group_lanes(q, g).astype(BF16) for q, (s, g) in zip(qs, chains)]
    p_cs = [_masked_softmax(_mm(q, kc_ts[s][gsl[g], :], 3), cmask) for q, (s, g) in zip(qs, chains)]
    o_cs = [_mm(p, kc_ts[s][2 * HEAD_DIM + g * HEAD_DIM: 2 * HEAD_DIM + (g + 1) * HEAD_DIM, :], 1, NT)
            for p, (s, g) in zip(p_cs, chains)]
    psums = [functools.reduce(lambda a, b: a + b, [p[r * tq:(r + 1) * tq] for r in range(C_GROUP)]) for p in p_cs]
    sels = _select_blocks(psums, m_ref[...], t_col, n_s)
    e_mat = e_ref[...]
    smasks = [(_dot(jnp.concatenate([sel] * C_GROUP, axis=0).astype(BF16), e_mat) > 0.5) & (kpos <= t_row)
              for sel in sels]
    p_ss = [_masked_softmax(_dot(q2, k_ts[s]), m) for q2, m, (s, g) in zip(q2s, smasks, chains)]
    o_ss = [_dot(p.astype(BF16), v_ts[s], NT)[:, gsl[g]] for p, (s, g) in zip(p_ss, chains)]
    p_ws = [_masked_softmax(_dot(q2, kw_ts[s]), wmask) for q2, (s, g) in zip(q2s, chains)]
    o_ws = [_dot(p.astype(BF16), vw_ts[s], NT)[:, gsl[g]] for p, (s, g) in zip(p_ws, chains)]
    for s in seqs:
        gates = _sigmoid(pg_ref[s] + gb_ref[...])
        pieces = []
        for c, (cs, g) in enumerate(chains):
            if cs == s:
                pieces += _combine_branches(gates, g, o_cs[c], o_ss[c], o_ws[c], tq)
        o_ref[s] = jnp.concatenate(pieces, axis=1)


def _nsa_sample(layer, page_table, cache_t, pq, pg, gate_b, pcmp, pslc, pwin, win_t, pwt0, pwt1, cmpw_t, m_mat, e_mat):
    B, T, _ = pq.shape
    nseq = 2 if B % 2 == 0 else 1
    n_pages = page_table.shape[1]
    past_len = n_pages * PAGE_SIZE
    lp = -(-(past_len + T) // SEL_BLOCK) * SEL_BLOCK
    n_s = lp // SEL_BLOCK
    n_c = lp // CMP_STRIDE - CMP_BLOCK // CMP_STRIDE + 1
    ngrp = m_mat.shape[0]
    nk = past_len + PAGE_SIZE
    nwb = win_t.shape[3]
    kern = functools.partial(_nsa_sample_kernel, n_pages=n_pages, nseq=nseq, past_len=past_len, n_s=n_s, n_c=n_c)
    page_specs = [
        pl.BlockSpec((None, None, 4 * LANES, PAGE_SIZE),
                     functools.partial(lambda b, pt, s, p: (layer, pt[b * nseq + s, p], 0, 0), s=s, p=p))
        for s in range(nseq) for p in range(n_pages)
    ]
    row = lambda w: pl.BlockSpec((nseq, T, w), lambda b, pt: (b, 0, 0))
    full = lambda a: pl.BlockSpec(a.shape, lambda b, pt: (0,) * a.ndim)
    grid_spec = pltpu.PrefetchScalarGridSpec(
        num_scalar_prefetch=1,
        grid=(B // nseq,),
        in_specs=page_specs + [
            row(C_WIDTH), row(LANES), full(gate_b), row(4 * HEAD_DIM), row(4 * HEAD_DIM), row(4 * HEAD_DIM),
            pl.BlockSpec((None, nseq, 4 * HEAD_DIM, nwb), lambda b, pt: (layer, b, 0, 0)),
            full(pwt0), full(pwt1), full(cmpw_t), full(m_mat), full(e_mat),
        ],
        out_specs=pl.BlockSpec((nseq, T, C_WIDTH), lambda b, pt: (b, 0, 0)),
    )
    return pl.pallas_call(
        kern,
        grid_spec=grid_spec,
        out_shape=jax.ShapeDtypeStruct((B, T, C_WIDTH), F32),
        compiler_params=_cparams(("arbitrary",)),
        name="nsa_sample",
    )(page_table, *([cache_t] * (nseq * n_pages)), pq, pg, gate_b, pcmp, pslc, pwin, win_t, pwt0, pwt1, cmpw_t, m_mat, e_mat)


def _nsa_consts(lp, n_keys):
    n_str = lp // CMP_STRIDE
    n_c = n_str - CMP_BLOCK // CMP_STRIDE + 1
    n_s = lp // SEL_BLOCK
    c0 = np.arange(n_str)[:, None] * CMP_STRIDE
    s0 = np.arange(LANES)[None, :] * SEL_BLOCK
    m = (c0 < s0 + SEL_BLOCK) & (c0 + CMP_BLOCK > s0) & (np.arange(n_str)[:, None] < n_c) & (np.arange(LANES)[None, :] < n_s)
    e = (np.arange(n_keys)[None, :] // SEL_BLOCK) == np.arange(LANES)[:, None]
    return m.astype(np.float32), e.astype(np.float32)


def _prep_mlstm(conv_w, conv_b, i_b, f_b, norm_g):
    cw8 = jnp.pad(conv_w, ((0, SUBLANES - MLSTM_CONV), (0, 0)))
    bias_if = jnp.pad(jnp.concatenate([i_b, f_b]), (0, LANES - 2 * B_HEADS)).reshape(1, LANES)
    return cw8, conv_b.reshape(1, 2 * B_WIDTH), bias_if, norm_g.reshape(1, B_WIDTH)


_A0, _B0, _Q0, _CMP0, _SLC0, _WIN0, _G0, _IF0, _PEND = 0, 1024, 2048, 2560, 2816, 3072, 3328, 3456, 3584
_PROJ_SPLITS = ((_A0, _B0), (_B0, _Q0), (_Q0, _CMP0), (_CMP0, _SLC0), (_SLC0, _WIN0), (_WIN0, _G0), (_G0, _IF0),
                (_IF0, _PEND))


def _pad_lanes(a, width):
    return jnp.pad(a, [(0, 0)] * (a.ndim - 1) + [(0, width - a.shape[-1])])


def _prep_layer(P, l):
    w_in = P["w_in"][l]
    wa, wb, wc = w_in[:, 0:1024], w_in[:, 1024:2056], w_in[:, 2056:3360]
    w_in_p = jnp.concatenate([
        wa, wb[:, 0:768], wb[:, 776:1032], wc[:, 0:512], wc[:, 512:1280],
        _pad_lanes(wc[:, 1280:1304], LANES), _pad_lanes(wb[:, 768:776], LANES)], axis=1).astype(BF16)
    up = P["ffn_up"][l]
    ffn_up_p = jnp.concatenate([_pad_lanes(up[:, 0:D_FF], D_FF_PAD), _pad_lanes(up[:, D_FF:], D_FF_PAD)], axis=1).astype(BF16)
    cw = P["nsa_cmp_w"][l]
    cmpw_bd = jnp.zeros((4 * HEAD_DIM, 4 * HEAD_DIM), F32)
    for kv in range(2):
        for g in range(C_KV_HEADS):
            o = (kv * C_KV_HEADS + g) * HEAD_DIM
            cmpw_bd = cmpw_bd.at[o:o + HEAD_DIM, o:o + HEAD_DIM].set(cw[kv, g])
    pool = P["nsa_pool_w"][l].reshape(CMP_BLOCK, 4 * HEAD_DIM)
    return dict(
        w_in=w_in_p,
        w_out=P["w_out"][l].astype(BF16),
        ffn_up=ffn_up_p,
        ffn_down=jnp.pad(P["ffn_down"][l], ((0, D_FF_PAD - D_FF), (0, 0))).astype(BF16),
        ffn_cw=jnp.pad(P["ffn_conv_w"][l], ((0, SUBLANES - FFN_CONV), (0, D_FF_PAD - D_FF))),
        ffn_cb=_pad_lanes(P["ffn_conv_b"][l].reshape(1, D_FF), D_FF_PAD),
        ln1=(P["ln_g"][l, 0].reshape(1, D_MODEL), P["ln_b"][l, 0].reshape(1, D_MODEL)),
        ln2=(P["ln_g"][l, 1].reshape(1, D_MODEL), P["ln_b"][l, 1].reshape(1, D_MODEL)),
        rwkv=_prep_rwkv(P["rwkv_mu"][l], P["rwkv_w0"][l], P["rwkv_w2"][l], P["rwkv_a0"][l], P["rwkv_a2"][l],
                        P["rwkv_g2"][l], P["rwkv_k_k"][l], P["rwkv_k_a"][l], P["rwkv_r_k"][l], P["rwkv_ln_g"][l],
                        P["rwkv_ln_b"][l]),
        mlstm=_prep_mlstm(P["mlstm_conv_w"][l], P["mlstm_conv_b"][l], P["mlstm_i_b"][l], P["mlstm_f_b"][l],
                          P["mlstm_norm_g"][l]),
        pool0=pool[0:CMP_STRIDE], pool1=pool[CMP_STRIDE:CMP_BLOCK], cmpw=cmpw_bd,
        gate_b=_pad_lanes(P["nsa_gate_b"][l].reshape(1, 3 * C_HEADS), LANES),
    )


def _rows8(state):
    return jnp.pad(state, ((0, 0), (SUBLANES - state.shape[1], 0), (0, 0)))


def _last_rows(prev, cur, k):
    if cur.shape[1] >= k:
        return cur[:, cur.shape[1] - k:]
    return jnp.concatenate([prev, cur], axis=1)[:, -k:]


def _trunk(x, mod, st, layers, nsa_fn, dense_tile, rec_tile):
    B, T, _ = x.shape
    dbb, dL = dense_tile
    rbb, rL = rec_tile
    new = {k: [] for k in ("nsa_kv", "win_kv", "rwkv", "rwkv_shift", "mlstm_C", "mlstm_n", "mlstm_m", "mlstm_conv", "ffn_conv")}
    for l, Lw in enumerate(layers):
        m = mod[l]
        pa, pb, pq, pcmp, pslc, pwin, pg, pif, kvb = _modmm(
            x, m, 0, 1, Lw["w_in"], _PROJ_SPLITS + ((_SLC0, _G0),), (False,) * 8 + (True,), dbb, dL, "in_proj")
        ya, rw_st = _rwkv(pa, _rows8(st["rwkv_shift"][l][:, None, :]), _state_to_lanes(st["rwkv"][l]), *Lw["rwkv"], rbb, rL)
        yb, c_st, n_st, m_st = _mlstm(pb, _rows8(st["mlstm_conv"][l]), pif, _state_to_lanes(st["mlstm_C"][l]),
                                      st["mlstm_n"][l], st["mlstm_m"][l], *Lw["mlstm"], rbb, rL)
        yc, win_new = nsa_fn(l, Lw, pq, pg, pcmp, pslc, pwin, kvb)
        x = _outproj(ya, yb, yc, x, m, Lw["w_out"], *Lw["ln1"], dbb, dL)
        (u,) = _modmm(x, m, 3, 4, Lw["ffn_up"], ((0, 2 * D_FF_PAD),), (False,), dbb, dL, "ffn_up")
        st8 = _rows8(_pad_lanes(st["ffn_conv"][l], D_FF_PAD))
        x = _ffn_down(u, st8, x, m, Lw["ffn_cw"], Lw["ffn_cb"], Lw["ffn_down"], *Lw["ln2"], dbb, dL)
        new["nsa_kv"].append(jnp.concatenate([pcmp, pslc], axis=-1).reshape(B, T, 4, C_KV_HEADS, HEAD_DIM))
        new["win_kv"].append(win_new)
        new["rwkv"].append(_state_from_lanes(rw_st, B))
        new["rwkv_shift"].append(pa[:, -1])
        new["mlstm_C"].append(_state_from_lanes(c_st, B))
        new["mlstm_n"].append(n_st)
        new["mlstm_m"].append(m_st)
        new["mlstm_conv"].append(_last_rows(st["mlstm_conv"][l], pb[:, :, 0:2 * B_WIDTH], MLSTM_CONV - 1))
        new["ffn_conv"].append(_last_rows(st["ffn_conv"][l], u[:, :, 0:D_FF], FFN_CONV - 1))
    return x, {k: jnp.stack(v) for k, v in new.items()}


def kernel(x_prompt, x_sample, c_prompt, c_sample, cache_nsa_kv, cache_win_kv, state_rwkv, state_rwkv_shift,
           state_mlstm_C, state_mlstm_n, state_mlstm_m, state_mlstm_conv, state_ffn_conv, page_table,
           w_in, w_out, ada_w, ada_b, ln_g, ln_b, rwkv_mu, rwkv_w0, rwkv_w2, rwkv_a0, rwkv_a2, rwkv_g2,
           rwkv_k_k, rwkv_k_a, rwkv_r_k, rwkv_ln_g, rwkv_ln_b, mlstm_conv_w, mlstm_conv_b, mlstm_i_b,
           mlstm_f_b, mlstm_norm_g, nsa_pool_w, nsa_cmp_w, nsa_gate_b, ffn_up, ffn_conv_w, ffn_conv_b, ffn_down):
    P = dict(w_in=w_in, w_out=w_out, ln_g=ln_g, ln_b=ln_b, rwkv_mu=rwkv_mu, rwkv_w0=rwkv_w0, rwkv_w2=rwkv_w2,
             rwkv_a0=rwkv_a0, rwkv_a2=rwkv_a2, rwkv_g2=rwkv_g2, rwkv_k_k=rwkv_k_k, rwkv_k_a=rwkv_k_a,
             rwkv_r_k=rwkv_r_k, rwkv_ln_g=rwkv_ln_g, rwkv_ln_b=rwkv_ln_b, mlstm_conv_w=mlstm_conv_w,
             mlstm_conv_b=mlstm_conv_b, mlstm_i_b=mlstm_i_b, mlstm_f_b=mlstm_f_b, mlstm_norm_g=mlstm_norm_g,
             nsa_pool_w=nsa_pool_w, nsa_cmp_w=nsa_cmp_w, nsa_gate_b=nsa_gate_b, ffn_up=ffn_up,
             ffn_conv_w=ffn_conv_w, ffn_conv_b=ffn_conv_b, ffn_down=ffn_down)
    Bp, Tp, _ = x_prompt.shape
    Bs, Ts, _ = x_sample.shape
    G, dh = C_KV_HEADS, HEAD_DIM
    layers = [_prep_layer(P, l) for l in range(DEPTH)]

    nb = -(-(Bp + Bs) // SUBLANES) * SUBLANES
    c_all = jnp.pad(jnp.concatenate([c_prompt, c_sample], axis=0), ((0, nb - Bp - Bs), (0, 0)))
    mod = _ada(c_all, ada_w, ada_b)
    mod_p = mod[:, 0:Bp].reshape(DEPTH, Bp, 1, 6 * D_MODEL)
    mod_s = mod[:, Bp:Bp + Bs].reshape(DEPTH, Bs, 1, 6 * D_MODEL)

    st_p = dict(
        rwkv=jnp.zeros((DEPTH, Bp, A_HEADS, dh, dh), F32), rwkv_shift=jnp.zeros((DEPTH, Bp, 4 * A_WIDTH), F32),
        mlstm_C=jnp.zeros((DEPTH, Bp, B_HEADS, dh, dh), F32), mlstm_n=jnp.zeros((DEPTH, Bp, B_HEADS, dh), F32),
        mlstm_m=jnp.zeros((DEPTH, Bp, B_HEADS), F32), mlstm_conv=jnp.zeros((DEPTH, Bp, MLSTM_CONV - 1, 2 * B_WIDTH), F32),
        ffn_conv=jnp.zeros((DEPTH, Bp, FFN_CONV - 1, D_FF), F32))
    m_p, e_p = _nsa_consts(Tp, Tp)
    e_p = jnp.asarray(e_p, BF16)
    tq = 128
    rs = 4 * tq

    def nsa_prompt(l, Lw, pq, pg, pcmp, pslc, pwin, kvb):
        kc = _compress(pcmp, jnp.tile(Lw["pool0"], (rs // CMP_STRIDE, 1)), jnp.tile(Lw["pool1"], (rs // CMP_STRIDE, 1)), Lw["cmpw"])
        yc = _nsa_prompt(pq, pg, Lw["gate_b"], kc, kvb, jnp.asarray(m_p), e_p, tq)
        return yc, pwin[:, -min(WINDOW, Tp):].reshape(Bp, min(WINDOW, Tp), 2, G, dh)

    y_prompt, new_p = _trunk(x_prompt, mod_p, st_p, layers, nsa_prompt, (1, 256), (Bp, ROWS // Bp))

    st_s = dict(rwkv=state_rwkv, rwkv_shift=state_rwkv_shift, mlstm_C=state_mlstm_C, mlstm_n=state_mlstm_n,
                mlstm_m=state_mlstm_m, mlstm_conv=state_mlstm_conv, ffn_conv=state_ffn_conv)
    n_pages = page_table.shape[1]
    past_len = n_pages * PAGE_SIZE
    lp = -(-(past_len + Ts) // SEL_BLOCK) * SEL_BLOCK
    ngrp = 2 * LANES
    assert (n_pages + 1) * (PAGE_SIZE // CMP_STRIDE) <= ngrp
    m_s, e_s = _nsa_consts(lp, past_len + PAGE_SIZE)
    m_s = jnp.asarray(np.pad(m_s, ((0, ngrp - m_s.shape[0]), (0, 0))))
    e_s = jnp.asarray(e_s, BF16)
    cache_t = jnp.transpose(cache_nsa_kv, (0, 1, 3, 4, 5, 2)).reshape(DEPTH, cache_nsa_kv.shape[1], 4 * G * dh, PAGE_SIZE)
    nwb = cache_win_kv.shape[2]
    win_t_all = jnp.transpose(cache_win_kv, (0, 1, 3, 4, 5, 2)).reshape(DEPTH, Bs, 2 * G * dh, nwb)
    reps = PAGE_SIZE // CMP_STRIDE

    def nsa_sample(l, Lw, pq, pg, pcmp, pslc, pwin, kvb):
        yc = _nsa_sample(l, page_table, cache_t, pq, pg, Lw["gate_b"], pcmp, pslc, pwin, win_t_all,
                         jnp.tile(Lw["pool0"].T, (1, reps)), jnp.tile(Lw["pool1"].T, (1, reps)),
                         Lw["cmpw"].T, m_s, e_s)
        return yc, pwin.reshape(Bs, Ts, 2, G, dh)

    y_sample, new_s = _trunk(x_sample, mod_s, st_s, layers, nsa_sample, (ROWS // Ts, Ts), (ROWS // Ts, Ts))
    keep = min(WINDOW, nwb + Ts)
    new_s["win_kv"] = jnp.concatenate([cache_win_kv, new_s["win_kv"]], axis=2)[:, :, nwb + Ts - keep:]

    return (y_prompt, y_sample,
            new_p["nsa_kv"], new_s["nsa_kv"], new_p["win_kv"], new_s["win_kv"],
            new_p["rwkv"], new_s["rwkv"], new_p["rwkv_shift"], new_s["rwkv_shift"],
            new_p["mlstm_C"], new_s["mlstm_C"], new_p["mlstm_n"], new_s["mlstm_n"],
            new_p["mlstm_m"], new_s["mlstm_m"], new_p["mlstm_conv"], new_s["mlstm_conv"],
            new_p["ffn_conv"], new_s["ffn_conv"])
```

```python
import functools
import math

import numpy as np
import jax
import jax.numpy as jnp
from jax import lax
from jax.experimental import pallas as pl
from jax.experimental.pallas import tpu as pltpu

F32 = jnp.float32
BF16 = jnp.bfloat16

D_MODEL = 1024
DEPTH = 4
HEAD_DIM = 64
A_WIDTH = 256
B_WIDTH = 256
C_WIDTH = 512
A_HEADS = 4
B_HEADS = 4
C_HEADS = 8
C_KV_HEADS = 2
C_GROUP = 4
PAGE_SIZE = 128
MLSTM_CONV = 4
CMP_BLOCK = 32
CMP_STRIDE = 16
SEL_BLOCK = 64
SEL_TOP = 16
WINDOW = 512
D_FF = 2752
D_FF_PAD = 2816
FFN_CONV = 3
ALPHA = (2 * DEPTH) ** 0.25
FORCE_BONUS = 1e4
NEG_INF = -1e30
LANES = 128
SUBLANES = 8
ROWS = 256
VMEM_LIMIT = 56 * 1024 * 1024

NN = (((1,), (0,)), ((), ()))
NT = (((1,), (1,)), ((), ()))


def _dot(a, b, dn=NN):
    return lax.dot_general(a, b, dn, preferred_element_type=F32)


def _split2(a):
    hi = a.astype(BF16)
    lo = (a - hi.astype(F32)).astype(BF16)
    return hi, lo


def _mm(a, b, passes=1, dn=NN):
    if passes == 1:
        return _dot(a.astype(BF16), b.astype(BF16), dn)
    ah, al = _split2(a)
    bh, bl = _split2(b)
    return _dot(ah, bh, dn) + (_dot(al, bh, dn) + _dot(ah, bl, dn))


def _mm_sel(sel, x, dn=NN):
    s = sel.astype(BF16)
    x1 = x.astype(BF16)
    r1 = x - x1.astype(F32)
    x2 = r1.astype(BF16)
    x3 = (r1 - x2.astype(F32)).astype(BF16)
    return _dot(s, x1, dn) + (_dot(s, x2, dn) + _dot(s, x3, dn))


def _mm_xsel(x, sel, dn=NN):
    s = sel.astype(BF16)
    x1 = x.astype(BF16)
    r1 = x - x1.astype(F32)
    x2 = r1.astype(BF16)
    x3 = (r1 - x2.astype(F32)).astype(BF16)
    return _dot(x1, s, dn) + (_dot(x2, s, dn) + _dot(x3, s, dn))


def _sigmoid(x):
    return 1.0 / (1.0 + jnp.exp(-x))


def _silu(x):
    return x * _sigmoid(x)


def _softplus(x):
    return jnp.maximum(x, 0.0) + jnp.log(1.0 + jnp.exp(-jnp.abs(x)))


def _log_sigmoid(x):
    return -_softplus(-x)


def _iota(shape, axis):
    return lax.broadcasted_iota(jnp.int32, shape, axis)


def _block_masks(rows, chunk):
    sh = int(math.log2(chunk))
    r = _iota((rows, rows), 0)
    s = _iota((rows, rows), 1)
    same = jnp.right_shift(r, sh) == jnp.right_shift(s, sh)
    return same, same & (s <= r), same & (s < r)


def _head_ones(width):
    r = _iota((width, width), 0)
    s = _iota((width, width), 1)
    return (jnp.right_shift(r, 6) == jnp.right_shift(s, 6)).astype(F32)


def _expand_mat(rows, chunk, nseq):
    sh = int(math.log2(chunk))
    r = _iota((rows, nseq * HEAD_DIM), 0)
    c = _iota((rows, nseq * HEAD_DIM), 1)
    return ((jnp.right_shift(c, 6) == jnp.right_shift(r, sh)) & ((r & (chunk - 1)) == 0)).astype(F32)


def _seq_lane_mask(rows, chunk, nseq):
    sh = int(math.log2(chunk))
    r = _iota((rows, nseq * HEAD_DIM), 0)
    c = _iota((rows, nseq * HEAD_DIM), 1)
    return jnp.right_shift(c, 6) == jnp.right_shift(r, sh)


def _fold_mat(nseq):
    r = _iota((nseq * HEAD_DIM, HEAD_DIM), 0)
    c = _iota((nseq * HEAD_DIM, HEAD_DIM), 1)
    return ((r & (HEAD_DIM - 1)) == c).astype(F32)


def _shifted_rows(pre, cur, nshift):
    bb, L, C = cur.shape
    full = jnp.concatenate([pre, cur], axis=1).reshape(bb * (L + SUBLANES), C)
    out = []
    for k in range(1, nshift + 1):
        sh = pltpu.roll(full, k, axis=0).reshape(bb, L + SUBLANES, C)
        out.append(sh[:, SUBLANES:, :])
    return out


def _layernorm(z, g, b):
    mu = jnp.mean(z, axis=-1, keepdims=True)
    zc = z - mu
    var = jnp.mean(zc * zc, axis=-1, keepdims=True)
    return zc * lax.rsqrt(var + 1e-5) * g + b


def _cparams(sem):
    return pltpu.CompilerParams(dimension_semantics=sem, vmem_limit_bytes=VMEM_LIMIT)


def _ada_kernel(c_ref, w_ref, b_ref, o_ref):
    c = c_ref[...]
    o_ref[...] = _mm(_silu(c), w_ref[...], 3) + b_ref[...]


def _ada(c_all, ada_w, ada_b):
    nb = c_all.shape[0]
    tn = 1536
    return pl.pallas_call(
        _ada_kernel,
        grid=(DEPTH, 6 * D_MODEL // tn),
        in_specs=[
            pl.BlockSpec((nb, D_MODEL), lambda l, n: (0, 0)),
            pl.BlockSpec((None, D_MODEL, tn), lambda l, n: (l, 0, n)),
            pl.BlockSpec((None, 1, tn), lambda l, n: (l, 0, n)),
        ],
        out_specs=pl.BlockSpec((None, nb, tn), lambda l, n: (l, 0, n)),
        out_shape=jax.ShapeDtypeStruct((DEPTH, nb, 6 * D_MODEL), F32),
        compiler_params=_cparams(("arbitrary", "arbitrary")),
        name="ada_mod",
    )(c_all, ada_w, ada_b.reshape(DEPTH, 1, 6 * D_MODEL))


def _modmm_kernel(x_ref, sh_ref, sc_ref, w_ref, *o_refs, splits, bf16_outs):
    x = x_ref[...]
    bb, L, D = x.shape
    h = (x * (1.0 + sc_ref[...]) + sh_ref[...]).reshape(bb * L, D).astype(BF16)
    o = jnp.dot(h, w_ref[...], preferred_element_type=F32)
    for (a, b), o_ref, as_bf16 in zip(splits, o_refs, bf16_outs):
        piece = o[:, a:b].reshape(bb, L, b - a)
        o_ref[...] = piece.astype(BF16) if as_bf16 else piece


def _modmm(x, mod, sh_col, sc_col, w, splits, bf16_outs, bb, L, name):
    B, T, D = x.shape
    N = w.shape[1]
    kern = functools.partial(_modmm_kernel, splits=splits, bf16_outs=bf16_outs)
    return pl.pallas_call(
        kern,
        grid=(B // bb, T // L),
        in_specs=[
            pl.BlockSpec((bb, L, D), lambda i, j: (i, j, 0)),
            pl.BlockSpec((bb, 1, D), lambda i, j: (i, 0, sh_col)),
            pl.BlockSpec((bb, 1, D), lambda i, j: (i, 0, sc_col)),
            pl.BlockSpec((D, N), lambda i, j: (0, 0)),
        ],
        out_specs=[pl.BlockSpec((bb, L, b - a), lambda i, j: (i, j, 0)) for a, b in splits],
        out_shape=[jax.ShapeDtypeStruct((B, T, b - a), BF16 if q else F32) for (a, b), q in zip(splits, bf16_outs)],
        compiler_params=_cparams(("arbitrary", "arbitrary")),
        name=name,
    )(x, mod, mod, w)


def _outproj_kernel(ya_ref, yb_ref, yc_ref, x_ref, g_ref, w_ref, lg_ref, lb_ref, o_ref):
    x = x_ref[...]
    bb, L, D = x.shape
    rows = bb * L
    ya = ya_ref[...].reshape(rows, A_WIDTH).astype(BF16)
    yb = yb_ref[...].reshape(rows, B_WIDTH).astype(BF16)
    yc = yc_ref[...].reshape(rows, C_WIDTH).astype(BF16)
    y = (jnp.dot(ya, w_ref[0:A_WIDTH, :], preferred_element_type=F32)
         + jnp.dot(yb, w_ref[A_WIDTH:A_WIDTH + B_WIDTH, :], preferred_element_type=F32)
         + jnp.dot(yc, w_ref[A_WIDTH + B_WIDTH:, :], preferred_element_type=F32))
    z = ALPHA * x + (1.0 + g_ref[...]) * y.reshape(bb, L, D)
    o_ref[...] = _layernorm(z, lg_ref[...], lb_ref[...])


def _outproj(ya, yb, yc, x, mod, w_out, ln_g, ln_b, bb, L):
    B, T, D = x.shape
    blk = lambda w: pl.BlockSpec((bb, L, w), lambda i, j: (i, j, 0))
    return pl.pallas_call(
        _outproj_kernel,
        grid=(B // bb, T // L),
        in_specs=[
            blk(A_WIDTH), blk(B_WIDTH), blk(C_WIDTH), blk(D),
            pl.BlockSpec((bb, 1, D), lambda i, j: (i, 0, 2)),
            pl.BlockSpec((D, D), lambda i, j: (0, 0)),
            pl.BlockSpec((1, D), lambda i, j: (0, 0)),
            pl.BlockSpec((1, D), lambda i, j: (0, 0)),
        ],
        out_specs=blk(D),
        out_shape=jax.ShapeDtypeStruct((B, T, D), F32),
        compiler_params=_cparams(("arbitrary", "arbitrary")),
        name="outproj_ln",
    )(ya, yb, yc, x, mod, w_out, ln_g, ln_b)


def _ffn_down_kernel(ug_ref, uv_ref, halo_ref, st_ref, x_ref, g_ref, cw_ref, cb_ref, w_ref, lg_ref, lb_ref, o_ref):
    ug = ug_ref[...]
    bb, L, N = ug.shape
    first = pl.program_id(1) == 0
    pre = jnp.where(first, st_ref[...], halo_ref[...])
    u1, u2 = _shifted_rows(pre, ug, FFN_CONV - 1)
    cw = cw_ref[...]
    conv = cb_ref[...] + ug * cw[2:3, :] + u1 * cw[1:2, :] + u2 * cw[0:1, :]
    a = (_silu(conv) * uv_ref[...]).reshape(bb * L, N).astype(BF16)
    y = jnp.dot(a, w_ref[...], preferred_element_type=F32)
    x = x_ref[...]
    z = ALPHA * x + (1.0 + g_ref[...]) * y.reshape(x.shape)
    o_ref[...] = _layernorm(z, lg_ref[...], lb_ref[...])


def _ffn_down(u, st8, x, mod, conv_w, conv_b, w_down, ln_g, ln_b, bb, L):
    B, T, D = x.shape
    N = D_FF_PAD
    lb8 = L // SUBLANES
    return pl.pallas_call(
        _ffn_down_kernel,
        grid=(B // bb, T // L),
        in_specs=[
            pl.BlockSpec((bb, L, N), lambda i, j: (i, j, 0)),
            pl.BlockSpec((bb, L, N), lambda i, j: (i, j, 1)),
            pl.BlockSpec((bb, SUBLANES, N), lambda i, j: (i, jnp.maximum(j * lb8 - 1, 0), 0)),
            pl.BlockSpec((bb, SUBLANES, N), lambda i, j: (i, 0, 0)),
            pl.BlockSpec((bb, L, D), lambda i, j: (i, j, 0)),
            pl.BlockSpec((bb, 1, D), lambda i, j: (i, 0, 5)),
            pl.BlockSpec((SUBLANES, N), lambda i, j: (0, 0)),
            pl.BlockSpec((1, N), lambda i, j: (0, 0)),
            pl.BlockSpec((N, D), lambda i, j: (0, 0)),
            pl.BlockSpec((1, D), lambda i, j: (0, 0)),
            pl.BlockSpec((1, D), lambda i, j: (0, 0)),
        ],
        out_specs=pl.BlockSpec((bb, L, D), lambda i, j: (i, j, 0)),
        out_shape=jax.ShapeDtypeStruct((B, T, D), F32),
        compiler_params=_cparams(("arbitrary", "arbitrary")),
        name="ffn_down_ln",
    )(u, u, u, st8, x, mod, conv_w, conv_b, w_down, ln_g, ln_b)


def _unit_lower_inverses(n_mats, chunk):
    rows = n_mats[0].shape[0]
    eye = (_iota((rows, rows), 0) == _iota((rows, rows), 1)).astype(F32)
    ps = [n.astype(BF16) for n in n_mats]
    ts = [eye + n for n in n_mats]
    for _ in range(int(math.log2(chunk)) - 1):
        ps = [_dot(p, p).astype(BF16) for p in ps]
        ts = [t + _dot(p, t.astype(BF16)) for p, t in zip(ps, ts)]
    resids = [(eye - t) + _mm(n, t, 3) for n, t in zip(n_mats, ts)]
    return [t + _dot(t.astype(BF16), r.astype(BF16)) for t, r in zip(ts, resids)]


def _rwkv_kernel(pa_ref, halo_ref, sh_ref, st0_ref, mu_ref, pv_ref, lw_ref, ya_ref, sto_ref, st_ref, *, chunk):
    c = pl.program_id(1)
    bb, L, _ = pa_ref.shape
    R = bb * L
    W = A_WIDTH

    @pl.when(c == 0)
    def _():
        st_ref[...] = st0_ref[...]

    pa = pa_ref[...]
    pre = jnp.where(c == 0, sh_ref[...], halo_ref[...])
    (prev,) = _shifted_rows(pre, pa, 1)
    x = pa.reshape(R, 4 * W)
    pm = x + (prev.reshape(R, 4 * W) - x) * mu_ref[...]
    r = pm[:, 0:W]
    k = pm[:, W:2 * W]
    v = pm[:, 2 * W:3 * W]
    lo = pm[:, 3 * W:4 * W]
    lane = _iota((R, W), 1)
    z = jnp.where(lane < 64, jnp.tanh(lo), jnp.where(lane < 128, lo, _sigmoid(lo)))
    lora = _mm(z, lw_ref[...], 3)
    pv = pv_ref[...]
    w0, a0, k_k, k_a, r_k, ln_g, ln_b = (pv[i:i + 1, :] for i in range(7))
    w = -_softplus(-(w0 + lora[:, 0:W])) - 0.5
    a = _sigmoid(a0 + lora[:, W:2 * W])
    g = lora[:, 2 * W:3 * W]
    ones_h = _head_ones(W)
    kk = k * k_k
    kk = kk / jnp.maximum(jnp.sqrt(_mm_xsel(kk * kk, ones_h)), 1e-12)
    k2 = k * (1.0 + (a - 1.0) * k_a)
    lw = -jnp.exp(w)
    same, incl, strict = _block_masks(R, L)
    cum = _mm_sel(incl.astype(F32), lw)
    tot = _mm_sel(same.astype(F32), lw)
    e_neg = jnp.exp(-cum)
    e_rem = jnp.exp(tot - cum)
    kb = kk * a
    a_t = -kk * jnp.exp(cum - lw)
    b_t = kb * e_neg
    k_t = k2 * e_neg
    r_t = r * jnp.exp(cum)
    bh_t = jnp.transpose(kb * e_rem)
    kh_t = jnp.transpose(k2 * e_rem)
    gam_t = jnp.transpose(jnp.exp(tot))
    per_seq = bb <= 4
    if not per_seq:
        expand = _expand_mat(R, L, bb)
        lmask = _seq_lane_mask(R, L, bb)
        lmask2 = jnp.concatenate([lmask, lmask], axis=0)
        fold = _fold_mat(bb)
        tile = jnp.transpose(fold)
    heads = range(A_HEADS)
    hsl = [slice(h * HEAD_DIM, (h + 1) * HEAD_DIM) for h in heads]
    states = [st_ref[h] for h in heads]
    As, Bs, Ks, Rs, Vs = ([x[:, hs] for hs in hsl] for x in (a_t, b_t, k_t, r_t, v))
    m_abs = [jnp.where(strict, _mm(As[h], Bs[h], 3, NT), 0.0) for h in heads]
    t_invs = _unit_lower_inverses(m_abs, L)
    m_aks = [jnp.where(strict, _mm(As[h], Ks[h], 1, NT), 0.0) for h in heads]
    g_rs = [_mm(Rs[h], jnp.concatenate([Bs[h], Ks[h]], axis=0), 1, NT) for h in heads]
    m_rbks = [jnp.concatenate([jnp.where(incl, g[:, 0:R], 0.0), jnp.where(incl, g[:, R:2 * R], 0.0)], axis=1) for g in g_rs]
    if per_seq:
        pss = [[_mm(jnp.concatenate([As[h][b * L:(b + 1) * L], Rs[h][b * L:(b + 1) * L]], axis=0),
                    states[h][:, b * HEAD_DIM:(b + 1) * HEAD_DIM], 1) for b in range(bb)] for h in heads]
        ps_as = [jnp.concatenate([p[0:L] for p in ps], axis=0) for ps in pss]
        ps_rs = [jnp.concatenate([p[L:2 * L] for p in ps], axis=0) for ps in pss]
    else:
        pss = [_mm_xsel(jnp.where(lmask2, _mm(jnp.concatenate([As[h], Rs[h]], axis=0), states[h], 1), 0.0), fold) for h in heads]
        ps_as = [p[0:R] for p in pss]
        ps_rs = [p[R:2 * R] for p in pss]
    Ys = [ps_as[h] + _mm(m_aks[h], Vs[h], 1) for h in heads]
    Us = [_mm(t_invs[h], Ys[h], 3) for h in heads]
    UVs = [jnp.concatenate([Us[h], Vs[h]], axis=0) for h in heads]
    outs = [ps_rs[h] + _mm(m_rbks[h], UVs[h], 1) for h in heads]
    new_states = []
    for h in heads:
        hs = hsl[h]
        if per_seq:
            new = []
            for b in range(bb):
                ts = slice(b * L, (b + 1) * L)
                lhs = jnp.concatenate([bh_t[hs, ts], kh_t[hs, ts]], axis=1)
                uv_b = jnp.concatenate([Us[h][ts], Vs[h][ts]], axis=0)
                new.append(gam_t[hs, b * L:b * L + 1] * states[h][:, b * HEAD_DIM:(b + 1) * HEAD_DIM] + _mm(lhs, uv_b, 1))
            new_states.append(jnp.concatenate(new, axis=1))
        else:
            UVb = jnp.where(lmask2, _mm_xsel(UVs[h], tile), 0.0)
            lhs = jnp.concatenate([bh_t[hs, :], kh_t[hs, :]], axis=1)
            gam = _mm_xsel(gam_t[hs, :], expand)
            new_states.append(gam * states[h] + _mm(lhs, UVb, 1))
    for h in range(A_HEADS):
        st_ref[h] = new_states[h]
    o = jnp.concatenate(outs, axis=1)
    inv = 1.0 / HEAD_DIM
    mu = _mm_xsel(o, ones_h) * inv
    oc = o - mu
    var = _mm_xsel(oc * oc, ones_h) * inv
    y = oc * lax.rsqrt(var + 64e-5) * ln_g + ln_b
    y = y + _mm_xsel(r * k2 * r_k, ones_h) * v
    ya_ref[...] = (y * g).reshape(bb, L, W)

    @pl.when(c == pl.num_programs(1) - 1)
    def _():
        sto_ref[...] = st_ref[...]


def _rwkv(pa, shift8, st0, mu, pvec, lora_w, bb, L):
    B, T, _ = pa.shape
    lb8 = L // SUBLANES
    kern = functools.partial(_rwkv_kernel, chunk=L)
    return pl.pallas_call(
        kern,
        grid=(B // bb, T // L),
        in_specs=[
            pl.BlockSpec((bb, L, 4 * A_WIDTH), lambda i, j: (i, j, 0)),
            pl.BlockSpec((bb, SUBLANES, 4 * A_WIDTH), lambda i, j: (i, jnp.maximum(j * lb8 - 1, 0), 0)),
            pl.BlockSpec((bb, SUBLANES, 4 * A_WIDTH), lambda i, j: (i, 0, 0)),
            pl.BlockSpec((A_HEADS, HEAD_DIM, bb * HEAD_DIM), lambda i, j: (0, 0, i)),
            pl.BlockSpec((1, 4 * A_WIDTH), lambda i, j: (0, 0)),
            pl.BlockSpec((SUBLANES, A_WIDTH), lambda i, j: (0, 0)),
            pl.BlockSpec((A_WIDTH, 3 * A_WIDTH), lambda i, j: (0, 0)),
        ],
        out_specs=[
            pl.BlockSpec((bb, L, A_WIDTH), lambda i, j: (i, j, 0)),
            pl.BlockSpec((A_HEADS, HEAD_DIM, bb * HEAD_DIM), lambda i, j: (0, 0, i)),
        ],
        out_shape=[
            jax.ShapeDtypeStruct((B, T, A_WIDTH), F32),
            jax.ShapeDtypeStruct((A_HEADS, HEAD_DIM, B * HEAD_DIM), F32),
        ],
        scratch_shapes=[pltpu.VMEM((A_HEADS, HEAD_DIM, bb * HEAD_DIM), F32)],
        compiler_params=_cparams(("arbitrary", "arbitrary")),
        name="rwkv7",
    )(pa, pa, shift8, st0, mu, pvec, lora_w)


def _prep_rwkv(mu, w0, w2, a0, a2, g2, k_k, k_a, r_k, ln_g, ln_b):
    zero = jnp.zeros((A_WIDTH,), F32)
    pvec = jnp.stack([w0, a0, k_k, k_a, r_k.reshape(A_WIDTH), ln_g, ln_b, zero])
    lora = jnp.zeros((A_WIDTH, 3 * A_WIDTH), F32)
    lora = lora.at[0:64, 0:A_WIDTH].set(w2)
    lora = lora.at[64:128, A_WIDTH:2 * A_WIDTH].set(a2)
    lora = lora.at[128:256, 2 * A_WIDTH:].set(g2)
    return mu.reshape(1, 4 * A_WIDTH), pvec, lora


def _state_to_lanes(s):
    B, H = s.shape[:2]
    return jnp.transpose(s, (1, 3, 0, 2)).reshape(H, HEAD_DIM, B * HEAD_DIM)


def _state_from_lanes(st, B):
    H = st.shape[0]
    return jnp.transpose(st.reshape(H, HEAD_DIM, B, HEAD_DIM), (2, 0, 3, 1))


def _mlstm_kernel(pb_ref, halo_ref, cv_ref, pif_ref, gt_ref, ct0_ref, n0_ref, m0_ref, cw_ref, cb_ref, bif_ref,
                  brow_ref, ng_ref, yb_ref, cto_ref, no_ref, mo_ref, ct_ref, nt_ref, m_ref):
    c = pl.program_id(1)
    bb, L, _ = pb_ref.shape
    R = bb * L
    W = B_WIDTH
    neg = -jnp.inf

    @pl.when(c == 0)
    def _():
        ct_ref[...] = ct0_ref[...]
        nt_ref[...] = n0_ref[...]
        m_ref[...] = jnp.broadcast_to(m0_ref[...], m_ref.shape)

    pb = pb_ref[...]
    qk_in = pb[:, :, 0:2 * W]
    pre = jnp.where(c == 0, cv_ref[...], halo_ref[...])
    s1, s2, s3 = _shifted_rows(pre, qk_in, MLSTM_CONV - 1)
    cw = cw_ref[...]
    conv = cb_ref[...] + qk_in * cw[3:4, :] + s1 * cw[2:3, :] + s2 * cw[1:2, :] + s3 * cw[0:1, :]
    qk = _silu(conv).reshape(R, 2 * W)
    q = qk[:, 0:W]
    k = qk[:, W:2 * W] * (HEAD_DIM ** -0.5)
    v = pb[:, :, 2 * W:3 * W].reshape(R, W)
    og = pb[:, :, 3 * W:4 * W].reshape(R, W)

    same, incl, _ = _block_masks(R, L)
    same_f = same.astype(F32)
    incl_f = incl.astype(F32)
    gc = pif_ref[...].reshape(R, LANES) + bif_ref[...]
    lane = _iota((R, LANES), 1)
    lfc = jnp.where((lane >= B_HEADS) & (lane < 2 * B_HEADS), _log_sigmoid(gc), 0.0)
    bcum_c = _mm_sel(incl_f, lfc)
    btot_c = _mm_sel(same_f, lfc)
    gr = gt_ref[...] + brow_ref[...]
    row = _iota((SUBLANES, R), 0)
    lfr = jnp.where(row >= B_HEADS, _log_sigmoid(gr), 0.0)
    bcum_r = _mm_xsel(lfr, incl_f, NT)
    btot_r = _mm_xsel(lfr, same_f)
    m_col = jnp.broadcast_to(m_ref[:, 0:1, :], (bb, L, LANES)).reshape(R, LANES)

    per_seq = bb <= 4
    if not per_seq:
        lmask = _seq_lane_mask(R, L, bb)
        fold = _fold_mat(bb)
        tile = jnp.transpose(fold)
        expand = _expand_mat(R, L, bb)
    sh = int(math.log2(L))
    rl = _iota((R, LANES), 0)
    blockind = (jnp.right_shift(rl, sh) == lane).astype(F32)
    firstind = ((jnp.right_shift(rl, sh) == lane) & ((rl & (L - 1)) == 0)).astype(F32)

    houts, kws, wcs = [], [], []
    cts = [ct_ref[h] for h in range(B_HEADS)]
    nts = [nt_ref[h] for h in range(B_HEADS)]
    m_new_all = jnp.zeros((R, LANES), F32)
    heads = range(B_HEADS)
    hsl = [slice(h * HEAD_DIM, (h + 1) * HEAD_DIM) for h in heads]
    qks = [_mm(q[:, hs], k[:, hs], 1, NT) for hs in hsl]
    if per_seq:
        QCs = [jnp.concatenate([_mm(q[b * L:(b + 1) * L, hsl[h]], cts[h][:, b * HEAD_DIM:(b + 1) * HEAD_DIM], 1)
                                for b in range(bb)], axis=0) for h in heads]
    else:
        QCs = [_mm_xsel(jnp.where(lmask, _mm(q[:, hsl[h]], cts[h], 1), 0.0), fold) for h in heads]
    qns = [jnp.sum(_mm(q[:, hsl[h]], nts[h], 3) * blockind, axis=1, keepdims=True) for h in heads]
    Ds = [jnp.where(incl, bcum_c[:, B_HEADS + h:B_HEADS + h + 1] - bcum_r[B_HEADS + h:B_HEADS + h + 1, :] + gr[h:h + 1, :], neg)
          for h in heads]
    inters = [bcum_c[:, B_HEADS + h:B_HEADS + h + 1] + m_col[:, h:h + 1] for h in heads]
    mts = [jnp.maximum(inters[h], jnp.max(Ds[h], axis=1, keepdims=True)) for h in heads]
    Ss = [qks[h] * jnp.exp(Ds[h] - mts[h]) for h in heads]
    svs = [_mm(Ss[h], v[:, hsl[h]], 1) for h in heads]
    for h in heads:
        hs = hsl[h]
        K = k[:, hs]
        b_c = bcum_c[:, B_HEADS + h:B_HEADS + h + 1]
        b_r = bcum_r[B_HEADS + h:B_HEADS + h + 1, :]
        i_r = gr[h:h + 1, :]
        i_c = gc[:, h:h + 1]
        m_c = m_col[:, h:h + 1]
        mt = mts[h]
        iw = jnp.exp(inters[h] - mt)
        num = svs[h] + iw * QCs[h]
        den = jnp.sum(Ss[h], axis=1, keepdims=True) + iw * qns[h]
        houts.append(num / jnp.maximum(jnp.abs(den), jnp.exp(-mt)))
        bl_c = btot_c[:, B_HEADS + h:B_HEADS + h + 1]
        bl_r = btot_r[B_HEADS + h:B_HEADS + h + 1, :]
        gs_c = bl_c - b_c + i_c
        gmax = jnp.max(jnp.where(same, bl_r - b_r + i_r, neg), axis=1, keepdims=True)
        m_new = jnp.maximum(bl_c + m_c, gmax)
        kws.append(K * jnp.exp(gs_c - m_new))
        wcs.append(jnp.exp(bl_c + m_c - m_new))
        m_new_all = jnp.where(lane == h, m_new, m_new_all)

    kw_t = jnp.transpose(jnp.concatenate(kws, axis=1))
    new_c, new_n = [], []
    for h in range(B_HEADS):
        hs = slice(h * HEAD_DIM, (h + 1) * HEAD_DIM)
        ct_h = cts[h]
        if per_seq:
            new_c.append(jnp.concatenate(
                [wcs[h][b * L:b * L + 1, :] * ct_h[:, b * HEAD_DIM:(b + 1) * HEAD_DIM]
                 + _mm(kw_t[hs, b * L:(b + 1) * L], v[b * L:(b + 1) * L, hs], 1) for b in range(bb)], axis=1))
        else:
            vb = jnp.where(lmask, _mm_xsel(v[:, hs], tile), 0.0)
            wc_row = jnp.sum(wcs[h] * expand, axis=0, keepdims=True)
            new_c.append(wc_row * ct_h + _mm(kw_t[hs, :], vb, 1))
        wc_lane = jnp.sum(wcs[h] * firstind, axis=0, keepdims=True)
        new_n.append(wc_lane * nts[h] + _mm_xsel(kw_t[hs, :], blockind))
    for h in range(B_HEADS):
        ct_ref[h] = new_c[h]
        nt_ref[h] = new_n[h]
    m_ref[...] = m_new_all.reshape(bb, L, LANES)[:, 0:SUBLANES, :]

    hcat = jnp.concatenate(houts, axis=1)
    ones_h = _head_ones(W)
    inv = 1.0 / HEAD_DIM
    mu = _mm_xsel(hcat, ones_h) * inv
    hc = hcat - mu
    var = _mm_xsel(hc * hc, ones_h) * inv
    hn = hc * lax.rsqrt(var + 1e-5) * ng_ref[...]
    yb_ref[...] = (_sigmoid(og) * hn).reshape(bb, L, W)

    @pl.when(c == pl.num_programs(1) - 1)
    def _():
        cto_ref[...] = ct_ref[...]
        no_ref[...] = nt_ref[...]
        mo_ref[...] = m_ref[...]


def _mlstm(pb, conv8, pif, ct0, n0, m0, conv_w8, conv_b, bias_if, norm_g, bb, L):
    B, T, _ = pb.shape
    R = bb * L
    nbi, nch = B // bb, T // L
    lb8 = L // SUBLANES
    g_t = pif[:, :, 0:SUBLANES].reshape(nbi, bb, nch, L, SUBLANES).transpose(0, 2, 4, 1, 3).reshape(nbi, nch, SUBLANES, R)
    bias_row = jnp.broadcast_to(bias_if[0, 0:SUBLANES].reshape(SUBLANES, 1), (SUBLANES, R))
    n_in = jnp.pad(n0.reshape(nbi, bb, B_HEADS, HEAD_DIM).transpose(0, 2, 3, 1), ((0, 0), (0, 0), (0, 0), (0, LANES - bb)))
    m_in = jnp.pad(m0, ((0, 0), (0, LANES - B_HEADS))).reshape(B, 1, LANES)
    yb, ct, nt, mo = pl.pallas_call(
        _mlstm_kernel,
        grid=(nbi, nch),
        in_specs=[
            pl.BlockSpec((bb, L, 4 * B_WIDTH), lambda i, j: (i, j, 0)),
            pl.BlockSpec((bb, SUBLANES, 2 * B_WIDTH), lambda i, j: (i, jnp.maximum(j * lb8 - 1, 0), 0)),
            pl.BlockSpec((bb, SUBLANES, 2 * B_WIDTH), lambda i, j: (i, 0, 0)),
            pl.BlockSpec((bb, L, LANES), lambda i, j: (i, j, 0)),
            pl.BlockSpec((None, None, SUBLANES, R), lambda i, j: (i, j, 0, 0)),
            pl.BlockSpec((B_HEADS, HEAD_DIM, bb * HEAD_DIM), lambda i, j: (0, 0, i)),
            pl.BlockSpec((None, B_HEADS, HEAD_DIM, LANES), lambda i, j: (i, 0, 0, 0)),
            pl.BlockSpec((bb, 1, LANES), lambda i, j: (i, 0, 0)),
            pl.BlockSpec((SUBLANES, 2 * B_WIDTH), lambda i, j: (0, 0)),
            pl.BlockSpec((1, 2 * B_WIDTH), lambda i, j: (0, 0)),
            pl.BlockSpec((1, LANES), lambda i, j: (0, 0)),
            pl.BlockSpec((SUBLANES, R), lambda i, j: (0, 0)),
            pl.BlockSpec((1, B_WIDTH), lambda i, j: (0, 0)),
        ],
        out_specs=[
            pl.BlockSpec((bb, L, B_WIDTH), lambda i, j: (i, j, 0)),
            pl.BlockSpec((B_HEADS, HEAD_DIM, bb * HEAD_DIM), lambda i, j: (0, 0, i)),
            pl.BlockSpec((None, B_HEADS, HEAD_DIM, LANES), lambda i, j: (i, 0, 0, 0)),
            pl.BlockSpec((bb, SUBLANES, LANES), lambda i, j: (i, 0, 0)),
        ],
        out_shape=[
            jax.ShapeDtypeStruct((B, T, B_WIDTH), F32),
            jax.ShapeDtypeStruct((B_HEADS, HEAD_DIM, B * HEAD_DIM), F32),
            jax.ShapeDtypeStruct((nbi, B_HEADS, HEAD_DIM, LANES), F32),
            jax.ShapeDtypeStruct((B, SUBLANES, LANES), F32),
        ],
        scratch_shapes=[
            pltpu.VMEM((B_HEADS, HEAD_DIM, bb * HEAD_DIM), F32),
            pltpu.VMEM((B_HEADS, HEAD_DIM, LANES), F32),
            pltpu.VMEM((bb, SUBLANES, LANES), F32),
        ],
        compiler_params=_cparams(("arbitrary", "arbitrary")),
        name="mlstm",
    )(pb, pb, conv8, pif, g_t, ct0, n_in, m_in, conv_w8, conv_b, bias_if, bias_row, norm_g)
    n_new = nt[:, :, :, 0:bb].transpose(0, 3, 1, 2).reshape(B, B_HEADS, HEAD_DIM)
    return yb, ct, n_new, mo[:, 0, 0:B_HEADS]


def _masked_softmax(s, mask):
    mx = jnp.max(jnp.where(mask, s, NEG_INF), axis=1, keepdims=True)
    e = jnp.where(mask, jnp.exp(s - mx), 0.0)
    return e / jnp.maximum(jnp.sum(e, axis=1, keepdims=True), 1e-30)


def _select_blocks(psums, m_mat, t_col, n_s, blocks_major=False):
    tq = psums[0].shape[0]
    imps = [_mm_xsel(p, m_mat) for p in psums]
    blk = _iota((tq, LANES), 1)
    cur = jnp.right_shift(t_col, 6)
    valid = (blk * SEL_BLOCK <= t_col)
    bonus = jnp.where((blk == 0) | (blk == cur) | (blk == cur - 1), FORCE_BONUS, 0.0)
    scores = [jnp.where(blk < n_s, jnp.where(valid, imp + bonus, NEG_INF), -jnp.inf) for imp in imps]
    n_sel = min(SEL_TOP, n_s)
    if tq == LANES and n_s % SUBLANES == 0:
        scs = [jnp.transpose(sc)[0:n_s, :] for sc in scores]
        idx = _iota((n_s, tq), 0)
        ranks = [jnp.zeros((n_s, tq), F32) for _ in scs]
        for s in range(n_s):
            for c, sc in enumerate(scs):
                row = sc[s:s + 1, :]
                ranks[c] = ranks[c] + jnp.where((row > sc) | ((row == sc) & (idx > s)), 1.0, 0.0)
        sel_ts = [jnp.where(r < n_sel, 1.0, 0.0) for r in ranks]
        if n_s < LANES:
            sel_ts = [jnp.concatenate([st, jnp.zeros((LANES - n_s, tq), F32)], axis=0) for st in sel_ts]
        return sel_ts if blocks_major else [jnp.transpose(st) for st in sel_ts]
    ranks = [jnp.zeros((tq, LANES), F32) for _ in scores]
    for s in range(n_s):
        for c, sc in enumerate(scores):
            col = sc[:, s:s + 1]
            ranks[c] = ranks[c] + jnp.where((col > sc) | ((col == sc) & (blk > s)), 1.0, 0.0)
    sels = [jnp.where(r < n_sel, 1.0, 0.0) for r in ranks]
    return [jnp.transpose(x) for x in sels] if blocks_major else sels


def _stack_heads(pq, g):
    base = g * C_GROUP * HEAD_DIM
    parts = [pq[:, base + r * HEAD_DIM: base + (r + 1) * HEAD_DIM] for r in range(C_GROUP)]
    return jnp.concatenate(parts, axis=0) * (HEAD_DIM ** -0.5)


def _to_group_lanes(q, g):
    z = jnp.zeros_like(q)
    return jnp.concatenate([q, z] if g == 0 else [z, q], axis=1)


def _compress_kernel(x_ref, pw0_ref, pw1_ref, cw_ref, kc_ref, a_ref, *, rows_per_step):
    T = x_ref.shape[0]
    rs = rows_per_step
    ng = rs // CMP_STRIDE
    pool = (jnp.right_shift(_iota((ng, rs), 1), 4) == _iota((ng, rs), 0)).astype(F32)
    for c in range(T // rs):
        x = x_ref[c * rs:(c + 1) * rs, :]
        xw = jnp.concatenate([x * pw0_ref[...], x * pw1_ref[...]], axis=1)
        a_ref[c * ng:(c + 1) * ng, :] = _mm_sel(pool, xw)
    ngrp = T // CMP_STRIDE
    W = x_ref.shape[1]
    pooled = a_ref[:, 0:W] + pltpu.roll(a_ref[:, W:2 * W], ngrp - 1, axis=0)
    kc_ref[...] = _mm(pooled, cw_ref[...], 3)


def _compress(pcmp, pwt0, pwt1, cmpw_bd):
    B, T, W = pcmp.shape
    rs = pwt0.shape[0]
    ngrp = T // CMP_STRIDE
    kern = functools.partial(_compress_kernel, rows_per_step=rs)
    return pl.pallas_call(
        kern,
        grid=(B,),
        in_specs=[
            pl.BlockSpec((None, T, W), lambda b: (b, 0, 0)),
            pl.BlockSpec((rs, W), lambda b: (0, 0)),
            pl.BlockSpec((rs, W), lambda b: (0, 0)),
            pl.BlockSpec((W, W), lambda b: (0, 0)),
        ],
        out_specs=pl.BlockSpec((None, ngrp, W), lambda b: (b, 0, 0)),
        out_shape=jax.ShapeDtypeStruct((B, ngrp, W), F32),
        scratch_shapes=[pltpu.VMEM((ngrp, 2 * W), F32)],
        compiler_params=_cparams(("arbitrary",)),
        name="nsa_compress",
    )(pcmp, pwt0, pwt1, cmpw_bd)


def _combine_branches(gates, g, o_c, o_s, o_w, tq):
    outs = []
    for r in range(C_GROUP):
        rs = slice(r * tq, (r + 1) * tq)
        j = (g * C_GROUP + r) * 3
        outs.append(gates[:, j:j + 1] * o_c[rs] + gates[:, j + 1:j + 2] * o_s[rs] + gates[:, j + 2:j + 3] * o_w[rs])
    return outs


def _nsa_prompt_kernel(pq_ref, pg_ref, gb_ref, kc_ref, kv_ref, vt_ref, m_ref, et_ref, wb_ref, o_ref, *, n_s):
    i = pl.program_id(1)
    tq = pq_ref.shape[0]
    ngrp = kc_ref.shape[0]
    kt = 4 * tq
    t0 = i * tq
    rows = C_GROUP * tq
    tl = _iota((rows, 1), 0) & (tq - 1)
    t_row = t0 + tl
    t_col = t0 + _iota((tq, 1), 0)
    pq = pq_ref[...]
    gates = _sigmoid(pg_ref[...] + gb_ref[...])
    kc = kc_ref[...]
    n_end = _iota((rows, ngrp), 1) * CMP_STRIDE + (CMP_BLOCK - 1)
    cmask = n_end <= t_row
    n_full = t0 // kt
    tl_lane = _iota((1, rows), 1) & (tq - 1)
    diag_bias = jnp.where(_iota((kt, rows), 0) <= (t0 - n_full * kt) + tl_lane, 0.0, NEG_INF)
    n_tiles = WINDOW // tq + 1
    groups = range(C_KV_HEADS)
    qs = [_stack_heads(pq, g) for g in groups]
    q2fs = [_to_group_lanes(qs[g], g) for g in groups]
    q2s = [x.astype(BF16) for x in q2fs]
    q2ts = [jnp.transpose(x).astype(BF16) for x in q2fs]
    p_cs = [_masked_softmax(_mm(qs[g], kc[:, g * HEAD_DIM:(g + 1) * HEAD_DIM], 3, NT), cmask) for g in groups]
    o_cs = [_mm(p_cs[g], kc[:, 2 * HEAD_DIM + g * HEAD_DIM: 2 * HEAD_DIM + (g + 1) * HEAD_DIM], 1) for g in groups]
    psums = [functools.reduce(lambda a, b: a + b, [p[r * tq:(r + 1) * tq] for r in range(C_GROUP)]) for p in p_cs]
    sel_ts = _select_blocks(psums, m_ref[...], t_col, n_s, blocks_major=True)
    q_exts = [jnp.concatenate([q2ts[g], jnp.concatenate([((sel_ts[g] - 1.0) * (-NEG_INF)).astype(BF16)] * C_GROUP, axis=1)],
                              axis=0) for g in groups]

    def step(j, carries, extra):
        off = pl.multiple_of(j * kt, kt)
        kk = kv_ref[pl.ds(off, kt), 0:LANES]
        vv_t = vt_ref[:, pl.ds(off, kt)]
        e_t = et_ref[pl.ds(off, kt), :]
        kk_ext = jnp.concatenate([kk, e_t], axis=1)
        ss = [_dot(kk_ext, q_exts[g]) for g in groups]
        if extra is not None:
            ss = [s + extra for s in ss]
        m_news = [jnp.maximum(carries[g][0], jnp.max(ss[g], axis=0, keepdims=True)) for g in groups]
        ps = [jnp.exp(ss[g] - m_news[g]) for g in groups]
        alphas = [jnp.exp(carries[g][0] - m_news[g]) for g in groups]
        ls = [alphas[g] * carries[g][1] + jnp.sum(ps[g], axis=0, keepdims=True) for g in groups]
        accs = [alphas[g] * carries[g][2] + _dot(vv_t, ps[g].astype(BF16)) for g in groups]
        return tuple((m_news[g], ls[g], accs[g]) for g in groups)

    init1 = (jnp.full((1, rows), NEG_INF, F32), jnp.zeros((1, rows), F32), jnp.zeros((LANES, rows), F32))
    carries = lax.fori_loop(0, n_full, lambda j, c: step(j, c, None), tuple(init1 for _ in groups))
    carries = step(n_full, carries, diag_bias)

    gsl = [slice(g * HEAD_DIM, (g + 1) * HEAD_DIM) for g in groups]
    o_ss = [jnp.transpose(carries[g][2] / carries[g][1])[:, gsl[g]] for g in groups]

    kws, vws, wbs = [], [], []
    for cidx in range(n_tiles):
        tile = i - (n_tiles - 1) + cidx
        off = pl.multiple_of(jnp.maximum(tile, 0) * tq, tq)
        kws.append(kv_ref[pl.ds(off, tq), 2 * LANES:3 * LANES])
        vws.append(kv_ref[pl.ds(off, tq), 3 * LANES:4 * LANES])
        wbs.append(wb_ref[:, cidx * tq:(cidx + 1) * tq] + jnp.where(tile >= 0, 0.0, NEG_INF))
    vw = jnp.concatenate(vws, axis=0)
    s_ws = [jnp.concatenate([_dot(q2s[g], kws[c], NT) + wbs[c] for c in range(n_tiles)], axis=1) for g in groups]
    e_ws = [jnp.exp(s - jnp.max(s, axis=1, keepdims=True)) for s in s_ws]
    p_ws = [e / jnp.sum(e, axis=1, keepdims=True) for e in e_ws]
    o_ws = [_dot(p_ws[g].astype(BF16), vw)[:, gsl[g]] for g in groups]
    pieces = []
    for g in groups:
        pieces += _combine_branches(gates, g, o_cs[g], o_ss[g], o_ws[g], tq)
    o_ref[...] = jnp.concatenate(pieces, axis=1)


def _nsa_prompt(pq, pg, gate_b, kc, kvb, m_mat, e_mat, tq):
    B, T, _ = pq.shape
    ngrp = kc.shape[1]
    n_s = T // SEL_BLOCK
    n_tiles = WINDOW // tq + 1
    kp = np.arange(n_tiles * tq)[None, :] - (n_tiles - 1) * tq
    tloc = (np.arange(C_GROUP * tq) % tq)[:, None]
    wbias = jnp.asarray(np.where((kp <= tloc) & (kp > tloc - WINDOW), 0.0, NEG_INF).astype(np.float32))
    kern = functools.partial(_nsa_prompt_kernel, n_s=n_s)
    return pl.pallas_call(
        kern,
        grid=(B, T // tq),
        in_specs=[
            pl.BlockSpec((None, tq, C_WIDTH), lambda b, i: (b, i, 0)),
            pl.BlockSpec((None, tq, LANES), lambda b, i: (b, i, 0)),
            pl.BlockSpec((1, LANES), lambda b, i: (0, 0)),
            pl.BlockSpec((None, ngrp, 4 * HEAD_DIM), lambda b, i: (b, 0, 0)),
            pl.BlockSpec((None, T, 4 * LANES), lambda b, i: (b, 0, 0)),
            pl.BlockSpec((None, LANES, T), lambda b, i: (b, 0, 0)),
            pl.BlockSpec((ngrp, LANES), lambda b, i: (0, 0)),
            pl.BlockSpec((T, LANES), lambda b, i: (0, 0)),
            pl.BlockSpec((C_GROUP * tq, n_tiles * tq), lambda b, i: (0, 0)),
        ],
        out_specs=pl.BlockSpec((None, tq, C_WIDTH), lambda b, i: (b, i, 0)),
        out_shape=jax.ShapeDtypeStruct((B, T, C_WIDTH), F32),
        compiler_params=_cparams(("arbitrary", "arbitrary")),
        name="nsa_prompt",
    )(pq, pg, gate_b, kc, kvb, jnp.swapaxes(kvb[:, :, LANES:2 * LANES], 1, 2), m_mat, jnp.transpose(e_mat), wbias)


def _nsa_sample_kernel(pt_ref, *refs, n_pages, nseq, past_len, n_s, n_c):
    (pq_ref, pg_ref, gb_ref, pcmp_ref, pslc_ref, pwin_ref, win_ref, pw0_ref, pw1_ref, cw_ref, m_ref, e_ref,
     o_ref) = refs[nseq * n_pages:]
    seqs = range(nseq)
    pages = [refs[s * n_pages:(s + 1) * n_pages] for s in seqs]
    tq = pq_ref.shape[1]
    rows = C_GROUP * tq
    W = 4 * HEAD_DIM
    ng = PAGE_SIZE // CMP_STRIDE
    ngrp = m_ref.shape[0]
    pw0 = pw0_ref[...]
    pw1 = pw1_ref[...]

    def tail_t(x):
        return jnp.transpose(jnp.concatenate([x, jnp.zeros((PAGE_SIZE - tq, x.shape[1]), F32)], axis=0))

    xs = [[pages[s][p][0:W, :] for p in range(n_pages)] + [tail_t(pcmp_ref[s])] for s in seqs]
    a_acc = jnp.zeros((nseq * 2 * W, ngrp), F32)
    for p0 in range(0, n_pages + 1, 2):
        xw = jnp.concatenate(
            [jnp.concatenate([jnp.concatenate([x * pw0, x * pw1], axis=0) for x in xs[s][p0:p0 + 2]], axis=1)
             for s in seqs], axis=0)
        kdim = xw.shape[1]
        sel = (jnp.right_shift(_iota((kdim, ngrp), 0), 4) + p0 * ng == _iota((kdim, ngrp), 1)).astype(BF16)
        a_acc = a_acc + _dot(xw.astype(BF16), sel)
    pooled_ts = [a_acc[s * 2 * W:s * 2 * W + W, :] + pltpu.roll(a_acc[s * 2 * W + W:(s + 1) * 2 * W, :], ngrp - 1, axis=1)
                 for s in seqs]
    kc_all = _mm(cw_ref[...], jnp.concatenate(pooled_ts, axis=1), 3)
    kc_ts = [kc_all[:, s * ngrp:(s + 1) * ngrp] for s in seqs]
    pslc_ts = [tail_t(pslc_ref[s]) for s in seqs]
    k_ts = [jnp.concatenate([pages[s][p][W:W + LANES, :].astype(BF16) for p in range(n_pages)]
                            + [pslc_ts[s][0:LANES].astype(BF16)], axis=1) for s in seqs]
    v_ts = [jnp.concatenate([pages[s][p][W + LANES:W + 2 * LANES, :].astype(BF16) for p in range(n_pages)]
                            + [pslc_ts[s][LANES:2 * LANES].astype(BF16)], axis=1) for s in seqs]
    nk = k_ts[0].shape[1]
    t_row = past_len + (_iota((rows, 1), 0) & (tq - 1))
    t_col = past_len + _iota((tq, 1), 0)
    n_idx = _iota((rows, ngrp), 1)
    cmask = (n_idx * CMP_STRIDE + (CMP_BLOCK - 1) <= t_row) & (n_idx < n_c)
    kpos = _iota((rows, nk), 1)
    nwb = win_ref.shape[2]
    pwin_ts = [tail_t(pwin_ref[s]) for s in seqs]
    kw_ts = [jnp.concatenate([win_ref[s, 0:LANES, :], pwin_ts[s][0:LANES]], axis=1).astype(BF16) for s in seqs]
    vw_ts = [jnp.concatenate([win_ref[s, LANES:2 * LANES, :], pwin_ts[s][LANES:2 * LANES]], axis=1).astype(BF16)
             for s in seqs]
    jj = _iota((rows, nwb + PAGE_SIZE), 1)
    tl = _iota((rows, 1), 0) & (tq - 1)
    wmask = (jj > tl + (nwb - WINDOW)) & (jj <= tl + nwb) & (jj < nwb + tq)

    chains = [(s, g) for s in seqs for g in range(C_KV_HEADS)]
    gsl = [slice(g * HEAD_DIM, (g + 1) * HEAD_DIM) for g in range(C_KV_HEADS)]
    qs = [_stack_heads(pq_ref[s], g) for s, g in chains]
    q2s = [_to_group_lanes(q, g).astype(BF16) for q, (s, g) in zip(qs, chains)]
    p_cs = [_masked_softmax(_mm(q, kc_ts[s][gsl[g], :], 3), cmask) for q, (s, g) in zip(qs, chains)]
    o_cs = [_mm(p, kc_ts[s][2 * HEAD_DIM + g * HEAD_DIM: 2 * HEAD_DIM + (g + 1) * HEAD_DIM, :], 1, NT)
            for p, (s, g) in zip(p_cs, chains)]
    psums = [functools.reduce(lambda a, b: a + b, [p[r * tq:(r + 1) * tq] for r in range(C_GROUP)]) for p in p_cs]
    sels = _select_blocks(psums, m_ref[...], t_col, n_s)
    e_mat = e_ref[...]
    smasks = [(_dot(jnp.concatenate([sel] * C_GROUP, axis=0).astype(BF16), e_mat) > 0.5) & (kpos <= t_row)
              for sel in sels]
    p_ss = [_masked_softmax(_dot(q2, k_ts[s]), m) for q2, m, (s, g) in zip(q2s, smasks, chains)]
    o_ss = [_dot(p.astype(BF16), v_ts[s], NT)[:, gsl[g]] for p, (s, g) in zip(p_ss, chains)]
    p_ws = [_masked_softmax(_dot(q2, kw_ts[s]), wmask) for q2, (s, g) in zip(q2s, chains)]
    o_ws = [_dot(p.astype(BF16), vw_ts[s], NT)[:, gsl[g]] for p, (s, g) in zip(p_ws, chains)]
    for s in seqs:
        gates = _sigmoid(pg_ref[s] + gb_ref[...])
        pieces = []
        for c, (cs, g) in enumerate(chains):
            if cs == s:
                pieces += _combine_branches(gates, g, o_cs[c], o_ss[c], o_ws[c], tq)
        o_ref[s] = jnp.concatenate(pieces, axis=1)


def _nsa_sample(layer, page_table, cache_t, pq, pg, gate_b, pcmp, pslc, pwin, win_t, pwt0, pwt1, cmpw_t, m_mat, e_mat):
    B, T, _ = pq.shape
    nseq = 2 if B % 2 == 0 else 1
    n_pages = page_table.shape[1]
    past_len = n_pages * PAGE_SIZE
    lp = -(-(past_len + T) // SEL_BLOCK) * SEL_BLOCK
    n_s = lp // SEL_BLOCK
    n_c = lp // CMP_STRIDE - CMP_BLOCK // CMP_STRIDE + 1
    ngrp = m_mat.shape[0]
    nk = past_len + PAGE_SIZE
    nwb = win_t.shape[3]
    kern = functools.partial(_nsa_sample_kernel, n_pages=n_pages, nseq=nseq, past_len=past_len, n_s=n_s, n_c=n_c)
    page_specs = [
        pl.BlockSpec((None, None, 4 * LANES, PAGE_SIZE),
                     functools.partial(lambda b, pt, s, p: (layer, pt[b * nseq + s, p], 0, 0), s=s, p=p))
        for s in range(nseq) for p in range(n_pages)
    ]
    row = lambda w: pl.BlockSpec((nseq, T, w), lambda b, pt: (b, 0, 0))
    full = lambda a: pl.BlockSpec(a.shape, lambda b, pt: (0,) * a.ndim)
    grid_spec = pltpu.PrefetchScalarGridSpec(
        num_scalar_prefetch=1,
        grid=(B // nseq,),
        in_specs=page_specs + [
            row(C_WIDTH), row(LANES), full(gate_b), row(4 * HEAD_DIM), row(4 * HEAD_DIM), row(4 * HEAD_DIM),
            pl.BlockSpec((None, nseq, 4 * HEAD_DIM, nwb), lambda b, pt: (layer, b, 0, 0)),
            full(pwt0), full(pwt1), full(cmpw_t), full(m_mat), full(e_mat),
        ],
        out_specs=pl.BlockSpec((nseq, T, C_WIDTH), lambda b, pt: (b, 0, 0)),
    )
    return pl.pallas_call(
        kern,
        grid_spec=grid_spec,
        out_shape=jax.ShapeDtypeStruct((B, T, C_WIDTH), F32),
        compiler_params=_cparams(("arbitrary",)),
        name="nsa_sample",
    )(page_table, *([cache_t] * (nseq * n_pages)), pq, pg, gate_b, pcmp, pslc, pwin, win_t, pwt0, pwt1, cmpw_t, m_mat, e_mat)


def _nsa_consts(lp, n_keys):
    n_str = lp // CMP_STRIDE
    n_c = n_str - CMP_BLOCK // CMP_STRIDE + 1
    n_s = lp // SEL_BLOCK
    c0 = np.arange(n_str)[:, None] * CMP_STRIDE
    s0 = np.arange(LANES)[None, :] * SEL_BLOCK
    m = (c0 < s0 + SEL_BLOCK) & (c0 + CMP_BLOCK > s0) & (np.arange(n_str)[:, None] < n_c) & (np.arange(LANES)[None, :] < n_s)
    e = (np.arange(n_keys)[None, :] // SEL_BLOCK) == np.arange(LANES)[:, None]
    return m.astype(np.float32), e.astype(np.float32)


def _prep_mlstm(conv_w, conv_b, i_b, f_b, norm_g):
    cw8 = jnp.pad(conv_w, ((0, SUBLANES - MLSTM_CONV), (0, 0)))
    bias_if = jnp.pad(jnp.concatenate([i_b, f_b]), (0, LANES - 2 * B_HEADS)).reshape(1, LANES)
    return cw8, conv_b.reshape(1, 2 * B_WIDTH), bias_if, norm_g.reshape(1, B_WIDTH)


_A0, _B0, _Q0, _CMP0, _SLC0, _WIN0, _G0, _IF0, _PEND = 0, 1024, 2048, 2560, 2816, 3072, 3328, 3456, 3584
_PROJ_SPLITS = ((_A0, _B0), (_B0, _Q0), (_Q0, _CMP0), (_CMP0, _SLC0), (_SLC0, _WIN0), (_WIN0, _G0), (_G0, _IF0),
                (_IF0, _PEND))


def _pad_lanes(a, width):
    return jnp.pad(a, [(0, 0)] * (a.ndim - 1) + [(0, width - a.shape[-1])])


def _prep_layer(P, l):
    w_in = P["w_in"][l]
    wa, wb, wc = w_in[:, 0:1024], w_in[:, 1024:2056], w_in[:, 2056:3360]
    w_in_p = jnp.concatenate([
        wa, wb[:, 0:768], wb[:, 776:1032], wc[:, 0:512], wc[:, 512:1280],
        _pad_lanes(wc[:, 1280:1304], LANES), _pad_lanes(wb[:, 768:776], LANES)], axis=1).astype(BF16)
    up = P["ffn_up"][l]
    ffn_up_p = jnp.concatenate([_pad_lanes(up[:, 0:D_FF], D_FF_PAD), _pad_lanes(up[:, D_FF:], D_FF_PAD)], axis=1).astype(BF16)
    cw = P["nsa_cmp_w"][l]
    cmpw_bd = jnp.zeros((4 * HEAD_DIM, 4 * HEAD_DIM), F32)
    for kv in range(2):
        for g in range(C_KV_HEADS):
            o = (kv * C_KV_HEADS + g) * HEAD_DIM
            cmpw_bd = cmpw_bd.at[o:o + HEAD_DIM, o:o + HEAD_DIM].set(cw[kv, g])
    pool = P["nsa_pool_w"][l].reshape(CMP_BLOCK, 4 * HEAD_DIM)
    return dict(
        w_in=w_in_p,
        w_out=P["w_out"][l].astype(BF16),
        ffn_up=ffn_up_p,
        ffn_down=jnp.pad(P["ffn_down"][l], ((0, D_FF_PAD - D_FF), (0, 0))).astype(BF16),
        ffn_cw=jnp.pad(P["ffn_conv_w"][l], ((0, SUBLANES - FFN_CONV), (0, D_FF_PAD - D_FF))),
        ffn_cb=_pad_lanes(P["ffn_conv_b"][l].reshape(1, D_FF), D_FF_PAD),
        ln1=(P["ln_g"][l, 0].reshape(1, D_MODEL), P["ln_b"][l, 0].reshape(1, D_MODEL)),
        ln2=(P["ln_g"][l, 1].reshape(1, D_MODEL), P["ln_b"][l, 1].reshape(1, D_MODEL)),
        rwkv=_prep_rwkv(P["rwkv_mu"][l], P["rwkv_w0"][l], P["rwkv_w2"][l], P["rwkv_a0"][l], P["rwkv_a2"][l],
                        P["rwkv_g2"][l], P["rwkv_k_k"][l], P["rwkv_k_a"][l], P["rwkv_r_k"][l], P["rwkv_ln_g"][l],
                        P["rwkv_ln_b"][l]),
        mlstm=_prep_mlstm(P["mlstm_conv_w"][l], P["mlstm_conv_b"][l], P["mlstm_i_b"][l], P["mlstm_f_b"][l],
                          P["mlstm_norm_g"][l]),
        pool0=pool[0:CMP_STRIDE], pool1=pool[CMP_STRIDE:CMP_BLOCK], cmpw=cmpw_bd,
        gate_b=_pad_lanes(P["nsa_gate_b"][l].reshape(1, 3 * C_HEADS), LANES),
    )


def _rows8(state):
    return jnp.pad(state, ((0, 0), (SUBLANES - state.shape[1], 0), (0, 0)))


def _last_rows(prev, cur, k):
    if cur.shape[1] >= k:
        return cur[:, cur.shape[1] - k:]
    return jnp.concatenate([prev, cur], axis=1)[:, -k:]


def _trunk(x, mod, st, layers, nsa_fn, dense_tile, rec_tile):
    B, T, _ = x.shape
    dbb, dL = dense_tile
    rbb, rL = rec_tile
    new = {k: [] for k in ("nsa_kv", "win_kv", "rwkv", "rwkv_shift", "mlstm_C", "mlstm_n", "mlstm_m", "mlstm_conv", "ffn_conv")}
    for l, Lw in enumerate(layers):
        m = mod[l]
        pa, pb, pq, pcmp, pslc, pwin, pg, pif, kvb = _modmm(
            x, m, 0, 1, Lw["w_in"], _PROJ_SPLITS + ((_SLC0, _G0),), (False,) * 8 + (True,), dbb, dL, "in_proj")
        ya, rw_st = _rwkv(pa, _rows8(st["rwkv_shift"][l][:, None, :]), _state_to_lanes(st["rwkv"][l]), *Lw["rwkv"], rbb, rL)
        yb, c_st, n_st, m_st = _mlstm(pb, _rows8(st["mlstm_conv"][l]), pif, _state_to_lanes(st["mlstm_C"][l]),
                                      st["mlstm_n"][l], st["mlstm_m"][l], *Lw["mlstm"], rbb, rL)
        yc, win_new = nsa_fn(l, Lw, pq, pg, pcmp, pslc, pwin, kvb)
        x = _outproj(ya, yb, yc, x, m, Lw["w_out"], *Lw["ln1"], dbb, dL)
        (u,) = _modmm(x, m, 3, 4, Lw["ffn_up"], ((0, 2 * D_FF_PAD),), (False,), dbb, dL, "ffn_up")
        st8 = _rows8(_pad_lanes(st["ffn_conv"][l], D_FF_PAD))
        x = _ffn_down(u, st8, x, m, Lw["ffn_cw"], Lw["ffn_cb"], Lw["ffn_down"], *Lw["ln2"], dbb, dL)
        new["nsa_kv"].append(jnp.concatenate([pcmp, pslc], axis=-1).reshape(B, T, 4, C_KV_HEADS, HEAD_DIM))
        new["win_kv"].append(win_new)
        new["rwkv"].append(_state_from_lanes(rw_st, B))
        new["rwkv_shift"].append(pa[:, -1])
        new["mlstm_C"].append(_state_from_lanes(c_st, B))
        new["mlstm_n"].append(n_st)
        new["mlstm_m"].append(m_st)
        new["mlstm_conv"].append(_last_rows(st["mlstm_conv"][l], pb[:, :, 0:2 * B_WIDTH], MLSTM_CONV - 1))
        new["ffn_conv"].append(_last_rows(st["ffn_conv"][l], u[:, :, 0:D_FF], FFN_CONV - 1))
    return x, {k: jnp.stack(v) for k, v in new.items()}


def kernel(x_prompt, x_sample, c_prompt, c_sample, cache_nsa_kv, cache_win_kv, state_rwkv, state_rwkv_shift,
           state_mlstm_C, state_mlstm_n, state_mlstm_m, state_mlstm_conv, state_ffn_conv, page_table,
           w_in, w_out, ada_w, ada_b, ln_g, ln_b, rwkv_mu, rwkv_w0, rwkv_w2, rwkv_a0, rwkv_a2, rwkv_g2,
           rwkv_k_k, rwkv_k_a, rwkv_r_k, rwkv_ln_g, rwkv_ln_b, mlstm_conv_w, mlstm_conv_b, mlstm_i_b,
           mlstm_f_b, mlstm_norm_g, nsa_pool_w, nsa_cmp_w, nsa_gate_b, ffn_up, ffn_conv_w, ffn_conv_b, ffn_down):
    P = dict(w_in=w_in, w_out=w_out, ln_g=ln_g, ln_b=ln_b, rwkv_mu=rwkv_mu, rwkv_w0=rwkv_w0, rwkv_w2=rwkv_w2,
             rwkv_a0=rwkv_a0, rwkv_a2=rwkv_a2, rwkv_g2=rwkv_g2, rwkv_k_k=rwkv_k_k, rwkv_k_a=rwkv_k_a,
             rwkv_r_k=rwkv_r_k, rwkv_ln_g=rwkv_ln_g, rwkv_ln_b=rwkv_ln_b, mlstm_conv_w=mlstm_conv_w,
             mlstm_conv_b=mlstm_conv_b, mlstm_i_b=mlstm_i_b, mlstm_f_b=mlstm_f_b, mlstm_norm_g=mlstm_norm_g,
             nsa_pool_w=nsa_pool_w, nsa_cmp_w=nsa_cmp_w, nsa_gate_b=nsa_gate_b, ffn_up=ffn_up,
             ffn_conv_w=ffn_conv_w, ffn_conv_b=ffn_conv_b, ffn_down=ffn_down)
    Bp, Tp, _ = x_prompt.shape
    Bs, Ts, _ = x_sample.shape
    G, dh = C_KV_HEADS, HEAD_DIM
    layers = [_prep_layer(P, l) for l in range(DEPTH)]

    nb = -(-(Bp + Bs) // SUBLANES) * SUBLANES
    c_all = jnp.pad(jnp.concatenate([c_prompt, c_sample], axis=0), ((0, nb - Bp - Bs), (0, 0)))
    mod = _ada(c_all, ada_w, ada_b)
    mod_p = mod[:, 0:Bp].reshape(DEPTH, Bp, 1, 6 * D_MODEL)
    mod_s = mod[:, Bp:Bp + Bs].reshape(DEPTH, Bs, 1, 6 * D_MODEL)

    st_p = dict(
        rwkv=jnp.zeros((DEPTH, Bp, A_HEADS, dh, dh), F32), rwkv_shift=jnp.zeros((DEPTH, Bp, 4 * A_WIDTH), F32),
        mlstm_C=jnp.zeros((DEPTH, Bp, B_HEADS, dh, dh), F32), mlstm_n=jnp.zeros((DEPTH, Bp, B_HEADS, dh), F32),
        mlstm_m=jnp.zeros((DEPTH, Bp, B_HEADS), F32), mlstm_conv=jnp.zeros((DEPTH, Bp, MLSTM_CONV - 1, 2 * B_WIDTH), F32),
        ffn_conv=jnp.zeros((DEPTH, Bp, FFN_CONV - 1, D_FF), F32))
    m_p, e_p = _nsa_consts(Tp, Tp)
    e_p = jnp.asarray(e_p, BF16)
    tq = 128
    rs = 4 * tq

    def nsa_prompt(l, Lw, pq, pg, pcmp, pslc, pwin, kvb):
        kc = _compress(pcmp, jnp.tile(Lw["pool0"], (rs // CMP_STRIDE, 1)), jnp.tile(Lw["pool1"], (rs // CMP_STRIDE, 1)), Lw["cmpw"])
        yc = _nsa_prompt(pq, pg, Lw["gate_b"], kc, kvb, jnp.asarray(m_p), e_p, tq)
        return yc, pwin[:, -min(WINDOW, Tp):].reshape(Bp, min(WINDOW, Tp), 2, G, dh)

    y_prompt, new_p = _trunk(x_prompt, mod_p, st_p, layers, nsa_prompt, (1, 256), (Bp, ROWS // Bp))

    st_s = dict(rwkv=state_rwkv, rwkv_shift=state_rwkv_shift, mlstm_C=state_mlstm_C, mlstm_n=state_mlstm_n,
                mlstm_m=state_mlstm_m, mlstm_conv=state_mlstm_conv, ffn_conv=state_ffn_conv)
    n_pages = page_table.shape[1]
    past_len = n_pages * PAGE_SIZE
    lp = -(-(past_len + Ts) // SEL_BLOCK) * SEL_BLOCK
    ngrp = 2 * LANES
    assert (n_pages + 1) * (PAGE_SIZE // CMP_STRIDE) <= ngrp
    m_s, e_s = _nsa_consts(lp, past_len + PAGE_SIZE)
    m_s = jnp.asarray(np.pad(m_s, ((0, ngrp - m_s.shape[0]), (0, 0))))
    e_s = jnp.asarray(e_s, BF16)
    cache_t = jnp.transpose(cache_nsa_kv, (0, 1, 3, 4, 5, 2)).reshape(DEPTH, cache_nsa_kv.shape[1], 4 * G * dh, PAGE_SIZE)
    nwb = cache_win_kv.shape[2]
    win_t_all = jnp.transpose(cache_win_kv, (0, 1, 3, 4, 5, 2)).reshape(DEPTH, Bs, 2 * G * dh, nwb)
    reps = PAGE_SIZE // CMP_STRIDE

    def nsa_sample(l, Lw, pq, pg, pcmp, pslc, pwin, kvb):
        yc = _nsa_sample(l, page_table, cache_t, pq, pg, Lw["gate_b"], pcmp, pslc, pwin, win_t_all,
                         jnp.tile(Lw["pool0"].T, (1, reps)), jnp.tile(Lw["pool1"].T, (1, reps)),
                         Lw["cmpw"].T, m_s, e_s)
        return yc, pwin.reshape(Bs, Ts, 2, G, dh)

    y_sample, new_s = _trunk(x_sample, mod_s, st_s, layers, nsa_sample, (ROWS // Ts, Ts), (ROWS // Ts, Ts))
    keep = min(WINDOW, nwb + Ts)
    new_s["win_kv"] = jnp.concatenate([cache_win_kv, new_s["win_kv"]], axis=2)[:, :, nwb + Ts - keep:]

    return (y_prompt, y_sample,
            new_p["nsa_kv"], new_s["nsa_kv"], new_p["win_kv"], new_s["win_kv"],
            new_p["rwkv"], new_s["rwkv"], new_p["rwkv_shift"], new_s["rwkv_shift"],
            new_p["mlstm_C"], new_s["mlstm_C"], new_p["mlstm_n"], new_s["mlstm_n"],
            new_p["mlstm_m"], new_s["mlstm_m"], new_p["mlstm_conv"], new_s["mlstm_conv"],
            new_p["ffn_conv"], new_s["ffn_conv"])
```

```python
import functools
import math

import numpy as np
import jax
import jax.numpy as jnp
from jax import lax
from jax.experimental import pallas as pl
from jax.experimental.pallas import tpu as pltpu

F32 = jnp.float32
BF16 = jnp.bfloat16

D_MODEL = 1024
DEPTH = 4
HEAD_DIM = 64
A_WIDTH = 256
B_WIDTH = 256
C_WIDTH = 512
A_HEADS = 4
B_HEADS = 4
C_HEADS = 8
C_KV_HEADS = 2
C_GROUP = 4
PAGE_SIZE = 128
MLSTM_CONV = 4
CMP_BLOCK = 32
CMP_STRIDE = 16
SEL_BLOCK = 64
SEL_TOP = 16
WINDOW = 512
D_FF = 2752
D_FF_PAD = 2816
FFN_CONV = 3
ALPHA = (2 * DEPTH) ** 0.25
FORCE_BONUS = 1e4
NEG_INF = -1e30
LANES = 128
SUBLANES = 8
ROWS = 256
VMEM_LIMIT = 60 * 1024 * 1024

NN = (((1,), (0,)), ((), ()))
NT = (((1,), (1,)), ((), ()))


def _dot(a, b, dn=NN):
    return lax.dot_general(a, b, dn, preferred_element_type=F32)


def _split2(a):
    hi = a.astype(BF16)
    lo = (a - hi.astype(F32)).astype(BF16)
    return hi, lo


def _mm(a, b, passes=1, dn=NN):
    if passes == 1:
        return _dot(a.astype(BF16), b.astype(BF16), dn)
    ah, al = _split2(a)
    bh, bl = _split2(b)
    return _dot(ah, bh, dn) + (_dot(al, bh, dn) + _dot(ah, bl, dn))


def _mm_sel(sel, x, dn=NN):
    s = sel.astype(BF16)
    x1 = x.astype(BF16)
    r1 = x - x1.astype(F32)
    x2 = r1.astype(BF16)
    x3 = (r1 - x2.astype(F32)).astype(BF16)
    return _dot(s, x1, dn) + (_dot(s, x2, dn) + _dot(s, x3, dn))


def _mm_xsel(x, sel, dn=NN):
    s = sel.astype(BF16)
    x1 = x.astype(BF16)
    r1 = x - x1.astype(F32)
    x2 = r1.astype(BF16)
    x3 = (r1 - x2.astype(F32)).astype(BF16)
    return _dot(x1, s, dn) + (_dot(x2, s, dn) + _dot(x3, s, dn))


def _sigmoid(x):
    return 1.0 / (1.0 + jnp.exp(-x))


def _silu(x):
    return x * _sigmoid(x)


def _softplus(x):
    return jnp.maximum(x, 0.0) + jnp.log(1.0 + jnp.exp(-jnp.abs(x)))


def _log_sigmoid(x):
    return -_softplus(-x)


def _iota(shape, axis):
    return lax.broadcasted_iota(jnp.int32, shape, axis)


def _block_masks(rows, chunk):
    sh = int(math.log2(chunk))
    r = _iota((rows, rows), 0)
    s = _iota((rows, rows), 1)
    same = jnp.right_shift(r, sh) == jnp.right_shift(s, sh)
    return same, same & (s <= r), same & (s < r)


def _head_ones(width):
    r = _iota((width, width), 0)
    s = _iota((width, width), 1)
    return (jnp.right_shift(r, 6) == jnp.right_shift(s, 6)).astype(F32)


def _expand_mat(rows, chunk, nseq):
    sh = int(math.log2(chunk))
    r = _iota((rows, nseq * HEAD_DIM), 0)
    c = _iota((rows, nseq * HEAD_DIM), 1)
    return ((jnp.right_shift(c, 6) == jnp.right_shift(r, sh)) & ((r & (chunk - 1)) == 0)).astype(F32)


def _seq_lane_mask(rows, chunk, nseq):
    sh = int(math.log2(chunk))
    r = _iota((rows, nseq * HEAD_DIM), 0)
    c = _iota((rows, nseq * HEAD_DIM), 1)
    return jnp.right_shift(c, 6) == jnp.right_shift(r, sh)


def _fold_mat(nseq):
    r = _iota((nseq * HEAD_DIM, HEAD_DIM), 0)
    c = _iota((nseq * HEAD_DIM, HEAD_DIM), 1)
    return ((r & (HEAD_DIM - 1)) == c).astype(F32)


def _shifted_rows(pre, cur, nshift):
    bb, L, C = cur.shape
    full = jnp.concatenate([pre, cur], axis=1).reshape(bb * (L + SUBLANES), C)
    out = []
    for k in range(1, nshift + 1):
        sh = pltpu.roll(full, k, axis=0).reshape(bb, L + SUBLANES, C)
        out.append(sh[:, SUBLANES:, :])
    return out


def _layernorm(z, g, b):
    mu = jnp.mean(z, axis=-1, keepdims=True)
    zc = z - mu
    var = jnp.mean(zc * zc, axis=-1, keepdims=True)
    return zc * lax.rsqrt(var + 1e-5) * g + b


def _cparams(sem):
    return pltpu.CompilerParams(dimension_semantics=sem, vmem_limit_bytes=VMEM_LIMIT)


def _ada_kernel(c_ref, w_ref, b_ref, o_ref):
    c = c_ref[...]
    o_ref[...] = _mm(_silu(c), w_ref[...], 3) + b_ref[...]


def _ada(c_all, ada_w, ada_b):
    nb = c_all.shape[0]
    tn = 1536
    return pl.pallas_call(
        _ada_kernel,
        grid=(DEPTH, 6 * D_MODEL // tn),
        in_specs=[
            pl.BlockSpec((nb, D_MODEL), lambda l, n: (0, 0)),
            pl.BlockSpec((None, D_MODEL, tn), lambda l, n: (l, 0, n)),
            pl.BlockSpec((None, 1, tn), lambda l, n: (l, 0, n)),
        ],
        out_specs=pl.BlockSpec((None, nb, tn), lambda l, n: (l, 0, n)),
        out_shape=jax.ShapeDtypeStruct((DEPTH, nb, 6 * D_MODEL), F32),
        compiler_params=_cparams(("arbitrary", "arbitrary")),
        name="ada_mod",
    )(c_all, ada_w, ada_b.reshape(DEPTH, 1, 6 * D_MODEL))


def _modmm_kernel(x_ref, sh_ref, sc_ref, w_ref, *o_refs, splits, bf16_outs):
    x = x_ref[...]
    bb, L, D = x.shape
    h = (x * (1.0 + sc_ref[...]) + sh_ref[...]).reshape(bb * L, D).astype(BF16)
    o = jnp.dot(h, w_ref[...], preferred_element_type=F32)
    for (a, b), o_ref, as_bf16 in zip(splits, o_refs, bf16_outs):
        piece = o[:, a:b].reshape(bb, L, b - a)
        o_ref[...] = piece.astype(BF16) if as_bf16 else piece


def _modmm(x, mod, sh_col, sc_col, w, splits, bf16_outs, bb, L, name):
    B, T, D = x.shape
    N = w.shape[1]
    kern = functools.partial(_modmm_kernel, splits=splits, bf16_outs=bf16_outs)
    return pl.pallas_call(
        kern,
        grid=(B // bb, T // L),
        in_specs=[
            pl.BlockSpec((bb, L, D), lambda i, j: (i, j, 0)),
            pl.BlockSpec((bb, 1, D), lambda i, j: (i, 0, sh_col)),
            pl.BlockSpec((bb, 1, D), lambda i, j: (i, 0, sc_col)),
            pl.BlockSpec((D, N), lambda i, j: (0, 0)),
        ],
        out_specs=[pl.BlockSpec((bb, L, b - a), lambda i, j: (i, j, 0)) for a, b in splits],
        out_shape=[jax.ShapeDtypeStruct((B, T, b - a), BF16 if q else F32) for (a, b), q in zip(splits, bf16_outs)],
        compiler_params=_cparams(("arbitrary", "arbitrary")),
        name=name,
    )(x, mod, mod, w)


def _outproj_kernel(ya_ref, yb_ref, yc_ref, x_ref, g_ref, w_ref, lg_ref, lb_ref, o_ref):
    x = x_ref[...]
    bb, L, D = x.shape
    rows = bb * L
    ya = ya_ref[...].reshape(rows, A_WIDTH).astype(BF16)
    yb = yb_ref[...].reshape(rows, B_WIDTH).astype(BF16)
    yc = yc_ref[...].reshape(rows, C_WIDTH).astype(BF16)
    y = (jnp.dot(ya, w_ref[0:A_WIDTH, :], preferred_element_type=F32)
         + jnp.dot(yb, w_ref[A_WIDTH:A_WIDTH + B_WIDTH, :], preferred_element_type=F32)
         + jnp.dot(yc, w_ref[A_WIDTH + B_WIDTH:, :], preferred_element_type=F32))
    z = ALPHA * x + (1.0 + g_ref[...]) * y.reshape(bb, L, D)
    o_ref[...] = _layernorm(z, lg_ref[...], lb_ref[...])


def _outproj(ya, yb, yc, x, mod, w_out, ln_g, ln_b, bb, L):
    B, T, D = x.shape
    blk = lambda w: pl.BlockSpec((bb, L, w), lambda i, j: (i, j, 0))
    return pl.pallas_call(
        _outproj_kernel,
        grid=(B // bb, T // L),
        in_specs=[
            blk(A_WIDTH), blk(B_WIDTH), blk(C_WIDTH), blk(D),
            pl.BlockSpec((bb, 1, D), lambda i, j: (i, 0, 2)),
            pl.BlockSpec((D, D), lambda i, j: (0, 0)),
            pl.BlockSpec((1, D), lambda i, j: (0, 0)),
            pl.BlockSpec((1, D), lambda i, j: (0, 0)),
        ],
        out_specs=blk(D),
        out_shape=jax.ShapeDtypeStruct((B, T, D), F32),
        compiler_params=_cparams(("arbitrary", "arbitrary")),
        name="outproj_ln",
    )(ya, yb, yc, x, mod, w_out, ln_g, ln_b)


def _ffn_down_kernel(ug_ref, uv_ref, halo_ref, st_ref, x_ref, g_ref, cw_ref, cb_ref, w_ref, lg_ref, lb_ref, o_ref):
    ug = ug_ref[...]
    bb, L, N = ug.shape
    first = pl.program_id(1) == 0
    pre = jnp.where(first, st_ref[...], halo_ref[...])
    u1, u2 = _shifted_rows(pre, ug, FFN_CONV - 1)
    cw = cw_ref[...]
    conv = cb_ref[...] + ug * cw[2:3, :] + u1 * cw[1:2, :] + u2 * cw[0:1, :]
    a = (_silu(conv) * uv_ref[...]).reshape(bb * L, N).astype(BF16)
    y = jnp.dot(a, w_ref[...], preferred_element_type=F32)
    x = x_ref[...]
    z = ALPHA * x + (1.0 + g_ref[...]) * y.reshape(x.shape)
    o_ref[...] = _layernorm(z, lg_ref[...], lb_ref[...])


def _ffn_down(u, st8, x, mod, conv_w, conv_b, w_down, ln_g, ln_b, bb, L):
    B, T, D = x.shape
    N = D_FF_PAD
    lb8 = L // SUBLANES
    return pl.pallas_call(
        _ffn_down_kernel,
        grid=(B // bb, T // L),
        in_specs=[
            pl.BlockSpec((bb, L, N), lambda i, j: (i, j, 0)),
            pl.BlockSpec((bb, L, N), lambda i, j: (i, j, 1)),
            pl.BlockSpec((bb, SUBLANES, N), lambda i, j: (i, jnp.maximum(j * lb8 - 1, 0), 0)),
            pl.BlockSpec((bb, SUBLANES, N), lambda i, j: (i, 0, 0)),
            pl.BlockSpec((bb, L, D), lambda i, j: (i, j, 0)),
            pl.BlockSpec((bb, 1, D), lambda i, j: (i, 0, 5)),
            pl.BlockSpec((SUBLANES, N), lambda i, j: (0, 0)),
            pl.BlockSpec((1, N), lambda i, j: (0, 0)),
            pl.BlockSpec((N, D), lambda i, j: (0, 0)),
            pl.BlockSpec((1, D), lambda i, j: (0, 0)),
            pl.BlockSpec((1, D), lambda i, j: (0, 0)),
        ],
        out_specs=pl.BlockSpec((bb, L, D), lambda i, j: (i, j, 0)),
        out_shape=jax.ShapeDtypeStruct((B, T, D), F32),
        compiler_params=_cparams(("arbitrary", "arbitrary")),
        name="ffn_down_ln",
    )(u, u, u, st8, x, mod, conv_w, conv_b, w_down, ln_g, ln_b)


def _unit_lower_inverses(n_mats, chunk):
    rows = n_mats[0].shape[0]
    eye = (_iota((rows, rows), 0) == _iota((rows, rows), 1)).astype(F32)
    ps = [n.astype(BF16) for n in n_mats]
    ts = [eye + n for n in n_mats]
    for _ in range(int(math.log2(chunk)) - 1):
        ps = [_dot(p, p).astype(BF16) for p in ps]
        ts = [t + _dot(p, t.astype(BF16)) for p, t in zip(ps, ts)]
    resids = [(eye - t) + _mm(n, t, 3) for n, t in zip(n_mats, ts)]
    return [t + _dot(t.astype(BF16), r.astype(BF16)) for t, r in zip(ts, resids)]


def _rwkv_kernel(pa_ref, halo_ref, sh_ref, st0_ref, mu_ref, pv_ref, lw_ref, ya_ref, sto_ref, st_ref, *, chunk):
    c = pl.program_id(1)
    bb, L, _ = pa_ref.shape
    R = bb * L
    W = A_WIDTH

    @pl.when(c == 0)
    def _():
        st_ref[...] = st0_ref[...]

    pa = pa_ref[...]
    pre = jnp.where(c == 0, sh_ref[...], halo_ref[...])
    (prev,) = _shifted_rows(pre, pa, 1)
    x = pa.reshape(R, 4 * W)
    pm = x + (prev.reshape(R, 4 * W) - x) * mu_ref[...]
    r = pm[:, 0:W]
    k = pm[:, W:2 * W]
    v = pm[:, 2 * W:3 * W]
    lo = pm[:, 3 * W:4 * W]
    lane = _iota((R, W), 1)
    z = jnp.where(lane < 64, jnp.tanh(lo), jnp.where(lane < 128, lo, _sigmoid(lo)))
    lora = _mm(z, lw_ref[...], 3)
    pv = pv_ref[...]
    w0, a0, k_k, k_a, r_k, ln_g, ln_b = (pv[i:i + 1, :] for i in range(7))
    w = -_softplus(-(w0 + lora[:, 0:W])) - 0.5
    a = _sigmoid(a0 + lora[:, W:2 * W])
    g = lora[:, 2 * W:3 * W]
    ones_h = _head_ones(W)
    kk = k * k_k
    kk = kk / jnp.maximum(jnp.sqrt(_mm_xsel(kk * kk, ones_h)), 1e-12)
    k2 = k * (1.0 + (a - 1.0) * k_a)
    lw = -jnp.exp(w)
    same, incl, strict = _block_masks(R, L)
    cum = _mm_sel(incl.astype(F32), lw)
    tot = _mm_sel(same.astype(F32), lw)
    e_neg = jnp.exp(-cum)
    e_rem = jnp.exp(tot - cum)
    kb = kk * a
    a_t = -kk * jnp.exp(cum - lw)
    b_t = kb * e_neg
    k_t = k2 * e_neg
    r_t = r * jnp.exp(cum)
    bh_t = jnp.transpose(kb * e_rem)
    kh_t = jnp.transpose(k2 * e_rem)
    gam_t = jnp.transpose(jnp.exp(tot))
    per_seq = bb <= 4
    if not per_seq:
        expand = _expand_mat(R, L, bb)
        lmask = _seq_lane_mask(R, L, bb)
        lmask2 = jnp.concatenate([lmask, lmask], axis=0)
        fold = _fold_mat(bb)
        tile = jnp.transpose(fold)
    heads = range(A_HEADS)
    hsl = [slice(h * HEAD_DIM, (h + 1) * HEAD_DIM) for h in heads]
    states = [st_ref[h] for h in heads]
    As, Bs, Ks, Rs, Vs = ([x[:, hs] for hs in hsl] for x in (a_t, b_t, k_t, r_t, v))
    m_abs = [jnp.where(strict, _mm(As[h], Bs[h], 3, NT), 0.0) for h in heads]
    t_invs = _unit_lower_inverses(m_abs, L)
    m_aks = [jnp.where(strict, _mm(As[h], Ks[h], 1, NT), 0.0) for h in heads]
    g_rs = [_mm(Rs[h], jnp.concatenate([Bs[h], Ks[h]], axis=0), 1, NT) for h in heads]
    m_rbks = [jnp.concatenate([jnp.where(incl, g[:, 0:R], 0.0), jnp.where(incl, g[:, R:2 * R], 0.0)], axis=1) for g in g_rs]
    if per_seq:
        pss = [[_mm(jnp.concatenate([As[h][b * L:(b + 1) * L], Rs[h][b * L:(b + 1) * L]], axis=0),
                    states[h][:, b * HEAD_DIM:(b + 1) * HEAD_DIM], 1) for b in range(bb)] for h in heads]
        ps_as = [jnp.concatenate([p[0:L] for p in ps], axis=0) for ps in pss]
        ps_rs = [jnp.concatenate([p[L:2 * L] for p in ps], axis=0) for ps in pss]
    else:
        pss = [_mm_xsel(jnp.where(lmask2, _mm(jnp.concatenate([As[h], Rs[h]], axis=0), states[h], 1), 0.0), fold) for h in heads]
        ps_as = [p[0:R] for p in pss]
        ps_rs = [p[R:2 * R] for p in pss]
    Ys = [ps_as[h] + _mm(m_aks[h], Vs[h], 1) for h in heads]
    Us = [_mm(t_invs[h], Ys[h], 3) for h in heads]
    UVs = [jnp.concatenate([Us[h], Vs[h]], axis=0) for h in heads]
    outs = [ps_rs[h] + _mm(m_rbks[h], UVs[h], 1) for h in heads]
    new_states = []
    for h in heads:
        hs = hsl[h]
        if per_seq:
            new = []
            for b in range(bb):
                ts = slice(b * L, (b + 1) * L)
                lhs = jnp.concatenate([bh_t[hs, ts], kh_t[hs, ts]], axis=1)
                uv_b = jnp.concatenate([Us[h][ts], Vs[h][ts]], axis=0)
                new.append(gam_t[hs, b * L:b * L + 1] * states[h][:, b * HEAD_DIM:(b + 1) * HEAD_DIM] + _mm(lhs, uv_b, 1))
            new_states.append(jnp.concatenate(new, axis=1))
        else:
            UVb = jnp.where(lmask2, _mm_xsel(UVs[h], tile), 0.0)
            lhs = jnp.concatenate([bh_t[hs, :], kh_t[hs, :]], axis=1)
            gam = _mm_xsel(gam_t[hs, :], expand)
            new_states.append(gam * states[h] + _mm(lhs, UVb, 1))
    for h in range(A_HEADS):
        st_ref[h] = new_states[h]
    o = jnp.concatenate(outs, axis=1)
    inv = 1.0 / HEAD_DIM
    mu = _mm_xsel(o, ones_h) * inv
    oc = o - mu
    var = _mm_xsel(oc * oc, ones_h) * inv
    y = oc * lax.rsqrt(var + 64e-5) * ln_g + ln_b
    y = y + _mm_xsel(r * k2 * r_k, ones_h) * v
    ya_ref[...] = (y * g).reshape(bb, L, W)

    @pl.when(c == pl.num_programs(1) - 1)
    def _():
        sto_ref[...] = st_ref[...]


def _rwkv(pa, shift8, st0, mu, pvec, lora_w, bb, L):
    B, T, _ = pa.shape
    lb8 = L // SUBLANES
    kern = functools.partial(_rwkv_kernel, chunk=L)
    return pl.pallas_call(
        kern,
        grid=(B // bb, T // L),
        in_specs=[
            pl.BlockSpec((bb, L, 4 * A_WIDTH), lambda i, j: (i, j, 0)),
            pl.BlockSpec((bb, SUBLANES, 4 * A_WIDTH), lambda i, j: (i, jnp.maximum(j * lb8 - 1, 0), 0)),
            pl.BlockSpec((bb, SUBLANES, 4 * A_WIDTH), lambda i, j: (i, 0, 0)),
            pl.BlockSpec((A_HEADS, HEAD_DIM, bb * HEAD_DIM), lambda i, j: (0, 0, i)),
            pl.BlockSpec((1, 4 * A_WIDTH), lambda i, j: (0, 0)),
            pl.BlockSpec((SUBLANES, A_WIDTH), lambda i, j: (0, 0)),
            pl.BlockSpec((A_WIDTH, 3 * A_WIDTH), lambda i, j: (0, 0)),
        ],
        out_specs=[
            pl.BlockSpec((bb, L, A_WIDTH), lambda i, j: (i, j, 0)),
            pl.BlockSpec((A_HEADS, HEAD_DIM, bb * HEAD_DIM), lambda i, j: (0, 0, i)),
        ],
        out_shape=[
            jax.ShapeDtypeStruct((B, T, A_WIDTH), F32),
            jax.ShapeDtypeStruct((A_HEADS, HEAD_DIM, B * HEAD_DIM), F32),
        ],
        scratch_shapes=[pltpu.VMEM((A_HEADS, HEAD_DIM, bb * HEAD_DIM), F32)],
        compiler_params=_cparams(("arbitrary", "arbitrary")),
        name="rwkv7",
    )(pa, pa, shift8, st0, mu, pvec, lora_w)


def _prep_rwkv(mu, w0, w2, a0, a2, g2, k_k, k_a, r_k, ln_g, ln_b):
    zero = jnp.zeros((A_WIDTH,), F32)
    pvec = jnp.stack([w0, a0, k_k, k_a, r_k.reshape(A_WIDTH), ln_g, ln_b, zero])
    lora = jnp.zeros((A_WIDTH, 3 * A_WIDTH), F32)
    lora = lora.at[0:64, 0:A_WIDTH].set(w2)
    lora = lora.at[64:128, A_WIDTH:2 * A_WIDTH].set(a2)
    lora = lora.at[128:256, 2 * A_WIDTH:].set(g2)
    return mu.reshape(1, 4 * A_WIDTH), pvec, lora


def _state_to_lanes(s):
    B, H = s.shape[:2]
    return jnp.transpose(s, (1, 3, 0, 2)).reshape(H, HEAD_DIM, B * HEAD_DIM)


def _state_from_lanes(st, B):
    H = st.shape[0]
    return jnp.transpose(st.reshape(H, HEAD_DIM, B, HEAD_DIM), (2, 0, 3, 1))


def _mlstm_kernel(pb_ref, halo_ref, cv_ref, pif_ref, gt_ref, ct0_ref, n0_ref, m0_ref, cw_ref, cb_ref, bif_ref,
                  brow_ref, ng_ref, yb_ref, cto_ref, no_ref, mo_ref, ct_ref, nt_ref, m_ref):
    c = pl.program_id(1)
    bb, L, _ = pb_ref.shape
    R = bb * L
    W = B_WIDTH
    neg = -jnp.inf

    @pl.when(c == 0)
    def _():
        ct_ref[...] = ct0_ref[...]
        nt_ref[...] = n0_ref[...]
        m_ref[...] = jnp.broadcast_to(m0_ref[...], m_ref.shape)

    pb = pb_ref[...]
    qk_in = pb[:, :, 0:2 * W]
    pre = jnp.where(c == 0, cv_ref[...], halo_ref[...])
    s1, s2, s3 = _shifted_rows(pre, qk_in, MLSTM_CONV - 1)
    cw = cw_ref[...]
    conv = cb_ref[...] + qk_in * cw[3:4, :] + s1 * cw[2:3, :] + s2 * cw[1:2, :] + s3 * cw[0:1, :]
    qk = _silu(conv).reshape(R, 2 * W)
    q = qk[:, 0:W]
    k = qk[:, W:2 * W] * (HEAD_DIM ** -0.5)
    v = pb[:, :, 2 * W:3 * W].reshape(R, W)
    og = pb[:, :, 3 * W:4 * W].reshape(R, W)

    same, incl, _ = _block_masks(R, L)
    same_f = same.astype(F32)
    incl_f = incl.astype(F32)
    gc = pif_ref[...].reshape(R, LANES) + bif_ref[...]
    lane = _iota((R, LANES), 1)
    lfc = jnp.where((lane >= B_HEADS) & (lane < 2 * B_HEADS), _log_sigmoid(gc), 0.0)
    bcum_c = _mm_sel(incl_f, lfc)
    btot_c = _mm_sel(same_f, lfc)
    gr = gt_ref[...] + brow_ref[...]
    row = _iota((SUBLANES, R), 0)
    lfr = jnp.where(row >= B_HEADS, _log_sigmoid(gr), 0.0)
    bcum_r = _mm_xsel(lfr, incl_f, NT)
    btot_r = _mm_xsel(lfr, same_f)
    m_col = jnp.broadcast_to(m_ref[:, 0:1, :], (bb, L, LANES)).reshape(R, LANES)

    per_seq = bb <= 4
    if not per_seq:
        lmask = _seq_lane_mask(R, L, bb)
        fold = _fold_mat(bb)
        tile = jnp.transpose(fold)
        expand = _expand_mat(R, L, bb)
    sh = int(math.log2(L))
    rl = _iota((R, LANES), 0)
    blockind = (jnp.right_shift(rl, sh) == lane).astype(F32)
    firstind = ((jnp.right_shift(rl, sh) == lane) & ((rl & (L - 1)) == 0)).astype(F32)

    houts, kws, wcs = [], [], []
    cts = [ct_ref[h] for h in range(B_HEADS)]
    nts = [nt_ref[h] for h in range(B_HEADS)]
    m_new_all = jnp.zeros((R, LANES), F32)
    heads = range(B_HEADS)
    hsl = [slice(h * HEAD_DIM, (h + 1) * HEAD_DIM) for h in heads]
    qks = [_mm(q[:, hs], k[:, hs], 1, NT) for hs in hsl]
    if per_seq:
        QCs = [jnp.concatenate([_mm(q[b * L:(b + 1) * L, hsl[h]], cts[h][:, b * HEAD_DIM:(b + 1) * HEAD_DIM], 1)
                                for b in range(bb)], axis=0) for h in heads]
    else:
        QCs = [_mm_xsel(jnp.where(lmask, _mm(q[:, hsl[h]], cts[h], 1), 0.0), fold) for h in heads]
    qns = [jnp.sum(_mm(q[:, hsl[h]], nts[h], 3) * blockind, axis=1, keepdims=True) for h in heads]
    Ds = [jnp.where(incl, bcum_c[:, B_HEADS + h:B_HEADS + h + 1] - bcum_r[B_HEADS + h:B_HEADS + h + 1, :] + gr[h:h + 1, :], neg)
          for h in heads]
    inters = [bcum_c[:, B_HEADS + h:B_HEADS + h + 1] + m_col[:, h:h + 1] for h in heads]
    mts = [jnp.maximum(inters[h], jnp.max(Ds[h], axis=1, keepdims=True)) for h in heads]
    Ss = [qks[h] * jnp.exp(Ds[h] - mts[h]) for h in heads]
    svs = [_mm(Ss[h], v[:, hsl[h]], 1) for h in heads]
    for h in heads:
        hs = hsl[h]
        K = k[:, hs]
        b_c = bcum_c[:, B_HEADS + h:B_HEADS + h + 1]
        b_r = bcum_r[B_HEADS + h:B_HEADS + h + 1, :]
        i_r = gr[h:h + 1, :]
        i_c = gc[:, h:h + 1]
        m_c = m_col[:, h:h + 1]
        mt = mts[h]
        iw = jnp.exp(inters[h] - mt)
        num = svs[h] + iw * QCs[h]
        den = jnp.sum(Ss[h], axis=1, keepdims=True) + iw * qns[h]
        houts.append(num / jnp.maximum(jnp.abs(den), jnp.exp(-mt)))
        bl_c = btot_c[:, B_HEADS + h:B_HEADS + h + 1]
        bl_r = btot_r[B_HEADS + h:B_HEADS + h + 1, :]
        gs_c = bl_c - b_c + i_c
        gmax = jnp.max(jnp.where(same, bl_r - b_r + i_r, neg), axis=1, keepdims=True)
        m_new = jnp.maximum(bl_c + m_c, gmax)
        kws.append(K * jnp.exp(gs_c - m_new))
        wcs.append(jnp.exp(bl_c + m_c - m_new))
        m_new_all = jnp.where(lane == h, m_new, m_new_all)

    kw_t = jnp.transpose(jnp.concatenate(kws, axis=1))
    new_c, new_n = [], []
    for h in range(B_HEADS):
        hs = slice(h * HEAD_DIM, (h + 1) * HEAD_DIM)
        ct_h = cts[h]
        if per_seq:
            new_c.append(jnp.concatenate(
                [wcs[h][b * L:b * L + 1, :] * ct_h[:, b * HEAD_DIM:(b + 1) * HEAD_DIM]
                 + _mm(kw_t[hs, b * L:(b + 1) * L], v[b * L:(b + 1) * L, hs], 1) for b in range(bb)], axis=1))
        else:
            vb = jnp.where(lmask, _mm_xsel(v[:, hs], tile), 0.0)
            wc_row = jnp.sum(wcs[h] * expand, axis=0, keepdims=True)
            new_c.append(wc_row * ct_h + _mm(kw_t[hs, :], vb, 1))
        wc_lane = jnp.sum(wcs[h] * firstind, axis=0, keepdims=True)
        new_n.append(wc_lane * nts[h] + _mm_xsel(kw_t[hs, :], blockind))
    for h in range(B_HEADS):
        ct_ref[h] = new_c[h]
        nt_ref[h] = new_n[h]
    m_ref[...] = m_new_all.reshape(bb, L, LANES)[:, 0:SUBLANES, :]

    hcat = jnp.concatenate(houts, axis=1)
    ones_h = _head_ones(W)
    inv = 1.0 / HEAD_DIM
    mu = _mm_xsel(hcat, ones_h) * inv
    hc = hcat - mu
    var = _mm_xsel(hc * hc, ones_h) * inv
    hn = hc * lax.rsqrt(var + 1e-5) * ng_ref[...]
    yb_ref[...] = (_sigmoid(og) * hn).reshape(bb, L, W)

    @pl.when(c == pl.num_programs(1) - 1)
    def _():
        cto_ref[...] = ct_ref[...]
        no_ref[...] = nt_ref[...]
        mo_ref[...] = m_ref[...]


def _mlstm(pb, conv8, pif, ct0, n0, m0, conv_w8, conv_b, bias_if, norm_g, bb, L):
    B, T, _ = pb.shape
    R = bb * L
    nbi, nch = B // bb, T // L
    lb8 = L // SUBLANES
    g_t = pif[:, :, 0:SUBLANES].reshape(nbi, bb, nch, L, SUBLANES).transpose(0, 2, 4, 1, 3).reshape(nbi, nch, SUBLANES, R)
    bias_row = jnp.broadcast_to(bias_if[0, 0:SUBLANES].reshape(SUBLANES, 1), (SUBLANES, R))
    n_in = jnp.pad(n0.reshape(nbi, bb, B_HEADS, HEAD_DIM).transpose(0, 2, 3, 1), ((0, 0), (0, 0), (0, 0), (0, LANES - bb)))
    m_in = jnp.pad(m0, ((0, 0), (0, LANES - B_HEADS))).reshape(B, 1, LANES)
    yb, ct, nt, mo = pl.pallas_call(
        _mlstm_kernel,
        grid=(nbi, nch),
        in_specs=[
            pl.BlockSpec((bb, L, 4 * B_WIDTH), lambda i, j: (i, j, 0)),
            pl.BlockSpec((bb, SUBLANES, 2 * B_WIDTH), lambda i, j: (i, jnp.maximum(j * lb8 - 1, 0), 0)),
            pl.BlockSpec((bb, SUBLANES, 2 * B_WIDTH), lambda i, j: (i, 0, 0)),
            pl.BlockSpec((bb, L, LANES), lambda i, j: (i, j, 0)),
            pl.BlockSpec((None, None, SUBLANES, R), lambda i, j: (i, j, 0, 0)),
            pl.BlockSpec((B_HEADS, HEAD_DIM, bb * HEAD_DIM), lambda i, j: (0, 0, i)),
            pl.BlockSpec((None, B_HEADS, HEAD_DIM, LANES), lambda i, j: (i, 0, 0, 0)),
            pl.BlockSpec((bb, 1, LANES), lambda i, j: (i, 0, 0)),
            pl.BlockSpec((SUBLANES, 2 * B_WIDTH), lambda i, j: (0, 0)),
            pl.BlockSpec((1, 2 * B_WIDTH), lambda i, j: (0, 0)),
            pl.BlockSpec((1, LANES), lambda i, j: (0, 0)),
            pl.BlockSpec((SUBLANES, R), lambda i, j: (0, 0)),
            pl.BlockSpec((1, B_WIDTH), lambda i, j: (0, 0)),
        ],
        out_specs=[
            pl.BlockSpec((bb, L, B_WIDTH), lambda i, j: (i, j, 0)),
            pl.BlockSpec((B_HEADS, HEAD_DIM, bb * HEAD_DIM), lambda i, j: (0, 0, i)),
            pl.BlockSpec((None, B_HEADS, HEAD_DIM, LANES), lambda i, j: (i, 0, 0, 0)),
            pl.BlockSpec((bb, SUBLANES, LANES), lambda i, j: (i, 0, 0)),
        ],
        out_shape=[
            jax.ShapeDtypeStruct((B, T, B_WIDTH), F32),
            jax.ShapeDtypeStruct((B_HEADS, HEAD_DIM, B * HEAD_DIM), F32),
            jax.ShapeDtypeStruct((nbi, B_HEADS, HEAD_DIM, LANES), F32),
            jax.ShapeDtypeStruct((B, SUBLANES, LANES), F32),
        ],
        scratch_shapes=[
            pltpu.VMEM((B_HEADS, HEAD_DIM, bb * HEAD_DIM), F32),
            pltpu.VMEM((B_HEADS, HEAD_DIM, LANES), F32),
            pltpu.VMEM((bb, SUBLANES, LANES), F32),
        ],
        compiler_params=_cparams(("arbitrary", "arbitrary")),
        name="mlstm",
    )(pb, pb, conv8, pif, g_t, ct0, n_in, m_in, conv_w8, conv_b, bias_if, bias_row, norm_g)
    n_new = nt[:, :, :, 0:bb].transpose(0, 3, 1, 2).reshape(B, B_HEADS, HEAD_DIM)
    return yb, ct, n_new, mo[:, 0, 0:B_HEADS]


def _masked_softmax(s, mask):
    mx = jnp.max(jnp.where(mask, s, NEG_INF), axis=1, keepdims=True)
    e = jnp.where(mask, jnp.exp(s - mx), 0.0)
    return e / jnp.maximum(jnp.sum(e, axis=1, keepdims=True), 1e-30)


def _select_blocks(psums, m_mat, t_col, n_s, blocks_major=False):
    tq = psums[0].shape[0]
    imps = [_mm_xsel(p, m_mat) for p in psums]
    blk = _iota((tq, LANES), 1)
    cur = jnp.right_shift(t_col, 6)
    valid = (blk * SEL_BLOCK <= t_col)
    bonus = jnp.where((blk == 0) | (blk == cur) | (blk == cur - 1), FORCE_BONUS, 0.0)
    scores = [jnp.where(blk < n_s, jnp.where(valid, imp + bonus, NEG_INF), -jnp.inf) for imp in imps]
    n_sel = min(SEL_TOP, n_s)
    if tq == LANES and n_s % SUBLANES == 0:
        scs = [jnp.transpose(sc)[0:n_s, :] for sc in scores]
        idx = _iota((n_s, tq), 0)
        ranks = [jnp.zeros((n_s, tq), F32) for _ in scs]
        for s in range(n_s):
            for c, sc in enumerate(scs):
                row = sc[s:s + 1, :]
                ranks[c] = ranks[c] + jnp.where((row > sc) | ((row == sc) & (idx > s)), 1.0, 0.0)
        sel_ts = [jnp.where(r < n_sel, 1.0, 0.0) for r in ranks]
        if n_s < LANES:
            sel_ts = [jnp.concatenate([st, jnp.zeros((LANES - n_s, tq), F32)], axis=0) for st in sel_ts]
        return sel_ts if blocks_major else [jnp.transpose(st) for st in sel_ts]
    ranks = [jnp.zeros((tq, LANES), F32) for _ in scores]
    for s in range(n_s):
        for c, sc in enumerate(scores):
            col = sc[:, s:s + 1]
            ranks[c] = ranks[c] + jnp.where((col > sc) | ((col == sc) & (blk > s)), 1.0, 0.0)
    sels = [jnp.where(r < n_sel, 1.0, 0.0) for r in ranks]
    return [jnp.transpose(x) for x in sels] if blocks_major else sels


def _stack_heads(pq, g):
    base = g * C_GROUP * HEAD_DIM
    parts = [pq[:, base + r * HEAD_DIM: base + (r + 1) * HEAD_DIM] for r in range(C_GROUP)]
    return jnp.concatenate(parts, axis=0) * (HEAD_DIM ** -0.5)


def _to_group_lanes(q, g):
    z = jnp.zeros_like(q)
    return jnp.concatenate([q, z] if g == 0 else [z, q], axis=1)


def _compress_kernel(x_ref, pw0_ref, pw1_ref, cw_ref, kc_ref, a_ref, *, rows_per_step):
    T = x_ref.shape[0]
    rs = rows_per_step
    ng = rs // CMP_STRIDE
    pool = (jnp.right_shift(_iota((ng, rs), 1), 4) == _iota((ng, rs), 0)).astype(F32)
    for c in range(T // rs):
        x = x_ref[c * rs:(c + 1) * rs, :]
        xw = jnp.concatenate([x * pw0_ref[...], x * pw1_ref[...]], axis=1)
        a_ref[c * ng:(c + 1) * ng, :] = _mm_sel(pool, xw)
    ngrp = T // CMP_STRIDE
    W = x_ref.shape[1]
    pooled = a_ref[:, 0:W] + pltpu.roll(a_ref[:, W:2 * W], ngrp - 1, axis=0)
    kc_ref[...] = _mm(pooled, cw_ref[...], 3)


def _compress(pcmp, pwt0, pwt1, cmpw_bd):
    B, T, W = pcmp.shape
    rs = pwt0.shape[0]
    ngrp = T // CMP_STRIDE
    kern = functools.partial(_compress_kernel, rows_per_step=rs)
    return pl.pallas_call(
        kern,
        grid=(B,),
        in_specs=[
            pl.BlockSpec((None, T, W), lambda b: (b, 0, 0)),
            pl.BlockSpec((rs, W), lambda b: (0, 0)),
            pl.BlockSpec((rs, W), lambda b: (0, 0)),
            pl.BlockSpec((W, W), lambda b: (0, 0)),
        ],
        out_specs=pl.BlockSpec((None, ngrp, W), lambda b: (b, 0, 0)),
        out_shape=jax.ShapeDtypeStruct((B, ngrp, W), F32),
        scratch_shapes=[pltpu.VMEM((ngrp, 2 * W), F32)],
        compiler_params=_cparams(("arbitrary",)),
        name="nsa_compress",
    )(pcmp, pwt0, pwt1, cmpw_bd)


def _combine_branches(gates, g, o_c, o_s, o_w, tq):
    outs = []
    for r in range(C_GROUP):
        rs = slice(r * tq, (r + 1) * tq)
        j = (g * C_GROUP + r) * 3
        outs.append(gates[:, j:j + 1] * o_c[rs] + gates[:, j + 1:j + 2] * o_s[rs] + gates[:, j + 2:j + 3] * o_w[rs])
    return outs


def _nsa_prompt_kernel(pq_ref, pg_ref, gb_ref, kc_ref, kv_ref, vt_ref, m_ref, et_ref, wb_ref, o_ref, *, n_s):
    i = pl.program_id(1)
    tq = pq_ref.shape[0]
    ngrp = kc_ref.shape[0]
    kt = 4 * tq
    t0 = i * tq
    rows = C_GROUP * tq
    tl = _iota((rows, 1), 0) & (tq - 1)
    t_row = t0 + tl
    t_col = t0 + _iota((tq, 1), 0)
    pq = pq_ref[...]
    gates = _sigmoid(pg_ref[...] + gb_ref[...])
    kc = kc_ref[...]
    n_end = _iota((rows, ngrp), 1) * CMP_STRIDE + (CMP_BLOCK - 1)
    cmask = n_end <= t_row
    n_full = t0 // kt
    tl_lane = _iota((1, rows), 1) & (tq - 1)
    diag_bias = jnp.where(_iota((kt, rows), 0) <= (t0 - n_full * kt) + tl_lane, 0.0, NEG_INF)
    n_tiles = WINDOW // tq + 1
    groups = range(C_KV_HEADS)
    qs = [_stack_heads(pq, g) for g in groups]
    q2fs = [_to_group_lanes(qs[g], g) for g in groups]
    q2s = [x.astype(BF16) for x in q2fs]
    q2ts = [jnp.transpose(x).astype(BF16) for x in q2fs]
    p_cs = [_masked_softmax(_mm(qs[g], kc[:, g * HEAD_DIM:(g + 1) * HEAD_DIM], 3, NT), cmask) for g in groups]
    o_cs = [_mm(p_cs[g], kc[:, 2 * HEAD_DIM + g * HEAD_DIM: 2 * HEAD_DIM + (g + 1) * HEAD_DIM], 1) for g in groups]
    psums = [functools.reduce(lambda a, b: a + b, [p[r * tq:(r + 1) * tq] for r in range(C_GROUP)]) for p in p_cs]
    sel_ts = _select_blocks(psums, m_ref[...], t_col, n_s, blocks_major=True)
    q_exts = [jnp.concatenate([q2ts[g], jnp.concatenate([((sel_ts[g] - 1.0) * (-NEG_INF)).astype(BF16)] * C_GROUP, axis=1)],
                              axis=0) for g in groups]

    def step(j, carries, extra):
        off = pl.multiple_of(j * kt, kt)
        kk = kv_ref[pl.ds(off, kt), 0:LANES]
        vv_t = vt_ref[:, pl.ds(off, kt)]
        e_t = et_ref[pl.ds(off, kt), :]
        kk_ext = jnp.concatenate([kk, e_t], axis=1)
        ss = [_dot(kk_ext, q_exts[g]) for g in groups]
        if extra is not None:
            ss = [s + extra for s in ss]
        m_news = [jnp.maximum(carries[g][0], jnp.max(ss[g], axis=0, keepdims=True)) for g in groups]
        ps = [jnp.exp(ss[g] - m_news[g]) for g in groups]
        alphas = [jnp.exp(carries[g][0] - m_news[g]) for g in groups]
        ls = [alphas[g] * carries[g][1] + jnp.sum(ps[g], axis=0, keepdims=True) for g in groups]
        accs = [alphas[g] * carries[g][2] + _dot(vv_t, ps[g].astype(BF16)) for g in groups]
        return tuple((m_news[g], ls[g], accs[g]) for g in groups)

    init1 = (jnp.full((1, rows), NEG_INF, F32), jnp.zeros((1, rows), F32), jnp.zeros((LANES, rows), F32))
    carries = lax.fori_loop(0, n_full, lambda j, c: step(j, c, None), tuple(init1 for _ in groups))
    carries = step(n_full, carries, diag_bias)

    gsl = [slice(g * HEAD_DIM, (g + 1) * HEAD_DIM) for g in groups]
    o_ss = [jnp.transpose(carries[g][2] / carries[g][1])[:, gsl[g]] for g in groups]

    kws, vws, wbs = [], [], []
    for cidx in range(n_tiles):
        tile = i - (n_tiles - 1) + cidx
        off = pl.multiple_of(jnp.maximum(tile, 0) * tq, tq)
        kws.append(kv_ref[pl.ds(off, tq), 2 * LANES:3 * LANES])
        vws.append(kv_ref[pl.ds(off, tq), 3 * LANES:4 * LANES])
        wbs.append(wb_ref[:, cidx * tq:(cidx + 1) * tq] + jnp.where(tile >= 0, 0.0, NEG_INF))
    vw = jnp.concatenate(vws, axis=0)
    s_ws = [jnp.concatenate([_dot(q2s[g], kws[c], NT) + wbs[c] for c in range(n_tiles)], axis=1) for g in groups]
    e_ws = [jnp.exp(s - jnp.max(s, axis=1, keepdims=True)) for s in s_ws]
    p_ws = [e / jnp.sum(e, axis=1, keepdims=True) for e in e_ws]
    o_ws = [_dot(p_ws[g].astype(BF16), vw)[:, gsl[g]] for g in groups]
    pieces = []
    for g in groups:
        pieces += _combine_branches(gates, g, o_cs[g], o_ss[g], o_ws[g], tq)
    o_ref[...] = jnp.concatenate(pieces, axis=1)


def _nsa_prompt(pq, pg, gate_b, kc, kvb, m_mat, e_mat, tq):
    B, T, _ = pq.shape
    ngrp = kc.shape[1]
    n_s = T // SEL_BLOCK
    n_tiles = WINDOW // tq + 1
    kp = np.arange(n_tiles * tq)[None, :] - (n_tiles - 1) * tq
    tloc = (np.arange(C_GROUP * tq) % tq)[:, None]
    wbias = jnp.asarray(np.where((kp <= tloc) & (kp > tloc - WINDOW), 0.0, NEG_INF).astype(np.float32))
    kern = functools.partial(_nsa_prompt_kernel, n_s=n_s)
    return pl.pallas_call(
        kern,
        grid=(B, T // tq),
        in_specs=[
            pl.BlockSpec((None, tq, C_WIDTH), lambda b, i: (b, i, 0)),
            pl.BlockSpec((None, tq, LANES), lambda b, i: (b, i, 0)),
            pl.BlockSpec((1, LANES), lambda b, i: (0, 0)),
            pl.BlockSpec((None, ngrp, 4 * HEAD_DIM), lambda b, i: (b, 0, 0)),
            pl.BlockSpec((None, T, 4 * LANES), lambda b, i: (b, 0, 0)),
            pl.BlockSpec((None, LANES, T), lambda b, i: (b, 0, 0)),
            pl.BlockSpec((ngrp, LANES), lambda b, i: (0, 0)),
            pl.BlockSpec((T, LANES), lambda b, i: (0, 0)),
            pl.BlockSpec((C_GROUP * tq, n_tiles * tq), lambda b, i: (0, 0)),
        ],
        out_specs=pl.BlockSpec((None, tq, C_WIDTH), lambda b, i: (b, i, 0)),
        out_shape=jax.ShapeDtypeStruct((B, T, C_WIDTH), F32),
        compiler_params=_cparams(("arbitrary", "arbitrary")),
        name="nsa_prompt",
    )(pq, pg, gate_b, kc, kvb, jnp.swapaxes(kvb[:, :, LANES:2 * LANES], 1, 2), m_mat, jnp.transpose(e_mat), wbias)


def _nsa_sample_kernel(pt_ref, *refs, n_pages, nseq, past_len, n_s, n_c):
    (pq_ref, pg_ref, gb_ref, pcmp_ref, pslc_ref, pwin_ref, win_ref, pw0_ref, pw1_ref, cw_ref, m_ref, e_ref,
     o_ref) = refs[nseq * n_pages:]
    seqs = range(nseq)
    pages = [refs[s * n_pages:(s + 1) * n_pages] for s in seqs]
    tq = pq_ref.shape[1]
    rows = C_GROUP * tq
    W = 4 * HEAD_DIM
    ng = PAGE_SIZE // CMP_STRIDE
    ngrp = m_ref.shape[0]
    pw0 = pw0_ref[...]
    pw1 = pw1_ref[...]

    def tail_t(x):
        return jnp.transpose(jnp.concatenate([x, jnp.zeros((PAGE_SIZE - tq, x.shape[1]), F32)], axis=0))

    xs = [[pages[s][p][0:W, :] for p in range(n_pages)] + [tail_t(pcmp_ref[s])] for s in seqs]
    a_acc = jnp.zeros((nseq * 2 * W, ngrp), F32)
    for p0 in range(0, n_pages + 1, 2):
        xw = jnp.concatenate(
            [jnp.concatenate([jnp.concatenate([x * pw0, x * pw1], axis=0) for x in xs[s][p0:p0 + 2]], axis=1)
             for s in seqs], axis=0)
        kdim = xw.shape[1]
        sel = (jnp.right_shift(_iota((kdim, ngrp), 0), 4) + p0 * ng == _iota((kdim, ngrp), 1)).astype(BF16)
        a_acc = a_acc + _dot(xw.astype(BF16), sel)
    pooled_ts = [a_acc[s * 2 * W:s * 2 * W + W, :] + pltpu.roll(a_acc[s * 2 * W + W:(s + 1) * 2 * W, :], ngrp - 1, axis=1)
                 for s in seqs]
    kc_all = _mm(cw_ref[...], jnp.concatenate(pooled_ts, axis=1), 3)
    kc_ts = [kc_all[:, s * ngrp:(s + 1) * ngrp] for s in seqs]
    pslc_ts = [tail_t(pslc_ref[s]) for s in seqs]
    k_ts = [jnp.concatenate([pages[s][p][W:W + LANES, :].astype(BF16) for p in range(n_pages)]
                            + [pslc_ts[s][0:LANES].astype(BF16)], axis=1) for s in seqs]
    v_ts = [jnp.concatenate([pages[s][p][W + LANES:W + 2 * LANES, :].astype(BF16) for p in range(n_pages)]
                            + [pslc_ts[s][LANES:2 * LANES].astype(BF16)], axis=1) for s in seqs]
    nk = k_ts[0].shape[1]
    t_row = past_len + (_iota((rows, 1), 0) & (tq - 1))
    t_col = past_len + _iota((tq, 1), 0)
    n_idx = _iota((rows, ngrp), 1)
    cmask = (n_idx * CMP_STRIDE + (CMP_BLOCK - 1) <= t_row) & (n_idx < n_c)
    kpos = _iota((rows, nk), 1)
    nwb = win_ref.shape[2]
    pwin_ts = [tail_t(pwin_ref[s]) for s in seqs]
    kw_ts = [jnp.concatenate([win_ref[s, 0:LANES, :], pwin_ts[s][0:LANES]], axis=1).astype(BF16) for s in seqs]
    vw_ts = [jnp.concatenate([win_ref[s, LANES:2 * LANES, :], pwin_ts[s][LANES:2 * LANES]], axis=1).astype(BF16)
             for s in seqs]
    jj = _iota((rows, nwb + PAGE_SIZE), 1)
    tl = _iota((rows, 1), 0) & (tq - 1)
    wmask = (jj > tl + (nwb - WINDOW)) & (jj <= tl + nwb) & (jj < nwb + tq)

    chains = [(s, g) for s in seqs for g in range(C_KV_HEADS)]
    gsl = [slice(g * HEAD_DIM, (g + 1) * HEAD_DIM) for g in range(C_KV_HEADS)]
    qs = [_stack_heads(pq_ref[s], g) for s, g in chains]
    q2s = [_to_group_lanes(q, g).astype(BF16) for q, (s, g) in zip(qs, chains)]
    p_cs = [_masked_softmax(_mm(q, kc_ts[s][gsl[g], :], 3), cmask) for q, (s, g) in zip(qs, chains)]
    o_cs = [_mm(p, kc_ts[s][2 * HEAD_DIM + g * HEAD_DIM: 2 * HEAD_DIM + (g + 1) * HEAD_DIM, :], 1, NT)
            for p, (s, g) in zip(p_cs, chains)]
    psums = [functools.reduce(lambda a, b: a + b, [p[r * tq:(r + 1) * tq] for r in range(C_GROUP)]) for p in p_cs]
    sels = _select_blocks(psums, m_ref[...], t_col, n_s)
    e_mat = e_ref[...]
    smasks = [(_dot(jnp.concatenate([sel] * C_GROUP, axis=0).astype(BF16), e_mat) > 0.5) & (kpos <= t_row)
              for sel in sels]
    p_ss = [_masked_softmax(_dot(q2, k_ts[s]), m) for q2, m, (s, g) in zip(q2s, smasks, chains)]
    o_ss = [_dot(p.astype(BF16), v_ts[s], NT)[:, gsl[g]] for p, (s, g) in zip(p_ss, chains)]
    p_ws = [_masked_softmax(_dot(q2, kw_ts[s]), wmask) for q2, (s, g) in zip(q2s, chains)]
    o_ws = [_dot(p.astype(BF16), vw_ts[s], NT)[:, gsl[g]] for p, (s, g) in zip(p_ws, chains)]
    for s in seqs:
        gates = _sigmoid(pg_ref[s] + gb_ref[...])
        pieces = []
        for c, (cs, g) in enumerate(chains):
            if cs == s:
                pieces += _combine_branches(gates, g, o_cs[c], o_ss[c], o_ws[c], tq)
        o_ref[s] = jnp.concatenate(pieces, axis=1)


def _nsa_sample(layer, page_table, cache_t, pq, pg, gate_b, pcmp, pslc, pwin, win_t, pwt0, pwt1, cmpw_t, m_mat, e_mat):
    B, T, _ = pq.shape
    nseq = 4 if B % 4 == 0 else (2 if B % 2 == 0 else 1)
    n_pages = page_table.shape[1]
    past_len = n_pages * PAGE_SIZE
    lp = -(-(past_len + T) // SEL_BLOCK) * SEL_BLOCK
    n_s = lp // SEL_BLOCK
    n_c = lp // CMP_STRIDE - CMP_BLOCK // CMP_STRIDE + 1
    ngrp = m_mat.shape[0]
    nk = past_len + PAGE_SIZE
    nwb = win_t.shape[3]
    kern = functools.partial(_nsa_sample_kernel, n_pages=n_pages, nseq=nseq, past_len=past_len, n_s=n_s, n_c=n_c)
    page_specs = [
        pl.BlockSpec((None, None, 4 * LANES, PAGE_SIZE),
                     functools.partial(lambda b, pt, s, p: (layer, pt[b * nseq + s, p], 0, 0), s=s, p=p))
        for s in range(nseq) for p in range(n_pages)
    ]
    row = lambda w: pl.BlockSpec((nseq, T, w), lambda b, pt: (b, 0, 0))
    full = lambda a: pl.BlockSpec(a.shape, lambda b, pt: (0,) * a.ndim)
    grid_spec = pltpu.PrefetchScalarGridSpec(
        num_scalar_prefetch=1,
        grid=(B // nseq,),
        in_specs=page_specs + [
            row(C_WIDTH), row(LANES), full(gate_b), row(4 * HEAD_DIM), row(4 * HEAD_DIM), row(4 * HEAD_DIM),
            pl.BlockSpec((None, nseq, 4 * HEAD_DIM, nwb), lambda b, pt: (layer, b, 0, 0)),
            full(pwt0), full(pwt1), full(cmpw_t), full(m_mat), full(e_mat),
        ],
        out_specs=pl.BlockSpec((nseq, T, C_WIDTH), lambda b, pt: (b, 0, 0)),
    )
    return pl.pallas_call(
        kern,
        grid_spec=grid_spec,
        out_shape=jax.ShapeDtypeStruct((B, T, C_WIDTH), F32),
        compiler_params=_cparams(("arbitrary",)),
        name="nsa_sample",
    )(page_table, *([cache_t] * (nseq * n_pages)), pq, pg, gate_b, pcmp, pslc, pwin, win_t, pwt0, pwt1, cmpw_t, m_mat, e_mat)


def _nsa_consts(lp, n_keys):
    n_str = lp // CMP_STRIDE
    n_c = n_str - CMP_BLOCK // CMP_STRIDE + 1
    n_s = lp // SEL_BLOCK
    c0 = np.arange(n_str)[:, None] * CMP_STRIDE
    s0 = np.arange(LANES)[None, :] * SEL_BLOCK
    m = (c0 < s0 + SEL_BLOCK) & (c0 + CMP_BLOCK > s0) & (np.arange(n_str)[:, None] < n_c) & (np.arange(LANES)[None, :] < n_s)
    e = (np.arange(n_keys)[None, :] // SEL_BLOCK) == np.arange(LANES)[:, None]
    return m.astype(np.float32), e.astype(np.float32)


def _prep_mlstm(conv_w, conv_b, i_b, f_b, norm_g):
    cw8 = jnp.pad(conv_w, ((0, SUBLANES - MLSTM_CONV), (0, 0)))
    bias_if = jnp.pad(jnp.concatenate([i_b, f_b]), (0, LANES - 2 * B_HEADS)).reshape(1, LANES)
    return cw8, conv_b.reshape(1, 2 * B_WIDTH), bias_if, norm_g.reshape(1, B_WIDTH)


_A0, _B0, _Q0, _CMP0, _SLC0, _WIN0, _G0, _IF0, _PEND = 0, 1024, 2048, 2560, 2816, 3072, 3328, 3456, 3584
_PROJ_SPLITS = ((_A0, _B0), (_B0, _Q0), (_Q0, _CMP0), (_CMP0, _SLC0), (_SLC0, _WIN0), (_WIN0, _G0), (_G0, _IF0),
                (_IF0, _PEND))


def _pad_lanes(a, width):
    return jnp.pad(a, [(0, 0)] * (a.ndim - 1) + [(0, width - a.shape[-1])])


def _prep_layer(P, l):
    w_in = P["w_in"][l]
    wa, wb, wc = w_in[:, 0:1024], w_in[:, 1024:2056], w_in[:, 2056:3360]
    w_in_p = jnp.concatenate([
        wa, wb[:, 0:768], wb[:, 776:1032], wc[:, 0:512], wc[:, 512:1280],
        _pad_lanes(wc[:, 1280:1304], LANES), _pad_lanes(wb[:, 768:776], LANES)], axis=1).astype(BF16)
    up = P["ffn_up"][l]
    ffn_up_p = jnp.concatenate([_pad_lanes(up[:, 0:D_FF], D_FF_PAD), _pad_lanes(up[:, D_FF:], D_FF_PAD)], axis=1).astype(BF16)
    cw = P["nsa_cmp_w"][l]
    cmpw_bd = jnp.zeros((4 * HEAD_DIM, 4 * HEAD_DIM), F32)
    for kv in range(2):
        for g in range(C_KV_HEADS):
            o = (kv * C_KV_HEADS + g) * HEAD_DIM
            cmpw_bd = cmpw_bd.at[o:o + HEAD_DIM, o:o + HEAD_DIM].set(cw[kv, g])
    pool = P["nsa_pool_w"][l].reshape(CMP_BLOCK, 4 * HEAD_DIM)
    return dict(
        w_in=w_in_p,
        w_out=P["w_out"][l].astype(BF16),
        ffn_up=ffn_up_p,
        ffn_down=jnp.pad(P["ffn_down"][l], ((0, D_FF_PAD - D_FF), (0, 0))).astype(BF16),
        ffn_cw=jnp.pad(P["ffn_conv_w"][l], ((0, SUBLANES - FFN_CONV), (0, D_FF_PAD - D_FF))),
        ffn_cb=_pad_lanes(P["ffn_conv_b"][l].reshape(1, D_FF), D_FF_PAD),
        ln1=(P["ln_g"][l, 0].reshape(1, D_MODEL), P["ln_b"][l, 0].reshape(1, D_MODEL)),
        ln2=(P["ln_g"][l, 1].reshape(1, D_MODEL), P["ln_b"][l, 1].reshape(1, D_MODEL)),
        rwkv=_prep_rwkv(P["rwkv_mu"][l], P["rwkv_w0"][l], P["rwkv_w2"][l], P["rwkv_a0"][l], P["rwkv_a2"][l],
                        P["rwkv_g2"][l], P["rwkv_k_k"][l], P["rwkv_k_a"][l], P["rwkv_r_k"][l], P["rwkv_ln_g"][l],
                        P["rwkv_ln_b"][l]),
        mlstm=_prep_mlstm(P["mlstm_conv_w"][l], P["mlstm_conv_b"][l], P["mlstm_i_b"][l], P["mlstm_f_b"][l],
                          P["mlstm_norm_g"][l]),
        pool0=pool[0:CMP_STRIDE], pool1=pool[CMP_STRIDE:CMP_BLOCK], cmpw=cmpw_bd,
        gate_b=_pad_lanes(P["nsa_gate_b"][l].reshape(1, 3 * C_HEADS), LANES),
    )


def _rows8(state):
    return jnp.pad(state, ((0, 0), (SUBLANES - state.shape[1], 0), (0, 0)))


def _last_rows(prev, cur, k):
    if cur.shape[1] >= k:
        return cur[:, cur.shape[1] - k:]
    return jnp.concatenate([prev, cur], axis=1)[:, -k:]


def _trunk(x, mod, st, layers, nsa_fn, dense_tile, rec_tile):
    B, T, _ = x.shape
    dbb, dL = dense_tile
    rbb, rL = rec_tile
    new = {k: [] for k in ("nsa_kv", "win_kv", "rwkv", "rwkv_shift", "mlstm_C", "mlstm_n", "mlstm_m", "mlstm_conv", "ffn_conv")}
    for l, Lw in enumerate(layers):
        m = mod[l]
        pa, pb, pq, pcmp, pslc, pwin, pg, pif, kvb = _modmm(
            x, m, 0, 1, Lw["w_in"], _PROJ_SPLITS + ((_SLC0, _G0),), (False,) * 8 + (True,), dbb, dL, "in_proj")
        ya, rw_st = _rwkv(pa, _rows8(st["rwkv_shift"][l][:, None, :]), _state_to_lanes(st["rwkv"][l]), *Lw["rwkv"], rbb, rL)
        yb, c_st, n_st, m_st = _mlstm(pb, _rows8(st["mlstm_conv"][l]), pif, _state_to_lanes(st["mlstm_C"][l]),
                                      st["mlstm_n"][l], st["mlstm_m"][l], *Lw["mlstm"], rbb, rL)
        yc, win_new = nsa_fn(l, Lw, pq, pg, pcmp, pslc, pwin, kvb)
        x = _outproj(ya, yb, yc, x, m, Lw["w_out"], *Lw["ln1"], dbb, dL)
        (u,) = _modmm(x, m, 3, 4, Lw["ffn_up"], ((0, 2 * D_FF_PAD),), (False,), dbb, dL, "ffn_up")
        st8 = _rows8(_pad_lanes(st["ffn_conv"][l], D_FF_PAD))
        x = _ffn_down(u, st8, x, m, Lw["ffn_cw"], Lw["ffn_cb"], Lw["ffn_down"], *Lw["ln2"], dbb, dL)
        new["nsa_kv"].append(jnp.concatenate([pcmp, pslc], axis=-1).reshape(B, T, 4, C_KV_HEADS, HEAD_DIM))
        new["win_kv"].append(win_new)
        new["rwkv"].append(_state_from_lanes(rw_st, B))
        new["rwkv_shift"].append(pa[:, -1])
        new["mlstm_C"].append(_state_from_lanes(c_st, B))
        new["mlstm_n"].append(n_st)
        new["mlstm_m"].append(m_st)
        new["mlstm_conv"].append(_last_rows(st["mlstm_conv"][l], pb[:, :, 0:2 * B_WIDTH], MLSTM_CONV - 1))
        new["ffn_conv"].append(_last_rows(st["ffn_conv"][l], u[:, :, 0:D_FF], FFN_CONV - 1))
    return x, {k: jnp.stack(v) for k, v in new.items()}


def kernel(x_prompt, x_sample, c_prompt, c_sample, cache_nsa_kv, cache_win_kv, state_rwkv, state_rwkv_shift,
           state_mlstm_C, state_mlstm_n, state_mlstm_m, state_mlstm_conv, state_ffn_conv, page_table,
           w_in, w_out, ada_w, ada_b, ln_g, ln_b, rwkv_mu, rwkv_w0, rwkv_w2, rwkv_a0, rwkv_a2, rwkv_g2,
           rwkv_k_k, rwkv_k_a, rwkv_r_k, rwkv_ln_g, rwkv_ln_b, mlstm_conv_w, mlstm_conv_b, mlstm_i_b,
           mlstm_f_b, mlstm_norm_g, nsa_pool_w, nsa_cmp_w, nsa_gate_b, ffn_up, ffn_conv_w, ffn_conv_b, ffn_down):
    P = dict(w_in=w_in, w_out=w_out, ln_g=ln_g, ln_b=ln_b, rwkv_mu=rwkv_mu, rwkv_w0=rwkv_w0, rwkv_w2=rwkv_w2,
             rwkv_a0=rwkv_a0, rwkv_a2=rwkv_a2, rwkv_g2=rwkv_g2, rwkv_k_k=rwkv_k_k, rwkv_k_a=rwkv_k_a,
             rwkv_r_k=rwkv_r_k, rwkv_ln_g=rwkv_ln_g, rwkv_ln_b=rwkv_ln_b, mlstm_conv_w=mlstm_conv_w,
             mlstm_conv_b=mlstm_conv_b, mlstm_i_b=mlstm_i_b, mlstm_f_b=mlstm_f_b, mlstm_norm_g=mlstm_norm_g,
             nsa_pool_w=nsa_pool_w, nsa_cmp_w=nsa_cmp_w, nsa_gate_b=nsa_gate_b, ffn_up=ffn_up,
             ffn_conv_w=ffn_conv_w, ffn_conv_b=ffn_conv_b, ffn_down=ffn_down)
    Bp, Tp, _ = x_prompt.shape
    Bs, Ts, _ = x_sample.shape
    G, dh = C_KV_HEADS, HEAD_DIM
    layers = [_prep_layer(P, l) for l in range(DEPTH)]

    nb = -(-(Bp + Bs) // SUBLANES) * SUBLANES
    c_all = jnp.pad(jnp.concatenate([c_prompt, c_sample], axis=0), ((0, nb - Bp - Bs), (0, 0)))
    mod = _ada(c_all, ada_w, ada_b)
    mod_p = mod[:, 0:Bp].reshape(DEPTH, Bp, 1, 6 * D_MODEL)
    mod_s = mod[:, Bp:Bp + Bs].reshape(DEPTH, Bs, 1, 6 * D_MODEL)

    st_p = dict(
        rwkv=jnp.zeros((DEPTH, Bp, A_HEADS, dh, dh), F32), rwkv_shift=jnp.zeros((DEPTH, Bp, 4 * A_WIDTH), F32),
        mlstm_C=jnp.zeros((DEPTH, Bp, B_HEADS, dh, dh), F32), mlstm_n=jnp.zeros((DEPTH, Bp, B_HEADS, dh), F32),
        mlstm_m=jnp.zeros((DEPTH, Bp, B_HEADS), F32), mlstm_conv=jnp.zeros((DEPTH, Bp, MLSTM_CONV - 1, 2 * B_WIDTH), F32),
        ffn_conv=jnp.zeros((DEPTH, Bp, FFN_CONV - 1, D_FF), F32))
    m_p, e_p = _nsa_consts(Tp, Tp)
    e_p = jnp.asarray(e_p, BF16)
    tq = 128
    rs = 4 * tq

    def nsa_prompt(l, Lw, pq, pg, pcmp, pslc, pwin, kvb):
        kc = _compress(pcmp, jnp.tile(Lw["pool0"], (rs // CMP_STRIDE, 1)), jnp.tile(Lw["pool1"], (rs // CMP_STRIDE, 1)), Lw["cmpw"])
        yc = _nsa_prompt(pq, pg, Lw["gate_b"], kc, kvb, jnp.asarray(m_p), e_p, tq)
        return yc, pwin[:, -min(WINDOW, Tp):].reshape(Bp, min(WINDOW, Tp), 2, G, dh)

    y_prompt, new_p = _trunk(x_prompt, mod_p, st_p, layers, nsa_prompt, (1, 256), (Bp, ROWS // Bp))

    st_s = dict(rwkv=state_rwkv, rwkv_shift=state_rwkv_shift, mlstm_C=state_mlstm_C, mlstm_n=state_mlstm_n,
                mlstm_m=state_mlstm_m, mlstm_conv=state_mlstm_conv, ffn_conv=state_ffn_conv)
    n_pages = page_table.shape[1]
    past_len = n_pages * PAGE_SIZE
    lp = -(-(past_len + Ts) // SEL_BLOCK) * SEL_BLOCK
    ngrp = 2 * LANES
    assert (n_pages + 1) * (PAGE_SIZE // CMP_STRIDE) <= ngrp
    m_s, e_s = _nsa_consts(lp, past_len + PAGE_SIZE)
    m_s = jnp.asarray(np.pad(m_s, ((0, ngrp - m_s.shape[0]), (0, 0))))
    e_s = jnp.asarray(e_s, BF16)
    cache_t = jnp.transpose(cache_nsa_kv, (0, 1, 3, 4, 5, 2)).reshape(DEPTH, cache_nsa_kv.shape[1], 4 * G * dh, PAGE_SIZE)
    nwb = cache_win_kv.shape[2]
    win_t_all = jnp.transpose(cache_win_kv, (0, 1, 3, 4, 5, 2)).reshape(DEPTH, Bs, 2 * G * dh, nwb)
    reps = PAGE_SIZE // CMP_STRIDE

    def nsa_sample(l, Lw, pq, pg, pcmp, pslc, pwin, kvb):
        yc = _nsa_sample(l, page_table, cache_t, pq, pg, Lw["gate_b"], pcmp, pslc, pwin, win_t_all,
                         jnp.tile(Lw["pool0"].T, (1, reps)), jnp.tile(Lw["pool1"].T, (1, reps)),
                         Lw["cmpw"].T, m_s, e_s)
        return yc, pwin.reshape(Bs, Ts, 2, G, dh)

    y_sample, new_s = _trunk(x_sample, mod_s, st_s, layers, nsa_sample, (ROWS // Ts, Ts), (ROWS // Ts, Ts))
    keep = min(WINDOW, nwb + Ts)
    new_s["win_kv"] = jnp.concatenate([cache_win_kv, new_s["win_kv"]], axis=2)[:, :, nwb + Ts - keep:]

    return (y_prompt, y_sample,
            new_p["nsa_kv"], new_s["nsa_kv"], new_p["win_kv"], new_s["win_kv"],
            new_p["rwkv"], new_s["rwkv"], new_p["rwkv_shift"], new_s["rwkv_shift"],
            new_p["mlstm_C"], new_s["mlstm_C"], new_p["mlstm_n"], new_s["mlstm_n"],
            new_p["mlstm_m"], new_s["mlstm_m"], new_p["mlstm_conv"], new_s["mlstm_conv"],
            new_p["ffn_conv"], new_s["ffn_conv"])
```
